```python
import jax, jax.numpy as jnp
from jax import lax
import numpy as np

D_MODEL = 2048
BATCH = 8
SEQ = 4096
DEPTH = 4

GRID_W = 64
MIX_WIDTH = D_MODEL
CONV_WIDTH = MIX_WIDTH // 2
NA_HEADS = 16
NA_HEAD_DIM = (MIX_WIDTH - CONV_WIDTH) // NA_HEADS
NA_WIDTH = NA_HEADS * NA_HEAD_DIM
CONV_KERNEL = 31
WIN_ROWS_MAX = 8
WIN_COLS = 16
D_FF = 4 * D_MODEL
IN_COLS = 2 * CONV_WIDTH + 3 * NA_WIDTH
RMS_EPS = 1e-6
LN_EPS = 1e-5
NEG_INF = -1e30

kernel_name = "hybrid_conv_natten_encoder"


def rms_norm(x, g):
    xf = x.astype(jnp.float32)
    y = xf * lax.rsqrt(jnp.mean(xf * xf, axis=-1, keepdims=True) + RMS_EPS)
    return (y * g.astype(jnp.float32)).astype(x.dtype)


def layer_norm(x, g, b):
    xf = x.astype(jnp.float32)
    mu = jnp.mean(xf, axis=-1, keepdims=True)
    xc = xf - mu
    var = jnp.mean(xc * xc, axis=-1, keepdims=True)
    y = xc * lax.rsqrt(var + LN_EPS) * g.astype(jnp.float32) + b.astype(jnp.float32)
    return y.astype(x.dtype)


def conformer_conv_group(a, gate, w_dw, b_dw, ln_g, ln_b):
    u = a * jax.nn.sigmoid(gate)
    u = lax.conv_general_dilated(
        u, w_dw[:, None, :].astype(u.dtype),
        window_strides=(1,),
        padding=[(CONV_KERNEL // 2, CONV_KERNEL // 2)],
        dimension_numbers=("NWC", "WIO", "NWC"),
        feature_group_count=CONV_WIDTH,
    ) + b_dw.astype(u.dtype)
    return jax.nn.silu(layer_norm(u, ln_g, ln_b))


def neighbourhood_attention_group(q, k, v, rpb):
    B, T, _ = q.shape
    rows = T // GRID_W
    kr = min(WIN_ROWS_MAX, rows)
    r = np.arange(rows)
    c = np.arange(GRID_W)
    row_start = np.clip(r - kr // 2, 0, rows - kr)
    row_idx = row_start[:, None] + np.arange(kr)[None, :]
    col_start = np.clip(c - WIN_COLS // 2, 0, GRID_W - WIN_COLS)
    col_mask = (c[None, :] >= col_start[:, None]) & (c[None, :] < col_start[:, None] + WIN_COLS)
    dr = row_idx - r[:, None] + (WIN_ROWS_MAX - 1)
    dc = np.clip(c[None, :] - c[:, None], -(WIN_COLS - 1), WIN_COLS - 1) + (WIN_COLS - 1)
    bias = rpb[:, dr[:, None, :, None], dc[None, :, None, :]].astype(jnp.float32)

    def to_grid(t):
        return t.reshape(B, rows, GRID_W, NA_HEADS, NA_HEAD_DIM)

    qg, kg, vg = to_grid(q), to_grid(k), to_grid(v)
    k_rows = kg[:, row_idx]
    v_rows = vg[:, row_idx]
    s = jnp.einsum("brwhd,brkvhd->bhrwkv", qg, k_rows,
                   preferred_element_type=jnp.float32) * (NA_HEAD_DIM ** -0.5)
    s = jnp.where(col_mask[:, None, :], s + bias, NEG_INF)
    p = jax.nn.softmax(s.reshape(B, NA_HEADS, rows, GRID_W, kr * GRID_W), axis=-1)
    p = p.reshape(s.shape).astype(v.dtype)
    o = jnp.einsum("bhrwkv,brkvhd->brwhd", p, v_rows)
    return o.reshape(B, T, NA_WIDTH)


def _fwd_setup_inputs(seed: int = 0) -> dict:
    key = jax.random.key(seed)
    ks = jax.random.split(key, 16)
    f32 = jnp.float32
    x = jax.random.normal(ks[0], (BATCH, SEQ, D_MODEL), f32)
    w_in = jax.random.normal(ks[1], (DEPTH, D_MODEL, IN_COLS), f32) * D_MODEL ** -0.5
    w_dw = jax.random.normal(ks[2], (DEPTH, CONV_KERNEL, CONV_WIDTH), f32) * CONV_KERNEL ** -0.5
    b_dw = jax.random.normal(ks[3], (DEPTH, CONV_WIDTH), f32) * 0.01
    conv_ln_g = 1.0 + 0.05 * jax.random.normal(ks[4], (DEPTH, CONV_WIDTH), f32)
    conv_ln_b = 0.01 * jax.random.normal(ks[5], (DEPTH, CONV_WIDTH), f32)
    rpb = 0.02 * jax.random.normal(ks[6], (DEPTH, NA_HEADS, 2 * WIN_ROWS_MAX - 1, 2 * WIN_COLS - 1), f32)
    w_out = jax.random.normal(ks[7], (DEPTH, MIX_WIDTH, D_MODEL), f32) * MIX_WIDTH ** -0.5
    w_up = jax.random.normal(ks[8], (DEPTH, D_MODEL, D_FF), f32) * D_MODEL ** -0.5
    w_down = jax.random.normal(ks[9], (DEPTH, D_FF, D_MODEL), f32) * D_FF ** -0.5
    pre_mix_g = 1.0 + 0.05 * jax.random.normal(ks[10], (DEPTH, D_MODEL), f32)
    post_mix_g = 1.0 + 0.05 * jax.random.normal(ks[11], (DEPTH, D_MODEL), f32)
    pre_mlp_g = 1.0 + 0.05 * jax.random.normal(ks[12], (DEPTH, D_MODEL), f32)
    post_mlp_g = 1.0 + 0.05 * jax.random.normal(ks[13], (DEPTH, D_MODEL), f32)
    return {"x": x, "w_in": w_in, "w_dw": w_dw, "b_dw": b_dw, "conv_ln_g": conv_ln_g,
            "conv_ln_b": conv_ln_b, "rpb": rpb, "w_out": w_out, "w_up": w_up, "w_down": w_down,
            "pre_mix_g": pre_mix_g, "post_mix_g": post_mix_g, "pre_mlp_g": pre_mlp_g,
            "post_mlp_g": post_mlp_g}


def _fwd_reference(x, w_in, w_dw, b_dw, conv_ln_g, conv_ln_b, rpb, w_out, w_up, w_down,
              pre_mix_g, post_mix_g, pre_mlp_g, post_mlp_g):
    splits = [CONV_WIDTH, 2 * CONV_WIDTH, 2 * CONV_WIDTH + NA_WIDTH, 2 * CONV_WIDTH + 2 * NA_WIDTH]
    for l in range(DEPTH):
        h = rms_norm(x, pre_mix_g[l])
        proj = h @ w_in[l]
        a, gate, q, k, v = jnp.split(proj, splits, axis=-1)
        yc = conformer_conv_group(a, gate, w_dw[l], b_dw[l], conv_ln_g[l], conv_ln_b[l])
        ya = neighbourhood_attention_group(q, k, v, rpb[l])
        mix = jnp.concatenate([yc, ya], axis=-1) @ w_out[l]
        x = x + rms_norm(mix, post_mix_g[l])
        h = rms_norm(x, pre_mlp_g[l])
        f = jnp.square(jax.nn.relu(h @ w_up[l])) @ w_down[l]
        x = x + rms_norm(f, post_mlp_g[l])
    return x


import jax as _jax
import jax.numpy as _jnp

TWIN_FORMAT = 'train_step'
FWD_PARAMS = ['x', 'w_in', 'w_dw', 'b_dw', 'conv_ln_g', 'conv_ln_b', 'rpb', 'w_out', 'w_up', 'w_down', 'pre_mix_g', 'post_mix_g', 'pre_mlp_g', 'post_mlp_g']
TWIN_WEIGHTS = ['w_in', 'w_dw', 'b_dw', 'conv_ln_g', 'conv_ln_b', 'rpb', 'w_out', 'w_up', 'w_down', 'pre_mix_g', 'post_mix_g', 'pre_mlp_g', 'post_mlp_g']
TWIN_DIFF_INPUT = 'x'
TWIN_INPUTS = ['x', 'w_in', 'w_dw', 'b_dw', 'conv_ln_g', 'conv_ln_b', 'rpb', 'w_out', 'w_up', 'w_down', 'pre_mix_g', 'post_mix_g', 'pre_mlp_g', 'post_mlp_g', 'loss_target', 'm_w_in', 'm_w_dw', 'm_b_dw', 'm_conv_ln_g', 'm_conv_ln_b', 'm_rpb', 'm_w_out', 'm_w_up', 'm_w_down', 'm_pre_mix_g', 'm_post_mix_g', 'm_pre_mlp_g', 'm_post_mlp_g', 'v_w_in', 'v_w_dw', 'v_b_dw', 'v_conv_ln_g', 'v_conv_ln_b', 'v_rpb', 'v_w_out', 'v_w_up', 'v_w_down', 'v_pre_mix_g', 'v_post_mix_g', 'v_pre_mlp_g', 'v_post_mlp_g']
TWIN_OUTPUTS = ['loss', 'grad_x', 'grad_w_in', 'grad_w_dw', 'grad_b_dw', 'grad_conv_ln_g', 'grad_conv_ln_b', 'grad_rpb', 'grad_w_out', 'grad_w_up', 'grad_w_down', 'grad_pre_mix_g', 'grad_post_mix_g', 'grad_pre_mlp_g', 'grad_post_mlp_g', 'delta_w_in', 'delta_w_dw', 'delta_b_dw', 'delta_conv_ln_g', 'delta_conv_ln_b', 'delta_rpb', 'delta_w_out', 'delta_w_up', 'delta_w_down', 'delta_pre_mix_g', 'delta_post_mix_g', 'delta_pre_mlp_g', 'delta_post_mlp_g', 'new_m_w_in', 'new_m_w_dw', 'new_m_b_dw', 'new_m_conv_ln_g', 'new_m_conv_ln_b', 'new_m_rpb', 'new_m_w_out', 'new_m_w_up', 'new_m_w_down', 'new_m_pre_mix_g', 'new_m_post_mix_g', 'new_m_pre_mlp_g', 'new_m_post_mlp_g', 'new_v_w_in', 'new_v_w_dw', 'new_v_b_dw', 'new_v_conv_ln_g', 'new_v_conv_ln_b', 'new_v_rpb', 'new_v_w_out', 'new_v_w_up', 'new_v_w_down', 'new_v_pre_mix_g', 'new_v_post_mix_g', 'new_v_pre_mlp_g', 'new_v_post_mlp_g']
TWIN_LEAF_KINDS = {'loss': 'loss', 'grad_x': 'grad_x', 'grad_w_in': 'grad_w', 'grad_w_dw': 'grad_w', 'grad_b_dw': 'grad_w', 'grad_conv_ln_g': 'grad_w', 'grad_conv_ln_b': 'grad_w', 'grad_rpb': 'grad_w', 'grad_w_out': 'grad_w', 'grad_w_up': 'grad_w', 'grad_w_down': 'grad_w', 'grad_pre_mix_g': 'grad_w', 'grad_post_mix_g': 'grad_w', 'grad_pre_mlp_g': 'grad_w', 'grad_post_mlp_g': 'grad_w', 'delta_w_in': 'delta_w', 'delta_w_dw': 'delta_w', 'delta_b_dw': 'delta_w', 'delta_conv_ln_g': 'delta_w', 'delta_conv_ln_b': 'delta_w', 'delta_rpb': 'delta_w', 'delta_w_out': 'delta_w', 'delta_w_up': 'delta_w', 'delta_w_down': 'delta_w', 'delta_pre_mix_g': 'delta_w', 'delta_post_mix_g': 'delta_w', 'delta_pre_mlp_g': 'delta_w', 'delta_post_mlp_g': 'delta_w', 'new_m_w_in': 'new_m', 'new_m_w_dw': 'new_m', 'new_m_b_dw': 'new_m', 'new_m_conv_ln_g': 'new_m', 'new_m_conv_ln_b': 'new_m', 'new_m_rpb': 'new_m', 'new_m_w_out': 'new_m', 'new_m_w_up': 'new_m', 'new_m_w_down': 'new_m', 'new_m_pre_mix_g': 'new_m', 'new_m_post_mix_g': 'new_m', 'new_m_pre_mlp_g': 'new_m', 'new_m_post_mlp_g': 'new_m', 'new_v_w_in': 'new_v', 'new_v_w_dw': 'new_v', 'new_v_b_dw': 'new_v', 'new_v_conv_ln_g': 'new_v', 'new_v_conv_ln_b': 'new_v', 'new_v_rpb': 'new_v', 'new_v_w_out': 'new_v', 'new_v_w_up': 'new_v', 'new_v_w_down': 'new_v', 'new_v_pre_mix_g': 'new_v', 'new_v_post_mix_g': 'new_v', 'new_v_pre_mlp_g': 'new_v', 'new_v_post_mlp_g': 'new_v'}


def _forward(args):
    return _fwd_reference(*[args[k] for k in FWD_PARAMS])


def _output_shape():
    def fwd():
        inp = _fwd_setup_inputs(0)
        return _fwd_reference(*[inp[k] for k in FWD_PARAMS])
    out = _jax.eval_shape(fwd)
    return out.shape, out.dtype

N_MICROBATCH = 1
ADAM_LR = 0.001
ADAM_B1 = 0.9
ADAM_B2 = 0.999
ADAM_EPS = 1e-08
ADAM_WD = 0.01
ADAM_STEP = 10
PER_EXAMPLE_BATCH_AXIS = {'x': 0, 'loss_target': 0}
SHARED_INPUTS = []
_WEIGHT_DTYPES = {'w_in': _jnp.float32, 'w_dw': _jnp.float32, 'b_dw': _jnp.float32, 'conv_ln_g': _jnp.float32, 'conv_ln_b': _jnp.float32, 'rpb': _jnp.float32, 'w_out': _jnp.float32, 'w_up': _jnp.float32, 'w_down': _jnp.float32, 'pre_mix_g': _jnp.float32, 'post_mix_g': _jnp.float32, 'pre_mlp_g': _jnp.float32, 'post_mlp_g': _jnp.float32}
MOMENT_SCALE = {'w_in': 9.417557e+00, 'w_dw': 1.041856e+01, 'b_dw': 7.383557e+01, 'conv_ln_g': 2.869072e+01, 'conv_ln_b': 3.922870e+01, 'rpb': 6.468372e-02, 'w_out': 1.851637e+01, 'w_up': 4.610630e+00, 'w_down': 1.650521e+01, 'pre_mix_g': 1.512185e+01, 'post_mix_g': 2.544416e+01, 'pre_mlp_g': 9.198317e+00, 'post_mlp_g': 2.369890e+01}


def _to_microbatches(a, axis):
    t = _jnp.moveaxis(a, axis, 0)
    t = t.reshape((N_MICROBATCH, t.shape[0] // N_MICROBATCH) + t.shape[1:])
    return _jnp.moveaxis(t, 1, axis + 1)


def setup_inputs(seed: int = 0) -> dict:
    inp = _fwd_setup_inputs(seed)
    key = _jax.random.fold_in(_jax.random.key(seed), 7919)
    shape, _ = _output_shape()
    out = dict(inp)
    out["loss_target"] = _jax.random.normal(_jax.random.fold_in(key, 0), shape, _jnp.float32)
    for i, name in enumerate(TWIN_WEIGHTS):
        w = inp[name].astype(_jnp.float32)
        if MOMENT_SCALE is None:
            s = _jnp.sqrt(_jnp.mean(_jnp.square(w)) + 1e-30)
        else:
            s = MOMENT_SCALE[name]
        km, kv = _jax.random.split(_jax.random.fold_in(key, i + 1))
        out[name] = w
        out["m_" + name] = s * _jax.random.normal(km, w.shape, _jnp.float32)
        out["v_" + name] = (s * s) * _jax.random.uniform(kv, w.shape, _jnp.float32, 0.5, 1.5)
    if N_MICROBATCH > 1:
        for name, axis in PER_EXAMPLE_BATCH_AXIS.items():
            out[name] = _to_microbatches(out[name], axis)
    return {'x': out['x'], 'w_in': out['w_in'], 'w_dw': out['w_dw'], 'b_dw': out['b_dw'], 'conv_ln_g': out['conv_ln_g'], 'conv_ln_b': out['conv_ln_b'], 'rpb': out['rpb'], 'w_out': out['w_out'], 'w_up': out['w_up'], 'w_down': out['w_down'], 'pre_mix_g': out['pre_mix_g'], 'post_mix_g': out['post_mix_g'], 'pre_mlp_g': out['pre_mlp_g'], 'post_mlp_g': out['post_mlp_g'], 'loss_target': out['loss_target'], 'm_w_in': out['m_w_in'], 'm_w_dw': out['m_w_dw'], 'm_b_dw': out['m_b_dw'], 'm_conv_ln_g': out['m_conv_ln_g'], 'm_conv_ln_b': out['m_conv_ln_b'], 'm_rpb': out['m_rpb'], 'm_w_out': out['m_w_out'], 'm_w_up': out['m_w_up'], 'm_w_down': out['m_w_down'], 'm_pre_mix_g': out['m_pre_mix_g'], 'm_post_mix_g': out['m_post_mix_g'], 'm_pre_mlp_g': out['m_pre_mlp_g'], 'm_post_mlp_g': out['m_post_mlp_g'], 'v_w_in': out['v_w_in'], 'v_w_dw': out['v_w_dw'], 'v_b_dw': out['v_b_dw'], 'v_conv_ln_g': out['v_conv_ln_g'], 'v_conv_ln_b': out['v_conv_ln_b'], 'v_rpb': out['v_rpb'], 'v_w_out': out['v_w_out'], 'v_w_up': out['v_w_up'], 'v_w_down': out['v_w_down'], 'v_pre_mix_g': out['v_pre_mix_g'], 'v_post_mix_g': out['v_post_mix_g'], 'v_pre_mlp_g': out['v_pre_mlp_g'], 'v_post_mlp_g': out['v_post_mlp_g']}


def _loss(weights, diff, rest, loss_target):
    with _jax.named_scope("forward"):
        args = {**rest, TWIN_DIFF_INPUT: diff, **{k: w.astype(_WEIGHT_DTYPES[k]) for k, w in weights.items()}}
        y = _forward(args)
    with _jax.named_scope("loss_head"):
        err = _jnp.square(y.astype(_jnp.float32) - loss_target)
        return 0.5 * _jnp.sum(_jnp.mean(err, axis=-1)) if err.ndim else 0.5 * err


def _adamw(w, g, m, v):
    m = ADAM_B1 * m + (1.0 - ADAM_B1) * g
    v = ADAM_B2 * v + (1.0 - ADAM_B2) * _jnp.square(g)
    m_hat = m / (1.0 - ADAM_B1 ** ADAM_STEP)
    v_hat = v / (1.0 - ADAM_B2 ** ADAM_STEP)
    delta = -ADAM_LR * (m_hat / (_jnp.sqrt(v_hat) + ADAM_EPS) + ADAM_WD * w)
    return delta, m, v


def reference(x, w_in, w_dw, b_dw, conv_ln_g, conv_ln_b, rpb, w_out, w_up, w_down, pre_mix_g, post_mix_g, pre_mlp_g, post_mlp_g, loss_target, m_w_in, m_w_dw, m_b_dw, m_conv_ln_g, m_conv_ln_b, m_rpb, m_w_out, m_w_up, m_w_down, m_pre_mix_g, m_post_mix_g, m_pre_mlp_g, m_post_mlp_g, v_w_in, v_w_dw, v_b_dw, v_conv_ln_g, v_conv_ln_b, v_rpb, v_w_out, v_w_up, v_w_down, v_pre_mix_g, v_post_mix_g, v_pre_mlp_g, v_post_mlp_g):
    given = dict(x=x, w_in=w_in, w_dw=w_dw, b_dw=b_dw, conv_ln_g=conv_ln_g, conv_ln_b=conv_ln_b, rpb=rpb, w_out=w_out, w_up=w_up, w_down=w_down, pre_mix_g=pre_mix_g, post_mix_g=post_mix_g, pre_mlp_g=pre_mlp_g, post_mlp_g=post_mlp_g, loss_target=loss_target, m_w_in=m_w_in, m_w_dw=m_w_dw, m_b_dw=m_b_dw, m_conv_ln_g=m_conv_ln_g, m_conv_ln_b=m_conv_ln_b, m_rpb=m_rpb, m_w_out=m_w_out, m_w_up=m_w_up, m_w_down=m_w_down, m_pre_mix_g=m_pre_mix_g, m_post_mix_g=m_post_mix_g, m_pre_mlp_g=m_pre_mlp_g, m_post_mlp_g=m_post_mlp_g, v_w_in=v_w_in, v_w_dw=v_w_dw, v_b_dw=v_b_dw, v_conv_ln_g=v_conv_ln_g, v_conv_ln_b=v_conv_ln_b, v_rpb=v_rpb, v_w_out=v_w_out, v_w_up=v_w_up, v_w_down=v_w_down, v_pre_mix_g=v_pre_mix_g, v_post_mix_g=v_post_mix_g, v_pre_mlp_g=v_pre_mlp_g, v_post_mlp_g=v_post_mlp_g)
    weights = {n: given[n] for n in TWIN_WEIGHTS}
    shared = {n: given[n] for n in SHARED_INPUTS}
    per_example = {n: given[n] for n in ['x']}
    grad_fn = _jax.value_and_grad(_loss, argnums=(0, 1))

    def one_microbatch(ex, loss_target):
        ex = dict(ex)
        diff = ex.pop(TWIN_DIFF_INPUT)
        return grad_fn(weights, diff, {**shared, **ex}, loss_target)

    if N_MICROBATCH == 1:
        loss, (grad_w, grad_x) = one_microbatch(per_example, given["loss_target"])
    else:
        def body(carry, xs):
            loss_sum, grad_sum = carry
            l_k, (gw_k, gx_k) = one_microbatch(xs[0], xs[1])
            with _jax.named_scope("update"):
                return (loss_sum + l_k, _jax.tree.map(_jnp.add, grad_sum, gw_k)), gx_k

        init = (_jnp.zeros((), _jnp.float32), _jax.tree.map(_jnp.zeros_like, weights))
        (loss, grad_w), grad_x = _jax.lax.scan(body, init, (per_example, given["loss_target"]))
    with _jax.named_scope("update"):
        delta_w, new_m, new_v = {}, {}, {}
        for n in TWIN_WEIGHTS:
            delta_w[n], new_m[n], new_v[n] = _adamw(weights[n], grad_w[n], given["m_" + n], given["v_" + n])
    return (loss, grad_x, *[grad_w[n] for n in TWIN_WEIGHTS], *[delta_w[n] for n in TWIN_WEIGHTS],
            *[new_m[n] for n in TWIN_WEIGHTS], *[new_v[n] for n in TWIN_WEIGHTS])
```

```python
import functools

import numpy as np
import jax
import jax.numpy as jnp
from jax import lax
from jax.experimental import pallas as pl
from jax.experimental.pallas import tpu as pltpu

F32 = jnp.float32
BF16 = jnp.bfloat16

D_MODEL = 2048
CONV_W = 1024
NA_W = 1024
N_HEADS = 16
HEAD_D = 64
GRID_W = 64
WIN_ROWS = 8
WIN_COLS = 16
CONV_K = 31
D_FF = 4 * D_MODEL
IN_COLS = 2 * CONV_W + 3 * NA_W
RMS_EPS = 1e-6
LN_EPS = 1e-5
NEG_INF = -1e30
N_CHIPS = 4
N_DEV = 8
HALO = 16
BAND = WIN_ROWS * GRID_W

ADAM_LR = 0.001
ADAM_B1 = 0.9
ADAM_B2 = 0.999
ADAM_EPS = 1e-08
ADAM_WD = 0.01
ADAM_STEP = 10

VMEM_LIMIT = 48 * 1024 * 1024
MESH = pl.DeviceIdType.MESH
ANY = pl.BlockSpec(memory_space=pl.ANY)


def _params(*sem):
    return pltpu.CompilerParams(dimension_semantics=sem, vmem_limit_bytes=VMEM_LIMIT)


def _tile(n, pref):
    for t in pref:
        if n % t == 0:
            return t
    return n


def _mm_nn(a, w, *, out_dtype, relu2=False, name):
    m, k = a.shape
    s, _, n = w.shape
    tm = _tile(m, (1024, 512, 256))
    tn = _tile(n, (1024, 1280, 512))
    tk = _tile(k, (512,))
    nps = n // tn
    nk = k // tk

    def body(a_ref, w_ref, *rest):
        outs, acc = rest[:-1], rest[-1]
        kk = pl.program_id(2)

        @pl.when(kk == 0)
        def _():
            acc[...] = jnp.zeros_like(acc)

        acc[...] += jnp.dot(a_ref[...], w_ref[...], preferred_element_type=F32)

        @pl.when(kk == nk - 1)
        def _():
            r = acc[...]
            if relu2:
                outs[0][...] = r.astype(outs[0].dtype)
                p = jnp.maximum(r, 0.0)
                outs[1][...] = (p * p).astype(outs[1].dtype)
            else:
                outs[0][...] = r.astype(outs[0].dtype)

    o_spec = pl.BlockSpec((tm, tn), lambda i, j, kk: (i, j))
    o_shape = jax.ShapeDtypeStruct((m, s * n), out_dtype)
    return pl.pallas_call(
        body, name=name,
        grid=(m // tm, s * nps, nk),
        in_specs=[pl.BlockSpec((tm, tk), lambda i, j, kk: (i, kk)),
                  pl.BlockSpec((None, tk, tn), lambda i, j, kk: (j // nps, kk, j % nps))],
        out_specs=(o_spec, o_spec) if relu2 else o_spec,
        out_shape=(o_shape, o_shape) if relu2 else o_shape,
        scratch_shapes=[pltpu.VMEM((tm, tn), F32)],
        compiler_params=_params("parallel", "parallel", "arbitrary"),
    )(a, w)


def _mm_nt(dy, w, *, out_dtype, up=None, name):
    m = dy.shape[0]
    s, k, n = w.shape
    tm = _tile(m, (1024, 512, 256))
    tko = _tile(k, (1024, 512))
    tn = _tile(n, (512, 640))
    nps = n // tn
    nn = s * nps

    def body(dy_ref, w_ref, *rest):
        if up is None:
            o_ref, acc = rest
        else:
            up_ref, o_ref, acc = rest
        kk = pl.program_id(2)

        @pl.when(kk == 0)
        def _():
            acc[...] = jnp.zeros_like(acc)

        acc[...] += lax.dot_general(dy_ref[...], w_ref[...], (((1,), (1,)), ((), ())),
                                    preferred_element_type=F32)

        @pl.when(kk == nn - 1)
        def _():
            r = acc[...]
            if up is not None:
                r = r * (2.0 * jnp.maximum(up_ref[...].astype(F32), 0.0))
            o_ref[...] = r.astype(o_ref.dtype)

    in_specs = [pl.BlockSpec((tm, tn), lambda i, j, kk: (i, kk)),
                pl.BlockSpec((None, tko, tn), lambda i, j, kk: (kk // nps, j, kk % nps))]
    args = [dy, w]
    if up is not None:
        in_specs.append(pl.BlockSpec((tm, tko), lambda i, j, kk: (i, j)))
        args.append(up)
    return pl.pallas_call(
        body, name=name,
        grid=(m // tm, k // tko, nn),
        in_specs=in_specs,
        out_specs=pl.BlockSpec((tm, tko), lambda i, j, kk: (i, j)),
        out_shape=jax.ShapeDtypeStruct((m, k), out_dtype),
        scratch_shapes=[pltpu.VMEM((tm, tko), F32)],
        compiler_params=_params("parallel", "parallel", "arbitrary"),
    )(*args)


def _mm_tn(a, dy, s, *, name):
    m, k = a.shape
    n = dy.shape[1] // s
    tk = _tile(k, (1024, 512))
    tn = _tile(n, (1024, 1280, 512))
    tm = _tile(m, (512,))
    nps = n // tn
    nm = m // tm

    def body(a_ref, dy_ref, o_ref, acc):
        mm = pl.program_id(2)

        @pl.when(mm == 0)
        def _():
            acc[...] = jnp.zeros_like(acc)

        acc[...] += lax.dot_general(a_ref[...], dy_ref[...], (((0,), (0,)), ((), ())),
                                    preferred_element_type=F32)

        @pl.when(mm == nm - 1)
        def _():
            o_ref[...] = acc[...].astype(o_ref.dtype)

    return pl.pallas_call(
        body, name=name,
        grid=(k // tk, s * nps, nm),
        in_specs=[pl.BlockSpec((tm, tk), lambda i, j, mm: (mm, i)),
                  pl.BlockSpec((tm, tn), lambda i, j, mm: (mm, j))],
        out_specs=pl.BlockSpec((None, tk, tn), lambda i, j, mm: (j // nps, i, j % nps)),
        out_shape=jax.ShapeDtypeStruct((s, k, n), BF16),
        scratch_shapes=[pltpu.VMEM((tk, tn), F32)],
        compiler_params=_params("parallel", "parallel", "arbitrary"),
    )(a, dy)


ROWS = 256


def _row_spec(d):
    return pl.BlockSpec((ROWS, d), lambda i: (i, 0))


def _vec_spec(d):
    return pl.BlockSpec((1, d), lambda i: (0, 0))


def _rstd(v):
    return lax.rsqrt(jnp.mean(v * v, axis=-1, keepdims=True) + RMS_EPS)


def _rms_fwd(x, g, *, name):
    t, d = x.shape

    def body(x_ref, g_ref, h_ref):
        xv = x_ref[...]
        h_ref[...] = ((xv * _rstd(xv)) * g_ref[...]).astype(BF16)

    return pl.pallas_call(
        body, name=name, grid=(t // ROWS,),
        in_specs=[_row_spec(d), _vec_spec(d)],
        out_specs=_row_spec(d),
        out_shape=jax.ShapeDtypeStruct((t, d), BF16),
        compiler_params=_params("parallel"),
    )(x, g)


def _resid_rms(res, y, g_post, g_next, *, name):
    t, d = res.shape
    with_next = g_next is not None

    def body(res_ref, y_ref, gp_ref, *rest):
        yv = y_ref[...]
        xn = res_ref[...] + (yv * _rstd(yv)) * gp_ref[...]
        if with_next:
            gn_ref, xo_ref, h_ref = rest
            h_ref[...] = ((xn * _rstd(xn)) * gn_ref[...]).astype(BF16)
        else:
            (xo_ref,) = rest
        xo_ref[...] = xn

    in_specs = [_row_spec(d), _row_spec(d), _vec_spec(d)]
    args = [res, y, g_post]
    out_specs = [_row_spec(d)]
    out_shape = [jax.ShapeDtypeStruct((t, d), F32)]
    if with_next:
        in_specs.append(_vec_spec(d))
        args.append(g_next)
        out_specs.append(_row_spec(d))
        out_shape.append(jax.ShapeDtypeStruct((t, d), BF16))
    return pl.pallas_call(
        body, name=name, grid=(t // ROWS,),
        in_specs=in_specs, out_specs=tuple(out_specs), out_shape=tuple(out_shape),
        compiler_params=_params("parallel"),
    )(*args)


def _rms_bwd(dy, xin, g, res, *, out_dtype, name):
    t, d = xin.shape
    with_res = res is not None

    def body(dy_ref, x_ref, g_ref, *rest):
        if with_res:
            res_ref, dx_ref, dg_ref = rest
        else:
            dx_ref, dg_ref = rest
        xv = x_ref[...]
        r = _rstd(xv)
        nrm = xv * r
        dyv = dy_ref[...]
        dn = dyv * g_ref[...]
        dx = r * (dn - nrm * jnp.mean(dn * nrm, axis=-1, keepdims=True))
        if with_res:
            dx = dx + res_ref[...]
        dx_ref[...] = dx.astype(dx_ref.dtype)

        @pl.when(pl.program_id(0) == 0)
        def _():
            dg_ref[...] = jnp.zeros_like(dg_ref)

        dg_ref[...] += jnp.sum(dyv * nrm, axis=0, keepdims=True)

    in_specs = [_row_spec(d), _row_spec(d), _vec_spec(d)]
    args = [dy, xin, g]
    if with_res:
        in_specs.append(_row_spec(d))
        args.append(res)
    return pl.pallas_call(
        body, name=name, grid=(t // ROWS,),
        in_specs=in_specs,
        out_specs=(_row_spec(d), _vec_spec(d)),
        out_shape=(jax.ShapeDtypeStruct((t, d), out_dtype), jax.ShapeDtypeStruct((1, d), F32)),
        compiler_params=_params("arbitrary"),
    )(*args)


def _loss_head(y, target):
    t, d = y.shape

    def body(y_ref, t_ref, dy_ref, loss_ref):
        diff = y_ref[...] - t_ref[...]
        dy_ref[...] = diff * (1.0 / d)

        @pl.when(pl.program_id(0) == 0)
        def _():
            loss_ref[...] = jnp.zeros_like(loss_ref)

        loss_ref[...] += jnp.sum(diff * diff) * (0.5 / d)

    return pl.pallas_call(
        body, name="loss_head", grid=(t // ROWS,),
        in_specs=[_row_spec(d), _row_spec(d)],
        out_specs=(_row_spec(d), pl.BlockSpec((8, 128), lambda i: (0, 0))),
        out_shape=(jax.ShapeDtypeStruct((t, d), F32), jax.ShapeDtypeStruct((8, 128), F32)),
        compiler_params=_params("arbitrary"),
    )(y, target)


def _halo_specs(t, col, width):
    rb = ROWS // HALO
    last = t // HALO - 1
    return [pl.BlockSpec((ROWS, width), lambda i: (i, col)),
            pl.BlockSpec((HALO, width), lambda i: (jnp.maximum(i * rb - 1, 0), col)),
            pl.BlockSpec((HALO, width), lambda i: (jnp.minimum((i + 1) * rb, last), col))]


def _glu(a_ref, g_ref):
    return a_ref[...].astype(F32) * jax.nn.sigmoid(g_ref[...].astype(F32))


def _fill_ext(ext, cur, prev, nxt, i, nblk):
    ext[pl.ds(HALO, ROWS), :] = cur
    ext[pl.ds(0, HALO), :] = jnp.where(i > 0, prev, 0.0)
    ext[pl.ds(HALO + ROWS, HALO), :] = jnp.where(i < nblk - 1, nxt, 0.0)


def _conv_fwd(proj, w_dw, b_dw, ln_g, ln_b):
    t = proj.shape[0]
    nblk = t // ROWS
    pad = HALO - CONV_K // 2

    def body(ac, ap, an, gc, gp, gn, w_ref, b_ref, lg_ref, lb_ref, yc_ref, uc_ref, uext):
        i = pl.program_id(0)
        _fill_ext(uext, _glu(ac, gc), _glu(ap, gp), _glu(an, gn), i, nblk)
        acc = jnp.broadcast_to(b_ref[...], (ROWS, CONV_W))
        for j in range(CONV_K):
            acc = acc + uext[pl.ds(j + pad, ROWS), :] * w_ref[pl.ds(j, 1), :]
        uc_ref[...] = acc
        mu = jnp.mean(acc, axis=-1, keepdims=True)
        xc = acc - mu
        var = jnp.mean(xc * xc, axis=-1, keepdims=True)
        yln = xc * lax.rsqrt(var + LN_EPS) * lg_ref[...] + lb_ref[...]
        yc_ref[...] = (yln * jax.nn.sigmoid(yln)).astype(BF16)

    vec = pl.BlockSpec((1, CONV_W), lambda i: (0, 0))
    return pl.pallas_call(
        body, name="conv_fwd", grid=(nblk,),
        in_specs=_halo_specs(t, 0, CONV_W) + _halo_specs(t, 1, CONV_W)
        + [pl.BlockSpec((32, CONV_W), lambda i: (0, 0)), vec, vec, vec],
        out_specs=(pl.BlockSpec((ROWS, CONV_W), lambda i: (i, 0)),
                   pl.BlockSpec((ROWS, CONV_W), lambda i: (i, 0))),
        out_shape=(jax.ShapeDtypeStruct((t, CONV_W), BF16), jax.ShapeDtypeStruct((t, CONV_W), F32)),
        scratch_shapes=[pltpu.VMEM((ROWS + 2 * HALO, CONV_W), F32)],
        compiler_params=_params("parallel"),
    )(proj, proj, proj, proj, proj, proj, w_dw, b_dw, ln_g, ln_b)


def _conv_bwd(proj, uc, dcat, w_dw, ln_g, ln_b):
    t = proj.shape[0]
    nblk = t // ROWS
    pad = HALO - CONV_K // 2

    def body(ac, ap, an, gc, gp, gn, uc_c, uc_p, uc_n, dy_c, dy_p, dy_n, w_ref, lg_ref, lb_ref,
             dag_ref, dw_ref, dvec_ref, uext, dext):
        i = pl.program_id(0)
        _fill_ext(uext, _glu(ac, gc), _glu(ap, gp), _glu(an, gn), i, nblk)

        def ln_bwd(u_ref, d_ref):
            u = u_ref[...]
            mu = jnp.mean(u, axis=-1, keepdims=True)
            xc = u - mu
            rstd = lax.rsqrt(jnp.mean(xc * xc, axis=-1, keepdims=True) + LN_EPS)
            xhat = xc * rstd
            yln = xhat * lg_ref[...] + lb_ref[...]
            sg = jax.nn.sigmoid(yln)
            dyln = d_ref[...] * (sg * (1.0 + yln * (1.0 - sg)))
            dxh = dyln * lg_ref[...]
            du = rstd * (dxh - jnp.mean(dxh, axis=-1, keepdims=True)
                         - xhat * jnp.mean(dxh * xhat, axis=-1, keepdims=True))
            return du, dyln, xhat

        duc, dyln, xhat = ln_bwd(uc_c, dy_c)
        _fill_ext(dext, duc, ln_bwd(uc_p, dy_p)[0], ln_bwd(uc_n, dy_n)[0], i, nblk)

        @pl.when(i == 0)
        def _():
            dw_ref[...] = jnp.zeros_like(dw_ref)
            dvec_ref[...] = jnp.zeros_like(dvec_ref)

        dvec_ref[pl.ds(0, 1), :] += jnp.sum(duc, axis=0, keepdims=True)
        dvec_ref[pl.ds(1, 1), :] += jnp.sum(dyln * xhat, axis=0, keepdims=True)
        dvec_ref[pl.ds(2, 1), :] += jnp.sum(dyln, axis=0, keepdims=True)

        du = jnp.zeros((ROWS, CONV_W), F32)
        for j in range(CONV_K):
            du = du + dext[pl.ds(2 * HALO - pad - j, ROWS), :] * w_ref[pl.ds(j, 1), :]
            dw_ref[pl.ds(j, 1), :] += jnp.sum(duc * uext[pl.ds(j + pad, ROWS), :], axis=0, keepdims=True)

        a = ac[...].astype(F32)
        sg = jax.nn.sigmoid(gc[...].astype(F32))
        dag_ref[:, pl.ds(0, CONV_W)] = (du * sg).astype(BF16)
        dag_ref[:, pl.ds(CONV_W, CONV_W)] = (du * a * sg * (1.0 - sg)).astype(BF16)

    vec = pl.BlockSpec((1, CONV_W), lambda i: (0, 0))
    return pl.pallas_call(
        body, name="conv_bwd", grid=(nblk,),
        in_specs=_halo_specs(t, 0, CONV_W) + _halo_specs(t, 1, CONV_W) + _halo_specs(t, 0, CONV_W)
        + _halo_specs(t, 0, CONV_W) + [pl.BlockSpec((32, CONV_W), lambda i: (0, 0)), vec, vec],
        out_specs=(pl.BlockSpec((ROWS, 2 * CONV_W), lambda i: (i, 0)),
                   pl.BlockSpec((32, CONV_W), lambda i: (0, 0)),
                   pl.BlockSpec((8, CONV_W), lambda i: (0, 0))),
        out_shape=(jax.ShapeDtypeStruct((t, 2 * CONV_W), BF16),
                   jax.ShapeDtypeStruct((32, CONV_W), F32),
                   jax.ShapeDtypeStruct((8, CONV_W), F32)),
        scratch_shapes=[pltpu.VMEM((ROWS + 2 * HALO, CONV_W), F32),
                        pltpu.VMEM((ROWS + 2 * HALO, CONV_W), F32)],
        compiler_params=_params("arbitrary"),
    )(proj, proj, proj, proj, proj, proj, uc, uc, uc, dcat, dcat, dcat, w_dw, ln_g, ln_b)


Q_BLK, K_BLK, V_BLK = 16, 24, 32


def _head_masks():
    lane = lax.broadcasted_iota(jnp.int32, (1, 2 * HEAD_D), 1)
    return [lane < HEAD_D, lane >= HEAD_D]


def _band(r, rows):
    rs = jnp.clip(r - WIN_ROWS // 2, 0, rows - WIN_ROWS)
    return rs, r - rs, pl.multiple_of(r * GRID_W, GRID_W), pl.multiple_of(rs * GRID_W, GRID_W)


def _softmax_band(qh, kb, bias):
    s = lax.dot_general(qh, kb, (((1,), (1,)), ((), ())), preferred_element_type=F32)
    s = s * (HEAD_D ** -0.5) + bias
    e = jnp.exp(s - jnp.max(s, axis=-1, keepdims=True))
    return e / jnp.sum(e, axis=-1, keepdims=True)


def _natten_fwd(proj, bias):
    t = proj.shape[0]
    rows = t // GRID_W

    def body(q_ref, k_ref, v_ref, b_ref, o_ref):
        masks = _head_masks()

        def step(r, carry):
            _, off, q0, k0 = _band(r, rows)
            qp = q_ref[pl.ds(q0, GRID_W), :]
            kb = k_ref[pl.ds(k0, BAND), :]
            vb = v_ref[pl.ds(k0, BAND), :]
            outs = []
            for hh in range(2):
                qh = jnp.where(masks[hh], qp, jnp.zeros_like(qp))
                p = _softmax_band(qh, kb, b_ref[hh, off])
                outs.append(jnp.dot(p.astype(BF16), vb, preferred_element_type=F32))
            o_ref[pl.ds(q0, GRID_W), :] = jnp.where(masks[0], outs[0], outs[1]).astype(BF16)
            return carry

        lax.fori_loop(0, rows, step, 0)

    col = lambda b: pl.BlockSpec((t, 2 * HEAD_D), lambda h: (0, b + h))
    return pl.pallas_call(
        body, name="natten_fwd", grid=(N_HEADS // 2,),
        in_specs=[col(Q_BLK), col(K_BLK), col(V_BLK),
                  pl.BlockSpec((2, WIN_ROWS, GRID_W, BAND), lambda h: (h, 0, 0, 0))],
        out_specs=pl.BlockSpec((t, 2 * HEAD_D), lambda h: (0, h)),
        out_shape=jax.ShapeDtypeStruct((t, NA_W), BF16),
        compiler_params=_params("parallel"),
    )(proj, proj, proj, bias)


def _natten_bwd(proj, bias, dcat):
    t = proj.shape[0]
    rows = t // GRID_W

    def body(q_ref, k_ref, v_ref, b_ref, do_ref, dq_ref, dk_ref, dv_ref, db_ref, dk_acc, dv_acc):
        masks = _head_masks()
        dk_acc[...] = jnp.zeros_like(dk_acc)
        dv_acc[...] = jnp.zeros_like(dv_acc)
        db_ref[...] = jnp.zeros_like(db_ref)
        tn = (((0,), (0,)), ((), ()))

        def step(r, carry):
            _, off, q0, k0 = _band(r, rows)
            qp = q_ref[pl.ds(q0, GRID_W), :]
            kb = k_ref[pl.ds(k0, BAND), :]
            vb = v_ref[pl.ds(k0, BAND), :]
            dop = do_ref[pl.ds(q0, GRID_W), :]
            dob = dop.astype(BF16)
            dq, dk, dv = [], [], []
            for hh in range(2):
                qh = jnp.where(masks[hh], qp, jnp.zeros_like(qp))
                doh = jnp.where(masks[hh], dob, jnp.zeros_like(dob))
                p = _softmax_band(qh, kb, b_ref[hh, off])
                dp = lax.dot_general(doh, vb, (((1,), (1,)), ((), ())), preferred_element_type=F32)
                ds = p * (dp - jnp.sum(p * dp, axis=-1, keepdims=True))
                db_ref[hh, off] += ds
                dsb = ds.astype(BF16)
                dq.append(jnp.dot(dsb, kb, preferred_element_type=F32))
                dk.append(lax.dot_general(dsb, qp, tn, preferred_element_type=F32))
                dv.append(lax.dot_general(p.astype(BF16), dob, tn, preferred_element_type=F32))
            scale = HEAD_D ** -0.5
            dq_ref[pl.ds(q0, GRID_W), :] = (jnp.where(masks[0], dq[0], dq[1]) * scale).astype(BF16)
            dk_acc[pl.ds(k0, BAND), :] += jnp.where(masks[0], dk[0], dk[1]) * scale
            dv_acc[pl.ds(k0, BAND), :] += jnp.where(masks[0], dv[0], dv[1])
            return carry

        lax.fori_loop(0, rows, step, 0)
        dk_ref[...] = dk_acc[...].astype(BF16)
        dv_ref[...] = dv_acc[...].astype(BF16)

    col = lambda b: pl.BlockSpec((t, 2 * HEAD_D), lambda h: (0, b + h))
    b_spec = pl.BlockSpec((2, WIN_ROWS, GRID_W, BAND), lambda h: (h, 0, 0, 0))
    o_spec = pl.BlockSpec((t, 2 * HEAD_D), lambda h: (0, h))
    o_shape = jax.ShapeDtypeStruct((t, NA_W), BF16)
    return pl.pallas_call(
        body, name="natten_bwd", grid=(N_HEADS // 2,),
        in_specs=[col(Q_BLK), col(K_BLK), col(V_BLK), b_spec, col(CONV_W // (2 * HEAD_D))],
        out_specs=(o_spec, o_spec, o_spec, b_spec),
        out_shape=(o_shape, o_shape, o_shape,
                   jax.ShapeDtypeStruct((N_HEADS, WIN_ROWS, GRID_W, BAND), F32)),
        scratch_shapes=[pltpu.VMEM((t, 2 * HEAD_D), F32), pltpu.VMEM((t, 2 * HEAD_D), F32)],
        compiler_params=_params("parallel"),
    )(proj, proj, proj, bias, dcat)


def _bias_index_tables():
    off = np.arange(WIN_ROWS)
    kr = np.arange(WIN_ROWS)
    c = np.arange(GRID_W)
    dr = kr[None, :] - off[:, None] + (WIN_ROWS - 1)
    dc = np.clip(c[None, :] - c[:, None], -(WIN_COLS - 1), WIN_COLS - 1) + (WIN_COLS - 1)
    cs = np.clip(c - WIN_COLS // 2, 0, GRID_W - WIN_COLS)
    mask = (c[None, :] >= cs[:, None]) & (c[None, :] < cs[:, None] + WIN_COLS)
    return dr, dc, mask


def _bias_table(rpb_l):
    dr, dc, mask = _bias_index_tables()
    b = rpb_l[:, dr[:, None, :, None], dc[None, :, None, :]]
    b = jnp.where(mask[None, None, :, None, :], b, NEG_INF)
    return b.reshape(N_HEADS, WIN_ROWS, GRID_W, BAND)


def _rpb_grad(dbias):
    dr, dc, _ = _bias_index_tables()
    e = np.zeros((GRID_W * GRID_W, 128), np.float32)
    e[np.arange(GRID_W * GRID_W), dc.reshape(-1)] = 1.0
    f = np.zeros((16, WIN_ROWS * WIN_ROWS), np.float32)
    f[dr.reshape(-1), np.arange(WIN_ROWS * WIN_ROWS)] = 1.0
    z = dbias.reshape(N_HEADS, WIN_ROWS, GRID_W, WIN_ROWS, GRID_W).transpose(0, 1, 3, 2, 4)
    z = z.reshape(N_HEADS, WIN_ROWS * WIN_ROWS, GRID_W * GRID_W)

    def body(z_ref, e_ref, f_ref, o_ref):
        hi = lax.Precision.HIGHEST
        zd = jnp.dot(z_ref[...], e_ref[...], preferred_element_type=F32, precision=hi)
        o_ref[...] = jnp.dot(f_ref[...], zd, preferred_element_type=F32, precision=hi)

    out = pl.pallas_call(
        body, name="rpb_grad", grid=(N_HEADS,),
        in_specs=[pl.BlockSpec((None, WIN_ROWS * WIN_ROWS, GRID_W * GRID_W), lambda h: (h, 0, 0)),
                  pl.BlockSpec((GRID_W * GRID_W, 128), lambda h: (0, 0)),
                  pl.BlockSpec((16, WIN_ROWS * WIN_ROWS), lambda h: (0, 0))],
        out_specs=pl.BlockSpec((None, 16, 128), lambda h: (h, 0, 0)),
        out_shape=jax.ShapeDtypeStruct((N_HEADS, 16, 128), F32),
        compiler_params=_params("parallel"),
    )(z, jnp.asarray(e), jnp.asarray(f))
    return out[:, :2 * WIN_ROWS - 1, :2 * WIN_COLS - 1]


def _cast_bf16(w2d, *, name):
    r, c = w2d.shape
    tr = _tile(r, (512, 256))

    def body(w_ref, o_ref):
        o_ref[...] = w_ref[...].astype(BF16)

    return pl.pallas_call(
        body, name=name, grid=(r // tr,),
        in_specs=[pl.BlockSpec((tr, c), lambda i: (i, 0))],
        out_specs=pl.BlockSpec((tr, c), lambda i: (i, 0)),
        out_shape=jax.ShapeDtypeStruct((r, c), BF16),
        compiler_params=_params("parallel"),
    )(w2d)


def _pair_sum(dw, other, *, name):
    s, r, c = dw.shape
    h = r // 2
    tr = _tile(h, (256, 128))
    nb = h // tr

    def body(a_ref, b_ref, o_ref):
        o_ref[...] = (a_ref[...].astype(F32) + b_ref[...].astype(F32)).astype(BF16)

    return pl.pallas_call(
        body, name=name, grid=(s, nb),
        in_specs=[pl.BlockSpec((None, tr, c), lambda si, i: (si, lax.axis_index("c") * nb + i, 0)),
                  pl.BlockSpec((None, tr, c), lambda si, i: (si, i, 0))],
        out_specs=pl.BlockSpec((None, tr, c), lambda si, i: (si, i, 0)),
        out_shape=jax.ShapeDtypeStruct((s, h, c), BF16),
        compiler_params=_params("parallel", "parallel"),
    )(dw, other)


def _chip_sum(parts, grad, layer, *, name):
    s, h, c = parts.shape
    tr = _tile(h, (256, 128))
    nb = h // tr

    def body(p_ref, g_in, o_ref):
        del g_in
        acc = p_ref[0].astype(F32) + p_ref[1].astype(F32)
        acc = acc + p_ref[2].astype(F32)
        o_ref[...] = acc + p_ref[3].astype(F32)

    return pl.pallas_call(
        body, name=name, grid=(nb,),
        in_specs=[pl.BlockSpec((s, tr, c), lambda i: (0, i, 0)), ANY],
        out_specs=pl.BlockSpec((None, tr, c), lambda i: (layer, lax.axis_index("c") * nb + i, 0)),
        out_shape=jax.ShapeDtypeStruct(grad.shape, F32),
        input_output_aliases={1: 0},
        compiler_params=_params("parallel"),
    )(parts, grad)


def _adamw(w, g, m, v, *, name):
    r, c = w.shape
    tr = _tile(r, (256, 128, 8))
    tr = tr if r % tr == 0 else r
    bc1 = 1.0 - ADAM_B1 ** ADAM_STEP
    bc2 = 1.0 - ADAM_B2 ** ADAM_STEP

    def body(w_ref, g_ref, m_ref, v_ref, go_ref, d_ref, mo_ref, vo_ref):
        gv = g_ref[...]
        mn = ADAM_B1 * m_ref[...] + (1.0 - ADAM_B1) * gv
        vn = ADAM_B2 * v_ref[...] + (1.0 - ADAM_B2) * (gv * gv)
        go_ref[...] = gv
        mo_ref[...] = mn
        vo_ref[...] = vn
        d_ref[...] = -ADAM_LR * ((mn / bc1) / (jnp.sqrt(vn / bc2) + ADAM_EPS) + ADAM_WD * w_ref[...])

    spec = pl.BlockSpec((tr, c), lambda i: (i, 0))
    shape = jax.ShapeDtypeStruct((r, c), F32)
    return pl.pallas_call(
        body, name=name, grid=(r // tr,),
        in_specs=[spec] * 4, out_specs=(spec,) * 4, out_shape=(shape,) * 4,
        compiler_params=_params("parallel"),
    )(w, g, m, v)


def _me():
    return lax.axis_index("x"), lax.axis_index("y"), lax.axis_index("c")


def _other_chips(x, y):
    return [(1 - x, y), (x, 1 - y), (1 - x, 1 - y)]


def _remote(src, dst, send, recv, k, to):
    return pltpu.make_async_remote_copy(src_ref=src, dst_ref=dst, send_sem=send.at[k], recv_sem=recv.at[k],
                                        device_id=to, device_id_type=MESH)


def _gather_weights(layer, shards):
    nw = len(shards)

    def body(*refs):
        ins, outs = refs[:nw], refs[nw:2 * nw]
        send, recv, local = refs[2 * nw:]
        x, y, c = _me()
        me_chip = 2 * x + y
        sib = (x, y, 1 - c)
        chips = _other_chips(x, y)

        def half(ref, chip, hc):
            hr = ref.shape[1] // 2
            return ref.at[chip, pl.ds(hc * hr, hr), :]

        mine = [pltpu.make_async_copy(ins[w].at[layer], outs[w].at[me_chip], local.at[w]) for w in range(nw)]
        for cp in mine:
            cp.start()
        first = []
        for w in range(nw):
            hr = ins[w].shape[1] // 2
            src = ins[w].at[layer, pl.ds(c * hr, hr), :]
            for j, chip in enumerate(chips):
                first.append(_remote(src, half(outs[w], me_chip, c), send, recv, 6 * w + j, (*chip, c)))
        for cp in first:
            cp.start()
        passed = []
        for w in range(nw):
            for j, chip in enumerate(chips):
                landed = half(outs[w], 2 * chip[0] + chip[1], c)
                _remote(landed, landed, send, recv, 6 * w + j, sib).wait_recv()
                cp = _remote(landed, landed, send, recv, 6 * w + 3 + j, sib)
                cp.start()
                passed.append(cp)
        for w in range(nw):
            for j, chip in enumerate(chips):
                theirs = half(outs[w], 2 * chip[0] + chip[1], 1 - c)
                _remote(theirs, theirs, send, recv, 6 * w + 3 + j, sib).wait_recv()
        for cp in first + passed:
            cp.wait_send()
        for cp in mine:
            cp.wait()

    return pl.pallas_call(
        body, name=f"gather_weights_{layer}",
        in_specs=[ANY] * nw, out_specs=[ANY] * nw,
        out_shape=[jax.ShapeDtypeStruct((N_CHIPS,) + s.shape[1:], BF16) for s in shards],
        scratch_shapes=[pltpu.SemaphoreType.DMA((6 * nw,)), pltpu.SemaphoreType.DMA((6 * nw,)),
                        pltpu.SemaphoreType.DMA((nw,))],
    )(*shards)


def _pair_exchange(dws):
    nw = len(dws)

    def body(*refs):
        ins, outs = refs[:nw], refs[nw:2 * nw]
        send, recv = refs[2 * nw:]
        x, y, c = _me()
        cps = []
        for w in range(nw):
            hr = ins[w].shape[1] // 2
            src = ins[w].at[:, pl.ds((1 - c) * hr, hr), :]
            cps.append(_remote(src, outs[w], send, recv, w, (x, y, 1 - c)))
        for cp in cps:
            cp.start()
        for cp in cps:
            cp.wait()

    return pl.pallas_call(
        body, name="grad_pair_exchange",
        in_specs=[ANY] * nw, out_specs=[ANY] * nw,
        out_shape=[jax.ShapeDtypeStruct((d.shape[0], d.shape[1] // 2, d.shape[2]), BF16) for d in dws],
        scratch_shapes=[pltpu.SemaphoreType.DMA((nw,)), pltpu.SemaphoreType.DMA((nw,))],
    )(*dws)


def _chip_scatter(qs):
    nw = len(qs)

    def body(*refs):
        ins, outs = refs[:nw], refs[nw:2 * nw]
        send, recv, local = refs[2 * nw:]
        x, y, c = _me()
        me_chip = 2 * x + y
        chips = _other_chips(x, y)
        mine = [pltpu.make_async_copy(ins[w].at[me_chip], outs[w].at[me_chip], local.at[w]) for w in range(nw)]
        for cp in mine:
            cp.start()
        cps = []
        for w in range(nw):
            for j, chip in enumerate(chips):
                cps.append(_remote(ins[w].at[2 * chip[0] + chip[1]], outs[w].at[me_chip], send, recv,
                                   3 * w + j, (*chip, c)))
        for cp in cps:
            cp.start()
        for w in range(nw):
            for j, chip in enumerate(chips):
                theirs = outs[w].at[2 * chip[0] + chip[1]]
                _remote(theirs, theirs, send, recv, 3 * w + j, (*chip, c)).wait_recv()
        for cp in cps:
            cp.wait_send()
        for cp in mine:
            cp.wait()

    return pl.pallas_call(
        body, name="grad_chip_scatter",
        in_specs=[ANY] * nw, out_specs=[ANY] * nw,
        out_shape=[jax.ShapeDtypeStruct(q.shape, BF16) for q in qs],
        scratch_shapes=[pltpu.SemaphoreType.DMA((3 * nw,)), pltpu.SemaphoreType.DMA((3 * nw,)),
                        pltpu.SemaphoreType.DMA((nw,))],
    )(*qs)


def _sibling_fill(layer, grads):
    nw = len(grads)

    def body(*refs):
        ins, outs = refs[:nw], refs[nw:2 * nw]
        send, recv = refs[2 * nw:]
        del ins
        x, y, c = _me()
        cps = []
        for w in range(nw):
            hr = outs[w].shape[1] // 2
            mine = outs[w].at[layer, pl.ds(c * hr, hr), :]
            cps.append(_remote(mine, mine, send, recv, w, (x, y, 1 - c)))
        for cp in cps:
            cp.start()
        for w in range(nw):
            hr = outs[w].shape[1] // 2
            theirs = outs[w].at[layer, pl.ds((1 - c) * hr, hr), :]
            _remote(theirs, theirs, send, recv, w, (x, y, 1 - c)).wait_recv()
        for cp in cps:
            cp.wait_send()

    return pl.pallas_call(
        body, name=f"grad_sibling_fill_{layer}",
        in_specs=[ANY] * nw, out_specs=[ANY] * nw,
        out_shape=[jax.ShapeDtypeStruct(g.shape, F32) for g in grads],
        input_output_aliases={w: w for w in range(nw)},
        scratch_shapes=[pltpu.SemaphoreType.DMA((nw,)), pltpu.SemaphoreType.DMA((nw,))],
    )(*grads)


def _all_devices(pack, *, reduce, name):
    r = pack.shape[0]

    def body(p_ref, o_ref, buf, send, recv):
        x, y, c = _me()
        me = 4 * x + 2 * y + c
        buf[me] = p_ref[...]
        flips = [(fx, fy, fc) for fx in (0, 1) for fy in (0, 1) for fc in (0, 1)][1:]
        peers = [(x ^ fx, y ^ fy, c ^ fc) for fx, fy, fc in flips]
        cps = [_remote(p_ref, buf.at[me], send, recv, k, peer) for k, peer in enumerate(peers)]
        for cp in cps:
            cp.start()
        for k, (px, py, pc) in enumerate(peers):
            theirs = buf.at[4 * px + 2 * py + pc]
            _remote(theirs, theirs, send, recv, k, (px, py, pc)).wait_recv()
        for cp in cps:
            cp.wait_send()
        if reduce:
            acc = buf[0]
            for d in range(1, N_DEV):
                acc = acc + buf[d]
            o_ref[...] = acc
        else:
            o_ref[...] = buf[...]

    vm = pl.BlockSpec(memory_space=pltpu.VMEM)
    return pl.pallas_call(
        body, name=name,
        in_specs=[vm], out_specs=vm,
        out_shape=jax.ShapeDtypeStruct((r, 128) if reduce else (N_DEV, r, 128), F32),
        scratch_shapes=[pltpu.VMEM((N_DEV, r, 128), F32), pltpu.SemaphoreType.DMA((N_DEV - 1,)),
                        pltpu.SemaphoreType.DMA((N_DEV - 1,))],
        compiler_params=pltpu.CompilerParams(vmem_limit_bytes=VMEM_LIMIT),
    )(pack)


def _to_rows(flat):
    v = flat.reshape(-1)
    rows = -(-v.shape[0] // 1024) * 8
    return jnp.pad(v, (0, rows * 128 - v.shape[0])).reshape(rows, 128)


def kernel(x, w_in, w_dw, b_dw, conv_ln_g, conv_ln_b, rpb, w_out, w_up, w_down, pre_mix_g, post_mix_g, pre_mlp_g, post_mlp_g, loss_target, m_w_in, m_w_dw, m_b_dw, m_conv_ln_g, m_conv_ln_b, m_rpb, m_w_out, m_w_up, m_w_down, m_pre_mix_g, m_post_mix_g, m_pre_mlp_g, m_post_mlp_g, v_w_in, v_w_dw, v_b_dw, v_conv_ln_g, v_conv_ln_b, v_rpb, v_w_out, v_w_up, v_w_down, v_pre_mix_g, v_post_mix_g, v_pre_mlp_g, v_post_mlp_g):
    depth = w_in.shape[0]
    t = x.shape[1]
    xl = x.reshape(t, D_MODEL)
    target = loss_target.reshape(t, D_MODEL)
    chip = 2 * lax.axis_index("x") + lax.axis_index("y")

    big = {"w_in": w_in, "w_out": w_out, "w_up": w_up, "w_down": w_down}
    big_names = list(big)
    shards = [_cast_bf16(big[n].reshape(-1, big[n].shape[-1]), name=f"cast_{n}").reshape(big[n].shape)
              for n in big_names]

    wdw_all = _all_devices(_to_rows(w_dw), reduce=False, name="gather_w_dw")[::2]
    wdw_all = wdw_all.reshape(N_CHIPS, -1)[:, :w_dw.size].reshape((N_CHIPS,) + w_dw.shape)
    wdw_full = jnp.moveaxis(wdw_all, 0, 2).reshape(depth, CONV_K, CONV_W)
    wdw_pad = jnp.pad(wdw_full, ((0, 0), (0, 32 - CONV_K), (0, 0)))

    vec = lambda a, l: a[l].reshape(1, -1)

    saved = []
    h = _rms_fwd(xl, vec(pre_mix_g, 0), name="rms_first")
    for l in range(depth):
        g_in, g_out, g_up, g_down = _gather_weights(l, shards)
        g_out = g_out.reshape(1, D_MODEL, D_MODEL)
        g_down = g_down.reshape(1, D_FF, D_MODEL)
        bias = _bias_table(rpb[l])
        proj = _mm_nn(h, g_in, out_dtype=BF16, name="proj")
        yc, uc = _conv_fwd(proj, wdw_pad[l], vec(b_dw, l), vec(conv_ln_g, l), vec(conv_ln_b, l))
        ya = _natten_fwd(proj, bias)
        cat = jnp.concatenate([yc, ya], axis=1)
        mix = _mm_nn(cat, g_out, out_dtype=F32, name="out_proj")
        x1, h2 = _resid_rms(xl, mix, vec(post_mix_g, l), vec(pre_mlp_g, l), name="mix_resid")
        up, act = _mm_nn(h2, g_up, out_dtype=BF16, relu2=True, name="mlp_up")
        f = _mm_nn(act, g_down, out_dtype=F32, name="mlp_down")
        saved.append(dict(x=xl, h=h, proj=proj, uc=uc, cat=cat, mix=mix, x1=x1, h2=h2, up=up, act=act, f=f,
                          bias=bias, w=(g_in, g_out, g_up, g_down)))
        if l + 1 < depth:
            xl, h = _resid_rms(x1, f, vec(post_mlp_g, l), vec(pre_mix_g, l + 1), name="mlp_resid")
        else:
            (xl,) = _resid_rms(x1, f, vec(post_mlp_g, l), None, name="mlp_resid_last")

    dx, loss_blk = _loss_head(xl, target)
    loss = lax.psum(loss_blk[0, 0], ("x", "y", "c"))

    grads = [lax.empty(big[n].shape, F32) for n in big_names]
    small = [None] * depth
    for l in reversed(range(depth)):
        sv = saved[l]
        g_in, g_out, g_up, g_down = sv["w"]
        df, dg_post_mlp = _rms_bwd(dx, sv["f"], vec(post_mlp_g, l), None, out_dtype=BF16, name="rms_bwd_mlp_post")
        dup = _mm_nt(df, g_down, out_dtype=BF16, up=sv["up"], name="d_act")
        dw_down = _mm_tn(sv["act"], df, 1, name="dw_down")
        dh2 = _mm_nt(dup, g_up, out_dtype=F32, name="d_h2")
        dw_up = _mm_tn(sv["h2"], dup, N_CHIPS, name="dw_up")
        dx1, dg_pre_mlp = _rms_bwd(dh2, sv["x1"], vec(pre_mlp_g, l), dx, out_dtype=F32, name="rms_bwd_mlp_pre")
        dmix, dg_post_mix = _rms_bwd(dx1, sv["mix"], vec(post_mix_g, l), None, out_dtype=BF16, name="rms_bwd_mix_post")
        dcat = _mm_nt(dmix, g_out, out_dtype=F32, name="d_cat")
        dw_out = _mm_tn(sv["cat"], dmix, 1, name="dw_out")
        dq, dk, dv, dbias = _natten_bwd(sv["proj"], sv["bias"], dcat)
        dag, dwdw, dvec = _conv_bwd(sv["proj"], sv["uc"], dcat, wdw_pad[l], vec(conv_ln_g, l), vec(conv_ln_b, l))
        dproj = jnp.concatenate([dag, dq, dk, dv], axis=1)
        dh = _mm_nt(dproj, g_in, out_dtype=F32, name="d_h")
        dw_in = _mm_tn(sv["h"], dproj, N_CHIPS, name="dw_in")
        dx, dg_pre_mix = _rms_bwd(dh, sv["x"], vec(pre_mix_g, l), dx1, out_dtype=F32, name="rms_bwd_mix_pre")

        dws = [dw_in, dw_out.reshape(N_CHIPS, -1, D_MODEL), dw_up, dw_down.reshape(N_CHIPS, -1, D_MODEL)]
        others = _pair_exchange(dws)
        qs = [_pair_sum(d, o, name=f"pair_sum_{n}") for d, o, n in zip(dws, others, big_names)]
        parts = _chip_scatter(qs)
        grads = [_chip_sum(p, g, l, name=f"chip_sum_{n}_{l}") for p, g, n in zip(parts, grads, big_names)]
        grads = _sibling_fill(l, grads)

        drpb = _rpb_grad(dbias)
        small[l] = jnp.concatenate([
            dvec[0], dvec[1], dvec[2], dg_pre_mix[0], dg_post_mix[0], dg_pre_mlp[0], dg_post_mlp[0],
            drpb.reshape(-1), dwdw[:CONV_K].reshape(-1)])

    small_sum = _all_devices(_to_rows(jnp.stack(small)), reduce=True, name="reduce_small_grads")
    small_sum = small_sum.reshape(-1)[:depth * small[0].shape[0]].reshape(depth, -1)
    sizes = [CONV_W, CONV_W, CONV_W, D_MODEL, D_MODEL, D_MODEL, D_MODEL, rpb[0].size, CONV_K * CONV_W]
    offs = np.concatenate([[0], np.cumsum(sizes)])
    pieces = [small_sum[:, offs[i]:offs[i + 1]] for i in range(len(sizes))]
    g_small = {
        "b_dw": pieces[0], "conv_ln_g": pieces[1], "conv_ln_b": pieces[2],
        "pre_mix_g": pieces[3], "post_mix_g": pieces[4], "pre_mlp_g": pieces[5], "post_mlp_g": pieces[6],
        "rpb": pieces[7].reshape(rpb.shape),
        "w_dw": lax.dynamic_slice_in_dim(pieces[8].reshape(depth, CONV_K, CONV_W), chip * w_dw.shape[2],
                                         w_dw.shape[2], axis=2),
    }

    given = dict(w_in=(w_in, m_w_in, v_w_in), w_dw=(w_dw, m_w_dw, v_w_dw), b_dw=(b_dw, m_b_dw, v_b_dw),
                 conv_ln_g=(conv_ln_g, m_conv_ln_g, v_conv_ln_g), conv_ln_b=(conv_ln_b, m_conv_ln_b, v_conv_ln_b),
                 rpb=(rpb, m_rpb, v_rpb), w_out=(w_out, m_w_out, v_w_out), w_up=(w_up, m_w_up, v_w_up),
                 w_down=(w_down, m_w_down, v_w_down), pre_mix_g=(pre_mix_g, m_pre_mix_g, v_pre_mix_g),
                 post_mix_g=(post_mix_g, m_post_mix_g, v_post_mix_g), pre_mlp_g=(pre_mlp_g, m_pre_mlp_g, v_pre_mlp_g),
                 post_mlp_g=(post_mlp_g, m_post_mlp_g, v_post_mlp_g))
    results = {}
    for n, g in zip(big_names, grads):
        w, m, v = given[n]
        flat = lambda a: a.reshape(-1, a.shape[-1])
        outs = _adamw(flat(w), flat(g), flat(m), flat(v), name=f"adamw_{n}")
        results[n] = [o.reshape(w.shape) for o in outs]

    small_names = list(g_small)
    pack = lambda arrs: _to_rows(jnp.concatenate([a.reshape(-1) for a in arrs]))
    outs = _adamw(pack([given[n][0] for n in small_names]), pack([g_small[n] for n in small_names]),
                  pack([given[n][1] for n in small_names]), pack([given[n][2] for n in small_names]),
                  name="adamw_small")
    pos = 0
    for n in small_names:
        w = given[n][0]
        results[n] = [o.reshape(-1)[pos:pos + w.size].reshape(w.shape) for o in outs]
        pos += w.size

    order = ["w_in", "w_dw", "b_dw", "conv_ln_g", "conv_ln_b", "rpb", "w_out", "w_up", "w_down",
             "pre_mix_g", "post_mix_g", "pre_mlp_g", "post_mlp_g"]
    return (loss, dx.reshape(x.shape), *[results[n][0] for n in order], *[results[n][1] for n in order],
            *[results[n][2] for n in order], *[results[n][3] for n in order])
```

```python
import functools

import numpy as np
import jax
import jax.numpy as jnp
from jax import lax
from jax.experimental import pallas as pl
from jax.experimental.pallas import tpu as pltpu

F32 = jnp.float32
BF16 = jnp.bfloat16

D_MODEL = 2048
CONV_W = 1024
NA_W = 1024
N_HEADS = 16
HEAD_D = 64
GRID_W = 64
WIN_ROWS = 8
WIN_COLS = 16
CONV_K = 31
D_FF = 4 * D_MODEL
IN_COLS = 2 * CONV_W + 3 * NA_W
RMS_EPS = 1e-6
LN_EPS = 1e-5
NEG_INF = -1e30
N_CHIPS = 4
N_DEV = 8
HALO = 16
BAND = WIN_ROWS * GRID_W

ADAM_LR = 0.001
ADAM_B1 = 0.9
ADAM_B2 = 0.999
ADAM_EPS = 1e-08
ADAM_WD = 0.01
ADAM_STEP = 10

VMEM_LIMIT = 48 * 1024 * 1024
MESH = pl.DeviceIdType.MESH
ANY = pl.BlockSpec(memory_space=pl.ANY)


def _params(*sem):
    return pltpu.CompilerParams(dimension_semantics=sem, vmem_limit_bytes=VMEM_LIMIT)


class _Comm:
    def __init__(self, ins, out_shape, sems, start, finish):
        self.ins, self.out_shape, self.sems, self.start, self.finish = ins, out_shape, sems, start, finish


def _call(body, *, name, grid, in_specs, out_specs, out_shape, scratch_shapes=(), sem, args, comm=None):
    in_specs, out_specs, out_shape = list(in_specs), list(out_specs), list(out_shape)
    scratch_shapes = list(scratch_shapes)
    if comm is None:
        return pl.pallas_call(body, name=name, grid=grid, in_specs=in_specs, out_specs=out_specs,
                              out_shape=out_shape, scratch_shapes=scratch_shapes,
                              compiler_params=_params(*sem))(*args)
    ni, no, ns = len(in_specs), len(out_specs), len(scratch_shapes)
    nci, nco = len(comm.ins), len(comm.out_shape)

    def full(*refs):
        ins, refs = refs[:ni], refs[ni:]
        cins, refs = refs[:nci], refs[nci:]
        outs, refs = refs[:no], refs[no:]
        couts, refs = refs[:nco], refs[nco:]
        scr, csems = refs[:ns], refs[ns:]
        ids = [pl.program_id(d) for d in range(len(grid))]
        first = functools.reduce(jnp.logical_and, [i == 0 for i in ids])
        last = functools.reduce(jnp.logical_and, [i == g - 1 for i, g in zip(ids, grid)])

        @pl.when(first)
        def _():
            comm.start(cins, couts, csems)

        body(*ins, *outs, *scr)

        @pl.when(last)
        def _():
            comm.finish(cins, couts, csems)

    return pl.pallas_call(
        full, name=name, grid=grid,
        in_specs=in_specs + [ANY] * nci, out_specs=out_specs + [ANY] * nco,
        out_shape=out_shape + list(comm.out_shape),
        scratch_shapes=scratch_shapes + list(comm.sems),
        compiler_params=_params(*(["arbitrary"] * len(grid))),
    )(*args, *comm.ins)


def _tile(n, pref):
    for t in pref:
        if n % t == 0:
            return t
    return n


def _mm_nn(a, w, *, out_dtype, relu2=False, name, comm=None):
    m, k = a.shape
    s, _, n = w.shape
    tm = _tile(m, (1024, 512, 256))
    tn = _tile(n, (1024, 1280, 512))
    tk = _tile(k, (512,))
    nps = n // tn
    nk = k // tk

    def body(a_ref, w_ref, *rest):
        outs, acc = rest[:-1], rest[-1]
        kk = pl.program_id(2)

        @pl.when(kk == 0)
        def _():
            acc[...] = jnp.zeros_like(acc)

        acc[...] += jnp.dot(a_ref[...], w_ref[...], preferred_element_type=F32)

        @pl.when(kk == nk - 1)
        def _():
            r = acc[...]
            if relu2:
                outs[0][...] = r.astype(outs[0].dtype)
                p = jnp.maximum(r, 0.0)
                outs[1][...] = (p * p).astype(outs[1].dtype)
            else:
                outs[0][...] = r.astype(outs[0].dtype)

    o_spec = pl.BlockSpec((tm, tn), lambda i, j, kk: (i, j))
    o_shape = jax.ShapeDtypeStruct((m, s * n), out_dtype)
    return _call(
        body, name=name,
        grid=(m // tm, s * nps, nk),
        in_specs=[pl.BlockSpec((tm, tk), lambda i, j, kk: (i, kk)),
                  pl.BlockSpec((None, tk, tn), lambda i, j, kk: (j // nps, kk, j % nps))],
        out_specs=[o_spec, o_spec] if relu2 else [o_spec],
        out_shape=[o_shape, o_shape] if relu2 else [o_shape],
        scratch_shapes=[pltpu.VMEM((tm, tn), F32)],
        sem=("parallel", "parallel", "arbitrary"), args=(a, w), comm=comm)


def _mm_nt(dy, w, *, out_dtype, up=None, name):
    m = dy.shape[0]
    s, k, n = w.shape
    tm = _tile(m, (1024, 512, 256))
    tko = _tile(k, (1024, 512))
    tn = _tile(n, (512, 640))
    nps = n // tn
    nn = s * nps

    def body(dy_ref, w_ref, *rest):
        if up is None:
            o_ref, acc = rest
        else:
            up_ref, o_ref, acc = rest
        kk = pl.program_id(2)

        @pl.when(kk == 0)
        def _():
            acc[...] = jnp.zeros_like(acc)

        acc[...] += lax.dot_general(dy_ref[...], w_ref[...], (((1,), (1,)), ((), ())),
                                    preferred_element_type=F32)

        @pl.when(kk == nn - 1)
        def _():
            r = acc[...]
            if up is not None:
                r = r * (2.0 * jnp.maximum(up_ref[...].astype(F32), 0.0))
            o_ref[...] = r.astype(o_ref.dtype)

    in_specs = [pl.BlockSpec((tm, tn), lambda i, j, kk: (i, kk)),
                pl.BlockSpec((None, tko, tn), lambda i, j, kk: (kk // nps, j, kk % nps))]
    args = [dy, w]
    if up is not None:
        in_specs.append(pl.BlockSpec((tm, tko), lambda i, j, kk: (i, j)))
        args.append(up)
    return pl.pallas_call(
        body, name=name,
        grid=(m // tm, k // tko, nn),
        in_specs=in_specs,
        out_specs=pl.BlockSpec((tm, tko), lambda i, j, kk: (i, j)),
        out_shape=jax.ShapeDtypeStruct((m, k), out_dtype),
        scratch_shapes=[pltpu.VMEM((tm, tko), F32)],
        compiler_params=_params("parallel", "parallel", "arbitrary"),
    )(*args)


def _mm_tn(a, dy, s, *, name):
    m, k = a.shape
    n = dy.shape[1] // s
    tk = _tile(k, (1024, 512))
    tn = _tile(n, (1024, 1280, 512))
    tm = _tile(m, (512,))
    nps = n // tn
    nm = m // tm

    def body(a_ref, dy_ref, o_ref, acc):
        mm = pl.program_id(2)

        @pl.when(mm == 0)
        def _():
            acc[...] = jnp.zeros_like(acc)

        acc[...] += lax.dot_general(a_ref[...], dy_ref[...], (((0,), (0,)), ((), ())),
                                    preferred_element_type=F32)

        @pl.when(mm == nm - 1)
        def _():
            o_ref[...] = acc[...].astype(o_ref.dtype)

    return pl.pallas_call(
        body, name=name,
        grid=(k // tk, s * nps, nm),
        in_specs=[pl.BlockSpec((tm, tk), lambda i, j, mm: (mm, i)),
                  pl.BlockSpec((tm, tn), lambda i, j, mm: (mm, j))],
        out_specs=pl.BlockSpec((None, tk, tn), lambda i, j, mm: (j // nps, i, j % nps)),
        out_shape=jax.ShapeDtypeStruct((s, k, n), BF16),
        scratch_shapes=[pltpu.VMEM((tk, tn), F32)],
        compiler_params=_params("parallel", "parallel", "arbitrary"),
    )(a, dy)


ROWS = 256


def _row_spec(d):
    return pl.BlockSpec((ROWS, d), lambda i: (i, 0))


def _vec_spec(d):
    return pl.BlockSpec((1, d), lambda i: (0, 0))


def _rstd(v):
    return lax.rsqrt(jnp.mean(v * v, axis=-1, keepdims=True) + RMS_EPS)


def _rms_fwd(x, g, *, name):
    t, d = x.shape

    def body(x_ref, g_ref, h_ref):
        xv = x_ref[...]
        h_ref[...] = ((xv * _rstd(xv)) * g_ref[...]).astype(BF16)

    return pl.pallas_call(
        body, name=name, grid=(t // ROWS,),
        in_specs=[_row_spec(d), _vec_spec(d)],
        out_specs=_row_spec(d),
        out_shape=jax.ShapeDtypeStruct((t, d), BF16),
        compiler_params=_params("parallel"),
    )(x, g)


def _resid_rms(res, y, g_post, g_next, *, name):
    t, d = res.shape
    with_next = g_next is not None

    def body(res_ref, y_ref, gp_ref, *rest):
        yv = y_ref[...]
        xn = res_ref[...] + (yv * _rstd(yv)) * gp_ref[...]
        if with_next:
            gn_ref, xo_ref, h_ref = rest
            h_ref[...] = ((xn * _rstd(xn)) * gn_ref[...]).astype(BF16)
        else:
            (xo_ref,) = rest
        xo_ref[...] = xn

    in_specs = [_row_spec(d), _row_spec(d), _vec_spec(d)]
    args = [res, y, g_post]
    out_specs = [_row_spec(d)]
    out_shape = [jax.ShapeDtypeStruct((t, d), F32)]
    if with_next:
        in_specs.append(_vec_spec(d))
        args.append(g_next)
        out_specs.append(_row_spec(d))
        out_shape.append(jax.ShapeDtypeStruct((t, d), BF16))
    return pl.pallas_call(
        body, name=name, grid=(t // ROWS,),
        in_specs=in_specs, out_specs=tuple(out_specs), out_shape=tuple(out_shape),
        compiler_params=_params("parallel"),
    )(*args)


def _rms_bwd(dy, xin, g, res, *, out_dtype, name):
    t, d = xin.shape
    with_res = res is not None

    def body(dy_ref, x_ref, g_ref, *rest):
        if with_res:
            res_ref, dx_ref, dg_ref = rest
        else:
            dx_ref, dg_ref = rest
        xv = x_ref[...]
        r = _rstd(xv)
        nrm = xv * r
        dyv = dy_ref[...]
        dn = dyv * g_ref[...]
        dx = r * (dn - nrm * jnp.mean(dn * nrm, axis=-1, keepdims=True))
        if with_res:
            dx = dx + res_ref[...]
        dx_ref[...] = dx.astype(dx_ref.dtype)

        @pl.when(pl.program_id(0) == 0)
        def _():
            dg_ref[...] = jnp.zeros_like(dg_ref)

        dg_ref[...] += jnp.sum(dyv * nrm, axis=0, keepdims=True)

    in_specs = [_row_spec(d), _row_spec(d), _vec_spec(d)]
    args = [dy, xin, g]
    if with_res:
        in_specs.append(_row_spec(d))
        args.append(res)
    return pl.pallas_call(
        body, name=name, grid=(t // ROWS,),
        in_specs=in_specs,
        out_specs=(_row_spec(d), _vec_spec(d)),
        out_shape=(jax.ShapeDtypeStruct((t, d), out_dtype), jax.ShapeDtypeStruct((1, d), F32)),
        compiler_params=_params("arbitrary"),
    )(*args)


def _loss_head(y, target):
    t, d = y.shape

    def body(y_ref, t_ref, dy_ref, loss_ref):
        diff = y_ref[...] - t_ref[...]
        dy_ref[...] = diff * (1.0 / d)

        @pl.when(pl.program_id(0) == 0)
        def _():
            loss_ref[...] = jnp.zeros_like(loss_ref)

        loss_ref[...] += jnp.sum(diff * diff) * (0.5 / d)

    return pl.pallas_call(
        body, name="loss_head", grid=(t // ROWS,),
        in_specs=[_row_spec(d), _row_spec(d)],
        out_specs=(_row_spec(d), pl.BlockSpec((8, 128), lambda i: (0, 0))),
        out_shape=(jax.ShapeDtypeStruct((t, d), F32), jax.ShapeDtypeStruct((8, 128), F32)),
        compiler_params=_params("arbitrary"),
    )(y, target)


def _halo_specs(t, col, width):
    rb = ROWS // HALO
    last = t // HALO - 1
    return [pl.BlockSpec((ROWS, width), lambda i: (i, col)),
            pl.BlockSpec((HALO, width), lambda i: (jnp.maximum(i * rb - 1, 0), col)),
            pl.BlockSpec((HALO, width), lambda i: (jnp.minimum((i + 1) * rb, last), col))]


def _glu(a_ref, g_ref):
    return a_ref[...].astype(F32) * jax.nn.sigmoid(g_ref[...].astype(F32))


def _fill_ext(ext, cur, prev, nxt, i, nblk):
    ext[pl.ds(HALO, ROWS), :] = cur
    ext[pl.ds(0, HALO), :] = jnp.where(i > 0, prev, 0.0)
    ext[pl.ds(HALO + ROWS, HALO), :] = jnp.where(i < nblk - 1, nxt, 0.0)


def _conv_fwd(proj, w_dw, b_dw, ln_g, ln_b):
    t = proj.shape[0]
    nblk = t // ROWS
    pad = HALO - CONV_K // 2

    def body(ac, ap, an, gc, gp, gn, w_ref, b_ref, lg_ref, lb_ref, yc_ref, uc_ref, uext):
        i = pl.program_id(0)
        _fill_ext(uext, _glu(ac, gc), _glu(ap, gp), _glu(an, gn), i, nblk)
        acc = jnp.broadcast_to(b_ref[...], (ROWS, CONV_W))
        for j in range(CONV_K):
            acc = acc + uext[pl.ds(j + pad, ROWS), :] * w_ref[pl.ds(j, 1), :]
        uc_ref[...] = acc
        mu = jnp.mean(acc, axis=-1, keepdims=True)
        xc = acc - mu
        var = jnp.mean(xc * xc, axis=-1, keepdims=True)
        yln = xc * lax.rsqrt(var + LN_EPS) * lg_ref[...] + lb_ref[...]
        yc_ref[...] = (yln * jax.nn.sigmoid(yln)).astype(BF16)

    vec = pl.BlockSpec((1, CONV_W), lambda i: (0, 0))
    return pl.pallas_call(
        body, name="conv_fwd", grid=(nblk,),
        in_specs=_halo_specs(t, 0, CONV_W) + _halo_specs(t, 1, CONV_W)
        + [pl.BlockSpec((32, CONV_W), lambda i: (0, 0)), vec, vec, vec],
        out_specs=(pl.BlockSpec((ROWS, CONV_W), lambda i: (i, 0)),
                   pl.BlockSpec((ROWS, CONV_W), lambda i: (i, 0))),
        out_shape=(jax.ShapeDtypeStruct((t, CONV_W), BF16), jax.ShapeDtypeStruct((t, CONV_W), F32)),
        scratch_shapes=[pltpu.VMEM((ROWS + 2 * HALO, CONV_W), F32)],
        compiler_params=_params("parallel"),
    )(proj, proj, proj, proj, proj, proj, w_dw, b_dw, ln_g, ln_b)


def _conv_bwd(proj, uc, dcat, w_dw, ln_g, ln_b):
    t = proj.shape[0]
    nblk = t // ROWS
    pad = HALO - CONV_K // 2

    def body(ac, ap, an, gc, gp, gn, uc_c, uc_p, uc_n, dy_c, dy_p, dy_n, w_ref, lg_ref, lb_ref,
             dag_ref, dw_ref, dvec_ref, uext, dext):
        i = pl.program_id(0)
        _fill_ext(uext, _glu(ac, gc), _glu(ap, gp), _glu(an, gn), i, nblk)

        def ln_bwd(u_ref, d_ref):
            u = u_ref[...]
            mu = jnp.mean(u, axis=-1, keepdims=True)
            xc = u - mu
            rstd = lax.rsqrt(jnp.mean(xc * xc, axis=-1, keepdims=True) + LN_EPS)
            xhat = xc * rstd
            yln = xhat * lg_ref[...] + lb_ref[...]
            sg = jax.nn.sigmoid(yln)
            dyln = d_ref[...] * (sg * (1.0 + yln * (1.0 - sg)))
            dxh = dyln * lg_ref[...]
            du = rstd * (dxh - jnp.mean(dxh, axis=-1, keepdims=True)
                         - xhat * jnp.mean(dxh * xhat, axis=-1, keepdims=True))
            return du, dyln, xhat

        duc, dyln, xhat = ln_bwd(uc_c, dy_c)
        _fill_ext(dext, duc, ln_bwd(uc_p, dy_p)[0], ln_bwd(uc_n, dy_n)[0], i, nblk)

        @pl.when(i == 0)
        def _():
            dw_ref[...] = jnp.zeros_like(dw_ref)
            dvec_ref[...] = jnp.zeros_like(dvec_ref)

        dvec_ref[pl.ds(0, 1), :] += jnp.sum(duc, axis=0, keepdims=True)
        dvec_ref[pl.ds(1, 1), :] += jnp.sum(dyln * xhat, axis=0, keepdims=True)
        dvec_ref[pl.ds(2, 1), :] += jnp.sum(dyln, axis=0, keepdims=True)

        du = jnp.zeros((ROWS, CONV_W), F32)
        for j in range(CONV_K):
            du = du + dext[pl.ds(2 * HALO - pad - j, ROWS), :] * w_ref[pl.ds(j, 1), :]
            dw_ref[pl.ds(j, 1), :] += jnp.sum(duc * uext[pl.ds(j + pad, ROWS), :], axis=0, keepdims=True)

        a = ac[...].astype(F32)
        sg = jax.nn.sigmoid(gc[...].astype(F32))
        dag_ref[:, pl.ds(0, CONV_W)] = (du * sg).astype(BF16)
        dag_ref[:, pl.ds(CONV_W, CONV_W)] = (du * a * sg * (1.0 - sg)).astype(BF16)

    vec = pl.BlockSpec((1, CONV_W), lambda i: (0, 0))
    return pl.pallas_call(
        body, name="conv_bwd", grid=(nblk,),
        in_specs=_halo_specs(t, 0, CONV_W) + _halo_specs(t, 1, CONV_W) + _halo_specs(t, 0, CONV_W)
        + _halo_specs(t, 0, CONV_W) + [pl.BlockSpec((32, CONV_W), lambda i: (0, 0)), vec, vec],
        out_specs=(pl.BlockSpec((ROWS, 2 * CONV_W), lambda i: (i, 0)),
                   pl.BlockSpec((32, CONV_W), lambda i: (0, 0)),
                   pl.BlockSpec((8, CONV_W), lambda i: (0, 0))),
        out_shape=(jax.ShapeDtypeStruct((t, 2 * CONV_W), BF16),
                   jax.ShapeDtypeStruct((32, CONV_W), F32),
                   jax.ShapeDtypeStruct((8, CONV_W), F32)),
        scratch_shapes=[pltpu.VMEM((ROWS + 2 * HALO, CONV_W), F32),
                        pltpu.VMEM((ROWS + 2 * HALO, CONV_W), F32)],
        compiler_params=_params("arbitrary"),
    )(proj, proj, proj, proj, proj, proj, uc, uc, uc, dcat, dcat, dcat, w_dw, ln_g, ln_b)


Q_BLK, K_BLK, V_BLK = 16, 24, 32


def _head_masks():
    lane = lax.broadcasted_iota(jnp.int32, (1, 2 * HEAD_D), 1)
    return [lane < HEAD_D, lane >= HEAD_D]


def _band(r, rows):
    rs = jnp.clip(r - WIN_ROWS // 2, 0, rows - WIN_ROWS)
    return rs, r - rs, pl.multiple_of(r * GRID_W, GRID_W), pl.multiple_of(rs * GRID_W, GRID_W)


def _softmax_band(qh, kb, bias):
    s = lax.dot_general(qh, kb, (((1,), (1,)), ((), ())), preferred_element_type=F32)
    s = s * (HEAD_D ** -0.5) + bias
    e = jnp.exp(s - jnp.max(s, axis=-1, keepdims=True))
    return e / jnp.sum(e, axis=-1, keepdims=True)


def _natten_fwd(proj, bias, comm=None):
    t = proj.shape[0]
    rows = t // GRID_W

    def body(q_ref, k_ref, v_ref, b_ref, o_ref):
        masks = _head_masks()

        def step(r, carry):
            _, off, q0, k0 = _band(r, rows)
            qp = q_ref[pl.ds(q0, GRID_W), :]
            kb = k_ref[pl.ds(k0, BAND), :]
            vb = v_ref[pl.ds(k0, BAND), :]
            outs = []
            for hh in range(2):
                qh = jnp.where(masks[hh], qp, jnp.zeros_like(qp))
                p = _softmax_band(qh, kb, b_ref[hh, off])
                outs.append(jnp.dot(p.astype(BF16), vb, preferred_element_type=F32))
            o_ref[pl.ds(q0, GRID_W), :] = jnp.where(masks[0], outs[0], outs[1]).astype(BF16)
            return carry

        lax.fori_loop(0, rows, step, 0)

    col = lambda b: pl.BlockSpec((t, 2 * HEAD_D), lambda h: (0, b + h))
    return _call(
        body, name="natten_fwd", grid=(N_HEADS // 2,),
        in_specs=[col(Q_BLK), col(K_BLK), col(V_BLK),
                  pl.BlockSpec((2, WIN_ROWS, GRID_W, BAND), lambda h: (h, 0, 0, 0))],
        out_specs=[pl.BlockSpec((t, 2 * HEAD_D), lambda h: (0, h))],
        out_shape=[jax.ShapeDtypeStruct((t, NA_W), BF16)],
        sem=("parallel",), args=(proj, proj, proj, bias), comm=comm)


def _natten_bwd(proj, bias, dcat, comm=None):
    t = proj.shape[0]
    rows = t // GRID_W

    def body(q_ref, k_ref, v_ref, b_ref, do_ref, dq_ref, dk_ref, dv_ref, db_ref, dk_acc, dv_acc):
        masks = _head_masks()
        dk_acc[...] = jnp.zeros_like(dk_acc)
        dv_acc[...] = jnp.zeros_like(dv_acc)
        db_ref[...] = jnp.zeros_like(db_ref)
        tn = (((0,), (0,)), ((), ()))

        def step(r, carry):
            _, off, q0, k0 = _band(r, rows)
            qp = q_ref[pl.ds(q0, GRID_W), :]
            kb = k_ref[pl.ds(k0, BAND), :]
            vb = v_ref[pl.ds(k0, BAND), :]
            dop = do_ref[pl.ds(q0, GRID_W), :]
            dob = dop.astype(BF16)
            dq, dk, dv = [], [], []
            for hh in range(2):
                qh = jnp.where(masks[hh], qp, jnp.zeros_like(qp))
                doh = jnp.where(masks[hh], dob, jnp.zeros_like(dob))
                p = _softmax_band(qh, kb, b_ref[hh, off])
                dp = lax.dot_general(doh, vb, (((1,), (1,)), ((), ())), preferred_element_type=F32)
                ds = p * (dp - jnp.sum(p * dp, axis=-1, keepdims=True))
                db_ref[hh, off] += ds
                dsb = ds.astype(BF16)
                dq.append(jnp.dot(dsb, kb, preferred_element_type=F32))
                dk.append(lax.dot_general(dsb, qp, tn, preferred_element_type=F32))
                dv.append(lax.dot_general(p.astype(BF16), dob, tn, preferred_element_type=F32))
            scale = HEAD_D ** -0.5
            dq_ref[pl.ds(q0, GRID_W), :] = (jnp.where(masks[0], dq[0], dq[1]) * scale).astype(BF16)
            dk_acc[pl.ds(k0, BAND), :] += jnp.where(masks[0], dk[0], dk[1]) * scale
            dv_acc[pl.ds(k0, BAND), :] += jnp.where(masks[0], dv[0], dv[1])
            return carry

        lax.fori_loop(0, rows, step, 0)
        dk_ref[...] = dk_acc[...].astype(BF16)
        dv_ref[...] = dv_acc[...].astype(BF16)

    col = lambda b: pl.BlockSpec((t, 2 * HEAD_D), lambda h: (0, b + h))
    b_spec = pl.BlockSpec((2, WIN_ROWS, GRID_W, BAND), lambda h: (h, 0, 0, 0))
    o_spec = pl.BlockSpec((t, 2 * HEAD_D), lambda h: (0, h))
    o_shape = jax.ShapeDtypeStruct((t, NA_W), BF16)
    return _call(
        body, name="natten_bwd", grid=(N_HEADS // 2,),
        in_specs=[col(Q_BLK), col(K_BLK), col(V_BLK), b_spec, col(CONV_W // (2 * HEAD_D))],
        out_specs=[o_spec, o_spec, o_spec, b_spec],
        out_shape=[o_shape, o_shape, o_shape,
                   jax.ShapeDtypeStruct((N_HEADS, WIN_ROWS, GRID_W, BAND), F32)],
        scratch_shapes=[pltpu.VMEM((t, 2 * HEAD_D), F32), pltpu.VMEM((t, 2 * HEAD_D), F32)],
        sem=("parallel",), args=(proj, proj, proj, bias, dcat), comm=comm)


def _bias_index_tables():
    off = np.arange(WIN_ROWS)
    kr = np.arange(WIN_ROWS)
    c = np.arange(GRID_W)
    dr = kr[None, :] - off[:, None] + (WIN_ROWS - 1)
    dc = np.clip(c[None, :] - c[:, None], -(WIN_COLS - 1), WIN_COLS - 1) + (WIN_COLS - 1)
    cs = np.clip(c - WIN_COLS // 2, 0, GRID_W - WIN_COLS)
    mask = (c[None, :] >= cs[:, None]) & (c[None, :] < cs[:, None] + WIN_COLS)
    return dr, dc, mask


def _bias_table(rpb_l):
    _, dc, mask = _bias_index_tables()
    n_dc = 32
    et = np.zeros((n_dc, GRID_W * GRID_W), np.float32)
    et[dc.reshape(-1), np.arange(GRID_W * GRID_W)] = 1.0
    rp = jnp.pad(rpb_l, ((0, 0), (0, 1), (0, 1))).reshape(N_HEADS * 16, n_dc)

    def body(r_ref, e_ref, o_ref):
        o_ref[...] = jnp.dot(r_ref[...], e_ref[...], preferred_element_type=F32, precision=lax.Precision.HIGHEST)

    vm = pl.BlockSpec(memory_space=pltpu.VMEM)
    ge = pl.pallas_call(
        body, name="rpb_expand", in_specs=[vm, vm], out_specs=vm,
        out_shape=jax.ShapeDtypeStruct((N_HEADS * 16, GRID_W * GRID_W), F32),
        compiler_params=pltpu.CompilerParams(vmem_limit_bytes=VMEM_LIMIT),
    )(rp, jnp.asarray(et))
    ge = ge.reshape(N_HEADS, 16, GRID_W, GRID_W)
    last = WIN_ROWS - 1
    b = jnp.stack([ge[:, last - off:last - off + WIN_ROWS] for off in range(WIN_ROWS)], axis=1)
    b = jnp.where(mask[None, None, None, :, :], b, NEG_INF).transpose(0, 1, 3, 2, 4)
    return b.reshape(N_HEADS, WIN_ROWS, GRID_W, BAND)


def _rpb_grad(dbias):
    dr, dc, _ = _bias_index_tables()
    e = np.zeros((GRID_W * GRID_W, 128), np.float32)
    e[np.arange(GRID_W * GRID_W), dc.reshape(-1)] = 1.0
    f = np.zeros((16, WIN_ROWS * WIN_ROWS), np.float32)
    f[dr.reshape(-1), np.arange(WIN_ROWS * WIN_ROWS)] = 1.0
    z = dbias.reshape(N_HEADS, WIN_ROWS, GRID_W, WIN_ROWS, GRID_W).transpose(0, 1, 3, 2, 4)
    z = z.reshape(N_HEADS, WIN_ROWS * WIN_ROWS, GRID_W * GRID_W)

    def body(z_ref, e_ref, f_ref, o_ref):
        hi = lax.Precision.HIGHEST
        zd = jnp.dot(z_ref[...], e_ref[...], preferred_element_type=F32, precision=hi)
        o_ref[...] = jnp.dot(f_ref[...], zd, preferred_element_type=F32, precision=hi)

    out = pl.pallas_call(
        body, name="rpb_grad", grid=(N_HEADS,),
        in_specs=[pl.BlockSpec((None, WIN_ROWS * WIN_ROWS, GRID_W * GRID_W), lambda h: (h, 0, 0)),
                  pl.BlockSpec((GRID_W * GRID_W, 128), lambda h: (0, 0)),
                  pl.BlockSpec((16, WIN_ROWS * WIN_ROWS), lambda h: (0, 0))],
        out_specs=pl.BlockSpec((None, 16, 128), lambda h: (h, 0, 0)),
        out_shape=jax.ShapeDtypeStruct((N_HEADS, 16, 128), F32),
        compiler_params=_params("parallel"),
    )(z, jnp.asarray(e), jnp.asarray(f))
    return out[:, :2 * WIN_ROWS - 1, :2 * WIN_COLS - 1]


def _cast_bf16(w2d, *, name):
    r, c = w2d.shape
    tr = _tile(r, (512, 256))

    def body(w_ref, o_ref):
        o_ref[...] = w_ref[...].astype(BF16)

    return pl.pallas_call(
        body, name=name, grid=(r // tr,),
        in_specs=[pl.BlockSpec((tr, c), lambda i: (i, 0))],
        out_specs=pl.BlockSpec((tr, c), lambda i: (i, 0)),
        out_shape=jax.ShapeDtypeStruct((r, c), BF16),
        compiler_params=_params("parallel"),
    )(w2d)


def _core_index():
    return lax.axis_index("c").astype(jnp.int32).reshape(1)


def _pair_sum(dw, other, *, name):
    s, r, c = dw.shape
    h = r // 2
    tr = _tile(h, (256, 128))
    nb = h // tr

    def body(c_ref, a_ref, b_ref, o_ref):
        del c_ref
        o_ref[...] = (a_ref[...].astype(F32) + b_ref[...].astype(F32)).astype(BF16)

    return pl.pallas_call(
        body, name=name,
        grid_spec=pltpu.PrefetchScalarGridSpec(
            num_scalar_prefetch=1, grid=(s, nb),
            in_specs=[pl.BlockSpec((None, tr, c), lambda si, i, cr: (si, cr[0] * nb + i, 0)),
                      pl.BlockSpec((None, tr, c), lambda si, i, cr: (si, i, 0))],
            out_specs=pl.BlockSpec((None, tr, c), lambda si, i, cr: (si, i, 0))),
        out_shape=jax.ShapeDtypeStruct((s, h, c), BF16),
        compiler_params=_params("parallel", "parallel"),
    )(_core_index(), dw, other)


def _chip_sum(parts, grad, layer, *, name):
    s, h, c = parts.shape
    tr = _tile(h, (256, 128))
    nb = h // tr

    def body(c_ref, p_ref, g_in, o_ref):
        del c_ref, g_in
        acc = p_ref[0].astype(F32) + p_ref[1].astype(F32)
        acc = acc + p_ref[2].astype(F32)
        o_ref[...] = acc + p_ref[3].astype(F32)

    return pl.pallas_call(
        body, name=name,
        grid_spec=pltpu.PrefetchScalarGridSpec(
            num_scalar_prefetch=1, grid=(nb,),
            in_specs=[pl.BlockSpec((s, tr, c), lambda i, cr: (0, i, 0)), ANY],
            out_specs=pl.BlockSpec((None, tr, c), lambda i, cr: (layer, cr[0] * nb + i, 0))),
        out_shape=jax.ShapeDtypeStruct(grad.shape, F32),
        input_output_aliases={2: 0},
        compiler_params=_params("parallel"),
    )(_core_index(), parts, grad)


def _adamw(w, g, m, v, *, name):
    r, c = w.shape
    tr = _tile(r, (256, 128, 8))
    tr = tr if r % tr == 0 else r
    bc1 = 1.0 - ADAM_B1 ** ADAM_STEP
    bc2 = 1.0 - ADAM_B2 ** ADAM_STEP

    def body(w_ref, g_ref, m_ref, v_ref, go_ref, d_ref, mo_ref, vo_ref):
        gv = g_ref[...]
        mn = ADAM_B1 * m_ref[...] + (1.0 - ADAM_B1) * gv
        vn = ADAM_B2 * v_ref[...] + (1.0 - ADAM_B2) * (gv * gv)
        go_ref[...] = gv
        mo_ref[...] = mn
        vo_ref[...] = vn
        d_ref[...] = -ADAM_LR * ((mn / bc1) / (jnp.sqrt(vn / bc2) + ADAM_EPS) + ADAM_WD * w_ref[...])

    spec = pl.BlockSpec((tr, c), lambda i: (i, 0))
    shape = jax.ShapeDtypeStruct((r, c), F32)
    return pl.pallas_call(
        body, name=name, grid=(r // tr,),
        in_specs=[spec] * 4, out_specs=(spec,) * 4, out_shape=(shape,) * 4,
        compiler_params=_params("parallel"),
    )(w, g, m, v)


def _me():
    return lax.axis_index("x"), lax.axis_index("y"), lax.axis_index("c")


def _other_chips(x, y):
    return [(1 - x, y), (x, 1 - y), (1 - x, 1 - y)]


def _remote(src, dst, send, recv, k, to):
    return pltpu.make_async_remote_copy(src_ref=src, dst_ref=dst, send_sem=send.at[k], recv_sem=recv.at[k],
                                        device_id=to, device_id_type=MESH)


def _run_comm(comm, *, name):
    def body(*refs):
        ni, no = len(comm.ins), len(comm.out_shape)
        ins, outs, sems = refs[:ni], refs[ni:ni + no], refs[ni + no:]
        comm.start(ins, outs, sems)
        comm.finish(ins, outs, sems)

    return pl.pallas_call(
        body, name=name,
        in_specs=[ANY] * len(comm.ins), out_specs=[ANY] * len(comm.out_shape),
        out_shape=list(comm.out_shape), scratch_shapes=list(comm.sems),
    )(*comm.ins)


def _gather_comm(layer, shards):
    nw = len(shards)

    def half(ref, chip, hc):
        hr = ref.shape[1] // 2
        return ref.at[chip, pl.ds(hc * hr, hr), :]

    def sent(ins, outs, sems):
        send, recv, local = sems
        x, y, c = _me()
        me_chip = 2 * x + y
        mine = [pltpu.make_async_copy(ins[w].at[layer], outs[w].at[me_chip], local.at[w]) for w in range(nw)]
        first = []
        for w in range(nw):
            hr = ins[w].shape[1] // 2
            src = ins[w].at[layer, pl.ds(c * hr, hr), :]
            for j, chip in enumerate(_other_chips(x, y)):
                first.append(_remote(src, half(outs[w], me_chip, c), send, recv, 6 * w + j, (*chip, c)))
        return mine, first

    def start(ins, outs, sems):
        mine, first = sent(ins, outs, sems)
        for cp in mine + first:
            cp.start()

    def finish(ins, outs, sems):
        send, recv, _ = sems
        mine, first = sent(ins, outs, sems)
        x, y, c = _me()
        sib = (x, y, 1 - c)
        chips = _other_chips(x, y)
        passed = []
        for w in range(nw):
            for j, chip in enumerate(chips):
                landed = half(outs[w], 2 * chip[0] + chip[1], c)
                _remote(landed, landed, send, recv, 6 * w + j, sib).wait_recv()
                cp = _remote(landed, landed, send, recv, 6 * w + 3 + j, sib)
                cp.start()
                passed.append(cp)
        for w in range(nw):
            for j, chip in enumerate(chips):
                theirs = half(outs[w], 2 * chip[0] + chip[1], 1 - c)
                _remote(theirs, theirs, send, recv, 6 * w + 3 + j, sib).wait_recv()
        for cp in first + passed:
            cp.wait_send()
        for cp in mine:
            cp.wait()

    return _Comm(list(shards), [jax.ShapeDtypeStruct((N_CHIPS,) + s.shape[1:], BF16) for s in shards],
                 [pltpu.SemaphoreType.DMA((6 * nw,)), pltpu.SemaphoreType.DMA((6 * nw,)),
                  pltpu.SemaphoreType.DMA((nw,))], start, finish)


def _pair_exchange(dws):
    nw = len(dws)

    def body(*refs):
        ins, outs = refs[:nw], refs[nw:2 * nw]
        send, recv = refs[2 * nw:]
        x, y, c = _me()
        cps = []
        for w in range(nw):
            hr = ins[w].shape[1] // 2
            src = ins[w].at[:, pl.ds((1 - c) * hr, hr), :]
            cps.append(_remote(src, outs[w], send, recv, w, (x, y, 1 - c)))
        for cp in cps:
            cp.start()
        for cp in cps:
            cp.wait()

    return pl.pallas_call(
        body, name="grad_pair_exchange",
        in_specs=[ANY] * nw, out_specs=[ANY] * nw,
        out_shape=[jax.ShapeDtypeStruct((d.shape[0], d.shape[1] // 2, d.shape[2]), BF16) for d in dws],
        scratch_shapes=[pltpu.SemaphoreType.DMA((nw,)), pltpu.SemaphoreType.DMA((nw,))],
    )(*dws)


def _scatter_comm(qs):
    nw = len(qs)

    def sent(ins, outs, sems):
        send, recv, local = sems
        x, y, c = _me()
        me_chip = 2 * x + y
        mine = [pltpu.make_async_copy(ins[w].at[me_chip], outs[w].at[me_chip], local.at[w]) for w in range(nw)]
        cps = []
        for w in range(nw):
            for j, chip in enumerate(_other_chips(x, y)):
                cps.append(_remote(ins[w].at[2 * chip[0] + chip[1]], outs[w].at[me_chip], send, recv,
                                   3 * w + j, (*chip, c)))
        return mine, cps

    def start(ins, outs, sems):
        mine, cps = sent(ins, outs, sems)
        for cp in mine + cps:
            cp.start()

    def finish(ins, outs, sems):
        send, recv, _ = sems
        mine, cps = sent(ins, outs, sems)
        x, y, c = _me()
        for w in range(nw):
            for j, chip in enumerate(_other_chips(x, y)):
                theirs = outs[w].at[2 * chip[0] + chip[1]]
                _remote(theirs, theirs, send, recv, 3 * w + j, (*chip, c)).wait_recv()
        for cp in cps:
            cp.wait_send()
        for cp in mine:
            cp.wait()

    return _Comm(list(qs), [jax.ShapeDtypeStruct(q.shape, BF16) for q in qs],
                 [pltpu.SemaphoreType.DMA((3 * nw,)), pltpu.SemaphoreType.DMA((3 * nw,)),
                  pltpu.SemaphoreType.DMA((nw,))], start, finish)


def _sibling_fill(layer, grads):
    nw = len(grads)

    def body(*refs):
        ins, outs = refs[:nw], refs[nw:2 * nw]
        send, recv = refs[2 * nw:]
        del ins
        x, y, c = _me()
        cps = []
        for w in range(nw):
            hr = outs[w].shape[1] // 2
            mine = outs[w].at[layer, pl.ds(c * hr, hr), :]
            cps.append(_remote(mine, mine, send, recv, w, (x, y, 1 - c)))
        for cp in cps:
            cp.start()
        for w in range(nw):
            hr = outs[w].shape[1] // 2
            theirs = outs[w].at[layer, pl.ds((1 - c) * hr, hr), :]
            _remote(theirs, theirs, send, recv, w, (x, y, 1 - c)).wait_recv()
        for cp in cps:
            cp.wait_send()

    return pl.pallas_call(
        body, name=f"grad_sibling_fill_{layer}",
        in_specs=[ANY] * nw, out_specs=[ANY] * nw,
        out_shape=[jax.ShapeDtypeStruct(g.shape, F32) for g in grads],
        input_output_aliases={w: w for w in range(nw)},
        scratch_shapes=[pltpu.SemaphoreType.DMA((nw,)), pltpu.SemaphoreType.DMA((nw,))],
    )(*grads)


def _all_devices(pack, *, reduce, name):
    r = pack.shape[0]

    def body(p_ref, o_ref, buf, send, recv):
        x, y, c = _me()
        me = 4 * x + 2 * y + c
        buf[me] = p_ref[...]
        flips = [(fx, fy, fc) for fx in (0, 1) for fy in (0, 1) for fc in (0, 1)][1:]
        peers = [(x ^ fx, y ^ fy, c ^ fc) for fx, fy, fc in flips]
        cps = [_remote(p_ref, buf.at[me], send, recv, k, peer) for k, peer in enumerate(peers)]
        for cp in cps:
            cp.start()
        for k, (px, py, pc) in enumerate(peers):
            theirs = buf.at[4 * px + 2 * py + pc]
            _remote(theirs, theirs, send, recv, k, (px, py, pc)).wait_recv()
        for cp in cps:
            cp.wait_send()
        if reduce:
            acc = buf[0]
            for d in range(1, N_DEV):
                acc = acc + buf[d]
            o_ref[...] = acc
        else:
            o_ref[...] = buf[...]

    vm = pl.BlockSpec(memory_space=pltpu.VMEM)
    return pl.pallas_call(
        body, name=name,
        in_specs=[vm], out_specs=vm,
        out_shape=jax.ShapeDtypeStruct((r, 128) if reduce else (N_DEV, r, 128), F32),
        scratch_shapes=[pltpu.VMEM((N_DEV, r, 128), F32), pltpu.SemaphoreType.DMA((N_DEV - 1,)),
                        pltpu.SemaphoreType.DMA((N_DEV - 1,))],
        compiler_params=pltpu.CompilerParams(vmem_limit_bytes=VMEM_LIMIT),
    )(pack)


def _to_rows(flat):
    v = flat.reshape(-1)
    rows = -(-v.shape[0] // 1024) * 8
    return jnp.pad(v, (0, rows * 128 - v.shape[0])).reshape(rows, 128)


def kernel(x, w_in, w_dw, b_dw, conv_ln_g, conv_ln_b, rpb, w_out, w_up, w_down, pre_mix_g, post_mix_g, pre_mlp_g, post_mlp_g, loss_target, m_w_in, m_w_dw, m_b_dw, m_conv_ln_g, m_conv_ln_b, m_rpb, m_w_out, m_w_up, m_w_down, m_pre_mix_g, m_post_mix_g, m_pre_mlp_g, m_post_mlp_g, v_w_in, v_w_dw, v_b_dw, v_conv_ln_g, v_conv_ln_b, v_rpb, v_w_out, v_w_up, v_w_down, v_pre_mix_g, v_post_mix_g, v_pre_mlp_g, v_post_mlp_g):
    depth = w_in.shape[0]
    t = x.shape[1]
    xl = x.reshape(t, D_MODEL)
    target = loss_target.reshape(t, D_MODEL)
    chip = 2 * lax.axis_index("x") + lax.axis_index("y")

    big = {"w_in": w_in, "w_out": w_out, "w_up": w_up, "w_down": w_down}
    big_names = list(big)
    shards = [_cast_bf16(big[n].reshape(-1, big[n].shape[-1]), name=f"cast_{n}").reshape(big[n].shape)
              for n in big_names]

    wdw_all = _all_devices(_to_rows(w_dw), reduce=False, name="gather_w_dw")[::2]
    wdw_all = wdw_all.reshape(N_CHIPS, -1)[:, :w_dw.size].reshape((N_CHIPS,) + w_dw.shape)
    wdw_full = jnp.moveaxis(wdw_all, 0, 2).reshape(depth, CONV_K, CONV_W)
    wdw_pad = jnp.pad(wdw_full, ((0, 0), (0, 32 - CONV_K), (0, 0)))

    vec = lambda a, l: a[l].reshape(1, -1)

    saved = []
    h = _rms_fwd(xl, vec(pre_mix_g, 0), name="rms_first")
    gathered = _run_comm(_gather_comm(0, shards), name="gather_weights_first")
    for l in range(depth):
        g_in, g_out, g_up, g_down = gathered
        g_out = g_out.reshape(1, D_MODEL, D_MODEL)
        g_down = g_down.reshape(1, D_FF, D_MODEL)
        more = l + 1 < depth
        bias = _bias_table(rpb[l])
        (proj,) = _mm_nn(h, g_in, out_dtype=BF16, name="proj")
        yc, uc = _conv_fwd(proj, wdw_pad[l], vec(b_dw, l), vec(conv_ln_g, l), vec(conv_ln_b, l))
        ya, *n_io = _natten_fwd(proj, bias, _gather_comm(l + 1, shards[:2]) if more else None)
        cat = jnp.concatenate([yc, ya], axis=1)
        (mix,) = _mm_nn(cat, g_out, out_dtype=F32, name="out_proj")
        x1, h2 = _resid_rms(xl, mix, vec(post_mix_g, l), vec(pre_mlp_g, l), name="mix_resid")
        up, act, *n_up = _mm_nn(h2, g_up, out_dtype=BF16, relu2=True, name="mlp_up",
                                comm=_gather_comm(l + 1, shards[2:3]) if more else None)
        f, *n_down = _mm_nn(act, g_down, out_dtype=F32, name="mlp_down",
                            comm=_gather_comm(l + 1, shards[3:]) if more else None)
        gathered = n_io + n_up + n_down
        saved.append(dict(x=xl, h=h, proj=proj, uc=uc, cat=cat, mix=mix, x1=x1, h2=h2, up=up, act=act, f=f,
                          bias=bias, w=(g_in, g_out, g_up, g_down)))
        if l + 1 < depth:
            xl, h = _resid_rms(x1, f, vec(post_mlp_g, l), vec(pre_mix_g, l + 1), name="mlp_resid")
        else:
            (xl,) = _resid_rms(x1, f, vec(post_mlp_g, l), None, name="mlp_resid_last")

    dx, loss_blk = _loss_head(xl, target)
    loss = lax.psum(loss_blk[0, 0], ("x", "y", "c"))

    grads = [lax.empty(big[n].shape, F32) for n in big_names]
    small = [None] * depth
    pending = None

    def finish_layer(layer, parts, grads):
        grads = [_chip_sum(p, g, layer, name=f"chip_sum_{n}_{layer}") for p, g, n in zip(parts, grads, big_names)]
        return _sibling_fill(layer, grads)

    for l in reversed(range(depth)):
        sv = saved[l]
        g_in, g_out, g_up, g_down = sv["w"]
        df, dg_post_mlp = _rms_bwd(dx, sv["f"], vec(post_mlp_g, l), None, out_dtype=BF16, name="rms_bwd_mlp_post")
        dup = _mm_nt(df, g_down, out_dtype=BF16, up=sv["up"], name="d_act")
        dw_down = _mm_tn(sv["act"], df, 1, name="dw_down")
        dh2 = _mm_nt(dup, g_up, out_dtype=F32, name="d_h2")
        dw_up = _mm_tn(sv["h2"], dup, N_CHIPS, name="dw_up")
        dx1, dg_pre_mlp = _rms_bwd(dh2, sv["x1"], vec(pre_mlp_g, l), dx, out_dtype=F32, name="rms_bwd_mlp_pre")
        dmix, dg_post_mix = _rms_bwd(dx1, sv["mix"], vec(post_mix_g, l), None, out_dtype=BF16, name="rms_bwd_mix_post")
        dcat = _mm_nt(dmix, g_out, out_dtype=F32, name="d_cat")
        dw_out = _mm_tn(sv["cat"], dmix, 1, name="dw_out")
        dq, dk, dv, dbias, *parts = _natten_bwd(sv["proj"], sv["bias"], dcat,
                                                _scatter_comm(pending[1]) if pending else None)
        if pending:
            grads = finish_layer(pending[0], parts, grads)
        dag, dwdw, dvec = _conv_bwd(sv["proj"], sv["uc"], dcat, wdw_pad[l], vec(conv_ln_g, l), vec(conv_ln_b, l))
        dproj = jnp.concatenate([dag, dq, dk, dv], axis=1)
        dh = _mm_nt(dproj, g_in, out_dtype=F32, name="d_h")
        dw_in = _mm_tn(sv["h"], dproj, N_CHIPS, name="dw_in")
        dx, dg_pre_mix = _rms_bwd(dh, sv["x"], vec(pre_mix_g, l), dx1, out_dtype=F32, name="rms_bwd_mix_pre")

        dws = [dw_in, dw_out.reshape(N_CHIPS, -1, D_MODEL), dw_up, dw_down.reshape(N_CHIPS, -1, D_MODEL)]
        others = _pair_exchange(dws)
        pending = (l, [_pair_sum(d, o, name=f"pair_sum_{n}") for d, o, n in zip(dws, others, big_names)])

        drpb = _rpb_grad(dbias)
        small[l] = jnp.concatenate([
            dvec[0], dvec[1], dvec[2], dg_pre_mix[0], dg_post_mix[0], dg_pre_mlp[0], dg_post_mlp[0],
            drpb.reshape(-1), dwdw[:CONV_K].reshape(-1)])

    grads = finish_layer(pending[0], _run_comm(_scatter_comm(pending[1]), name="grad_chip_scatter_last"), grads)

    small_sum = _all_devices(_to_rows(jnp.stack(small)), reduce=True, name="reduce_small_grads")
    small_sum = small_sum.reshape(-1)[:depth * small[0].shape[0]].reshape(depth, -1)
    sizes = [CONV_W, CONV_W, CONV_W, D_MODEL, D_MODEL, D_MODEL, D_MODEL, rpb[0].size, CONV_K * CONV_W]
    offs = np.concatenate([[0], np.cumsum(sizes)])
    pieces = [small_sum[:, offs[i]:offs[i + 1]] for i in range(len(sizes))]
    g_small = {
        "b_dw": pieces[0], "conv_ln_g": pieces[1], "conv_ln_b": pieces[2],
        "pre_mix_g": pieces[3], "post_mix_g": pieces[4], "pre_mlp_g": pieces[5], "post_mlp_g": pieces[6],
        "rpb": pieces[7].reshape(rpb.shape),
        "w_dw": lax.dynamic_slice_in_dim(pieces[8].reshape(depth, CONV_K, CONV_W), chip * w_dw.shape[2],
                                         w_dw.shape[2], axis=2),
    }

    given = dict(w_in=(w_in, m_w_in, v_w_in), w_dw=(w_dw, m_w_dw, v_w_dw), b_dw=(b_dw, m_b_dw, v_b_dw),
                 conv_ln_g=(conv_ln_g, m_conv_ln_g, v_conv_ln_g), conv_ln_b=(conv_ln_b, m_conv_ln_b, v_conv_ln_b),
                 rpb=(rpb, m_rpb, v_rpb), w_out=(w_out, m_w_out, v_w_out), w_up=(w_up, m_w_up, v_w_up),
                 w_down=(w_down, m_w_down, v_w_down), pre_mix_g=(pre_mix_g, m_pre_mix_g, v_pre_mix_g),
                 post_mix_g=(post_mix_g, m_post_mix_g, v_post_mix_g), pre_mlp_g=(pre_mlp_g, m_pre_mlp_g, v_pre_mlp_g),
                 post_mlp_g=(post_mlp_g, m_post_mlp_g, v_post_mlp_g))
    results = {}
    for n, g in zip(big_names, grads):
        w, m, v = given[n]
        flat = lambda a: a.reshape(-1, a.shape[-1])
        outs = _adamw(flat(w), flat(g), flat(m), flat(v), name=f"adamw_{n}")
        results[n] = [o.reshape(w.shape) for o in outs]

    small_names = list(g_small)
    pack = lambda arrs: _to_rows(jnp.concatenate([a.reshape(-1) for a in arrs]))
    outs = _adamw(pack([given[n][0] for n in small_names]), pack([g_small[n] for n in small_names]),
                  pack([given[n][1] for n in small_names]), pack([given[n][2] for n in small_names]),
                  name="adamw_small")
    pos = 0
    for n in small_names:
        w = given[n][0]
        results[n] = [o.reshape(-1)[pos:pos + w.size].reshape(w.shape) for o in outs]
        pos += w.size

    order = ["w_in", "w_dw", "b_dw", "conv_ln_g", "conv_ln_b", "rpb", "w_out", "w_up", "w_down",
             "pre_mix_g", "post_mix_g", "pre_mlp_g", "post_mlp_g"]
    return (loss, dx.reshape(x.shape), *[results[n][0] for n in order], *[results[n][1] for n in order],
            *[results[n][2] for n in order], *[results[n][3] for n in order])
```

```python
import functools

import numpy as np
import jax
import jax.numpy as jnp
from jax import lax
from jax.experimental import pallas as pl
from jax.experimental.pallas import tpu as pltpu

F32 = jnp.float32
BF16 = jnp.bfloat16

D_MODEL = 2048
CONV_W = 1024
NA_W = 1024
N_HEADS = 16
HEAD_D = 64
GRID_W = 64
WIN_ROWS = 8
WIN_COLS = 16
CONV_K = 31
D_FF = 4 * D_MODEL
IN_COLS = 2 * CONV_W + 3 * NA_W
RMS_EPS = 1e-6
LN_EPS = 1e-5
NEG_INF = -1e30
N_CHIPS = 4
N_DEV = 8
HALO = 16
BAND = WIN_ROWS * GRID_W

ADAM_LR = 0.001
ADAM_B1 = 0.9
ADAM_B2 = 0.999
ADAM_EPS = 1e-08
ADAM_WD = 0.01
ADAM_STEP = 10

VMEM_LIMIT = 48 * 1024 * 1024
MESH = pl.DeviceIdType.MESH
ANY = pl.BlockSpec(memory_space=pl.ANY)


def _params(*sem):
    return pltpu.CompilerParams(dimension_semantics=sem, vmem_limit_bytes=VMEM_LIMIT)


class _Comm:
    def __init__(self, ins, out_shape, sems, start, finish):
        self.ins, self.out_shape, self.sems, self.start, self.finish = ins, out_shape, sems, start, finish


def _call(body, *, name, grid, in_specs, out_specs, out_shape, scratch_shapes=(), sem, args, comm=None):
    in_specs, out_specs, out_shape = list(in_specs), list(out_specs), list(out_shape)
    scratch_shapes = list(scratch_shapes)
    if comm is None:
        return pl.pallas_call(body, name=name, grid=grid, in_specs=in_specs, out_specs=out_specs,
                              out_shape=out_shape, scratch_shapes=scratch_shapes,
                              compiler_params=_params(*sem))(*args)
    ni, no, ns = len(in_specs), len(out_specs), len(scratch_shapes)
    nci, nco = len(comm.ins), len(comm.out_shape)

    def full(*refs):
        ins, refs = refs[:ni], refs[ni:]
        cins, refs = refs[:nci], refs[nci:]
        outs, refs = refs[:no], refs[no:]
        couts, refs = refs[:nco], refs[nco:]
        scr, csems = refs[:ns], refs[ns:]
        ids = [pl.program_id(d) for d in range(len(grid))]
        first = functools.reduce(jnp.logical_and, [i == 0 for i in ids])
        last = functools.reduce(jnp.logical_and, [i == g - 1 for i, g in zip(ids, grid)])

        @pl.when(first)
        def _():
            comm.start(cins, couts, csems)

        body(*ins, *outs, *scr)

        @pl.when(last)
        def _():
            comm.finish(cins, couts, csems)

    return pl.pallas_call(
        full, name=name, grid=grid,
        in_specs=in_specs + [ANY] * nci, out_specs=out_specs + [ANY] * nco,
        out_shape=out_shape + list(comm.out_shape),
        scratch_shapes=scratch_shapes + list(comm.sems),
        compiler_params=_params(*(["arbitrary"] * len(grid))),
    )(*args, *comm.ins)


K_TILES = (2048, 1280, 1024, 512)


def _tile(n, pref):
    for t in pref:
        if n % t == 0:
            return t
    return n


def _accumulate(acc, step, n_steps, finish, part):
    if n_steps == 1:
        finish(part())
        return

    @pl.when(step == 0)
    def _():
        acc[...] = part()

    @pl.when(jnp.logical_and(step > 0, step < n_steps - 1))
    def _():
        acc[...] += part()

    @pl.when(step == n_steps - 1)
    def _():
        finish(acc[...] + part())


def _mm_nn(a, w, *, out_dtype, relu2=False, name, comm=None):
    m, k = a.shape
    s, _, n = w.shape
    tm = _tile(m, (1024, 512, 256))
    tn = _tile(n, (1024, 1280, 512))
    tk = _tile(k, K_TILES)
    nps = n // tn
    nk = k // tk

    def body(a_ref, w_ref, *rest):
        outs, acc = rest[:-1], rest[-1]

        def finish(r):
            outs[0][...] = r.astype(outs[0].dtype)
            if relu2:
                p = jnp.maximum(r, 0.0)
                outs[1][...] = (p * p).astype(outs[1].dtype)

        _accumulate(acc, pl.program_id(2), nk, finish,
                    lambda: jnp.dot(a_ref[...], w_ref[...], preferred_element_type=F32))

    o_spec = pl.BlockSpec((tm, tn), lambda i, j, kk: (i, j))
    o_shape = jax.ShapeDtypeStruct((m, s * n), out_dtype)
    return _call(
        body, name=name,
        grid=(m // tm, s * nps, nk),
        in_specs=[pl.BlockSpec((tm, tk), lambda i, j, kk: (i, kk)),
                  pl.BlockSpec((None, tk, tn), lambda i, j, kk: (j // nps, kk, j % nps))],
        out_specs=[o_spec, o_spec] if relu2 else [o_spec],
        out_shape=[o_shape, o_shape] if relu2 else [o_shape],
        scratch_shapes=[pltpu.VMEM((tm, tn), F32)],
        sem=("parallel", "parallel", "arbitrary"), args=(a, w), comm=comm)


def _mm_nt(dy, w, *, out_dtype, up=None, name, comm=None):
    m = dy.shape[0]
    s, k, n = w.shape
    tm = _tile(m, (1024, 512, 256))
    tko = _tile(k, (1024, 512))
    tn = _tile(n, K_TILES)
    nps = n // tn
    nn = s * nps

    def body(dy_ref, w_ref, *rest):
        if up is None:
            o_ref, acc = rest
        else:
            up_ref, o_ref, acc = rest

        def finish(r):
            if up is not None:
                r = r * (2.0 * jnp.maximum(up_ref[...].astype(F32), 0.0))
            o_ref[...] = r.astype(o_ref.dtype)

        _accumulate(acc, pl.program_id(2), nn, finish,
                    lambda: lax.dot_general(dy_ref[...], w_ref[...], (((1,), (1,)), ((), ())),
                                            preferred_element_type=F32))

    in_specs = [pl.BlockSpec((tm, tn), lambda i, j, kk: (i, kk)),
                pl.BlockSpec((None, tko, tn), lambda i, j, kk: (kk // nps, j, kk % nps))]
    args = [dy, w]
    if up is not None:
        in_specs.append(pl.BlockSpec((tm, tko), lambda i, j, kk: (i, j)))
        args.append(up)
    return _call(
        body, name=name,
        grid=(m // tm, k // tko, nn),
        in_specs=in_specs,
        out_specs=[pl.BlockSpec((tm, tko), lambda i, j, kk: (i, j))],
        out_shape=[jax.ShapeDtypeStruct((m, k), out_dtype)],
        scratch_shapes=[pltpu.VMEM((tm, tko), F32)],
        sem=("parallel", "parallel", "arbitrary"), args=args, comm=comm)


def _mm_tn(a, dy, s, *, name):
    m, k = a.shape
    n = dy.shape[1] // s
    tk = _tile(k, (1024, 512))
    tn = _tile(n, (1024, 1280, 512))
    tm = _tile(m, K_TILES)
    nps = n // tn
    nm = m // tm

    def body(a_ref, dy_ref, o_ref, acc):
        def finish(r):
            o_ref[...] = r.astype(o_ref.dtype)

        _accumulate(acc, pl.program_id(2), nm, finish,
                    lambda: lax.dot_general(a_ref[...], dy_ref[...], (((0,), (0,)), ((), ())),
                                            preferred_element_type=F32))

    return pl.pallas_call(
        body, name=name,
        grid=(k // tk, s * nps, nm),
        in_specs=[pl.BlockSpec((tm, tk), lambda i, j, mm: (mm, i)),
                  pl.BlockSpec((tm, tn), lambda i, j, mm: (mm, j))],
        out_specs=pl.BlockSpec((None, tk, tn), lambda i, j, mm: (j // nps, i, j % nps)),
        out_shape=jax.ShapeDtypeStruct((s, k, n), BF16),
        scratch_shapes=[pltpu.VMEM((tk, tn), F32)],
        compiler_params=_params("parallel", "parallel", "arbitrary"),
    )(a, dy)


ROWS = 256


def _row_spec(d):
    return pl.BlockSpec((ROWS, d), lambda i: (i, 0))


def _vec_spec(d):
    return pl.BlockSpec((1, d), lambda i: (0, 0))


def _rstd(v):
    return lax.rsqrt(jnp.mean(v * v, axis=-1, keepdims=True) + RMS_EPS)


def _rms_fwd(x, g, *, name):
    t, d = x.shape

    def body(x_ref, g_ref, h_ref):
        xv = x_ref[...]
        h_ref[...] = ((xv * _rstd(xv)) * g_ref[...]).astype(BF16)

    return pl.pallas_call(
        body, name=name, grid=(t // ROWS,),
        in_specs=[_row_spec(d), _vec_spec(d)],
        out_specs=_row_spec(d),
        out_shape=jax.ShapeDtypeStruct((t, d), BF16),
        compiler_params=_params("parallel"),
    )(x, g)


def _resid_rms(res, y, g_post, g_next, *, name):
    t, d = res.shape
    with_next = g_next is not None

    def body(res_ref, y_ref, gp_ref, *rest):
        yv = y_ref[...]
        xn = res_ref[...] + (yv * _rstd(yv)) * gp_ref[...]
        if with_next:
            gn_ref, xo_ref, h_ref = rest
            h_ref[...] = ((xn * _rstd(xn)) * gn_ref[...]).astype(BF16)
        else:
            (xo_ref,) = rest
        xo_ref[...] = xn

    in_specs = [_row_spec(d), _row_spec(d), _vec_spec(d)]
    args = [res, y, g_post]
    out_specs = [_row_spec(d)]
    out_shape = [jax.ShapeDtypeStruct((t, d), F32)]
    if with_next:
        in_specs.append(_vec_spec(d))
        args.append(g_next)
        out_specs.append(_row_spec(d))
        out_shape.append(jax.ShapeDtypeStruct((t, d), BF16))
    return pl.pallas_call(
        body, name=name, grid=(t // ROWS,),
        in_specs=in_specs, out_specs=tuple(out_specs), out_shape=tuple(out_shape),
        compiler_params=_params("parallel"),
    )(*args)


def _rms_bwd(dy, xin, g, res, *, out_dtype, name):
    t, d = xin.shape
    with_res = res is not None

    def body(dy_ref, x_ref, g_ref, *rest):
        if with_res:
            res_ref, dx_ref, dg_ref = rest
        else:
            dx_ref, dg_ref = rest
        xv = x_ref[...]
        r = _rstd(xv)
        nrm = xv * r
        dyv = dy_ref[...]
        dn = dyv * g_ref[...]
        dx = r * (dn - nrm * jnp.mean(dn * nrm, axis=-1, keepdims=True))
        if with_res:
            dx = dx + res_ref[...]
        dx_ref[...] = dx.astype(dx_ref.dtype)

        @pl.when(pl.program_id(0) == 0)
        def _():
            dg_ref[...] = jnp.zeros_like(dg_ref)

        dg_ref[...] += jnp.sum(dyv * nrm, axis=0, keepdims=True)

    in_specs = [_row_spec(d), _row_spec(d), _vec_spec(d)]
    args = [dy, xin, g]
    if with_res:
        in_specs.append(_row_spec(d))
        args.append(res)
    return pl.pallas_call(
        body, name=name, grid=(t // ROWS,),
        in_specs=in_specs,
        out_specs=(_row_spec(d), _vec_spec(d)),
        out_shape=(jax.ShapeDtypeStruct((t, d), out_dtype), jax.ShapeDtypeStruct((1, d), F32)),
        compiler_params=_params("arbitrary"),
    )(*args)


def _loss_head(y, target):
    t, d = y.shape

    def body(y_ref, t_ref, dy_ref, loss_ref):
        diff = y_ref[...] - t_ref[...]
        dy_ref[...] = diff * (1.0 / d)

        @pl.when(pl.program_id(0) == 0)
        def _():
            loss_ref[...] = jnp.zeros_like(loss_ref)

        loss_ref[...] += jnp.sum(diff * diff) * (0.5 / d)

    return pl.pallas_call(
        body, name="loss_head", grid=(t // ROWS,),
        in_specs=[_row_spec(d), _row_spec(d)],
        out_specs=(_row_spec(d), pl.BlockSpec((8, 128), lambda i: (0, 0))),
        out_shape=(jax.ShapeDtypeStruct((t, d), F32), jax.ShapeDtypeStruct((8, 128), F32)),
        compiler_params=_params("arbitrary"),
    )(y, target)


def _halo_specs(t, col, width):
    rb = ROWS // HALO
    last = t // HALO - 1
    return [pl.BlockSpec((ROWS, width), lambda i: (i, col)),
            pl.BlockSpec((HALO, width), lambda i: (jnp.maximum(i * rb - 1, 0), col)),
            pl.BlockSpec((HALO, width), lambda i: (jnp.minimum((i + 1) * rb, last), col))]


def _glu(a_ref, g_ref):
    return a_ref[...].astype(F32) * jax.nn.sigmoid(g_ref[...].astype(F32))


def _fill_ext(ext, cur, prev, nxt, i, nblk):
    ext[pl.ds(HALO, ROWS), :] = cur
    ext[pl.ds(0, HALO), :] = jnp.where(i > 0, prev, 0.0)
    ext[pl.ds(HALO + ROWS, HALO), :] = jnp.where(i < nblk - 1, nxt, 0.0)


def _conv_fwd(proj, w_dw, b_dw, ln_g, ln_b):
    t = proj.shape[0]
    nblk = t // ROWS
    pad = HALO - CONV_K // 2

    def body(ac, ap, an, gc, gp, gn, w_ref, b_ref, lg_ref, lb_ref, yc_ref, uc_ref, uext):
        i = pl.program_id(0)
        _fill_ext(uext, _glu(ac, gc), _glu(ap, gp), _glu(an, gn), i, nblk)
        acc = jnp.broadcast_to(b_ref[...], (ROWS, CONV_W))
        for j in range(CONV_K):
            acc = acc + uext[pl.ds(j + pad, ROWS), :] * w_ref[pl.ds(j, 1), :]
        uc_ref[...] = acc
        mu = jnp.mean(acc, axis=-1, keepdims=True)
        xc = acc - mu
        var = jnp.mean(xc * xc, axis=-1, keepdims=True)
        yln = xc * lax.rsqrt(var + LN_EPS) * lg_ref[...] + lb_ref[...]
        yc_ref[...] = (yln * jax.nn.sigmoid(yln)).astype(BF16)

    vec = pl.BlockSpec((1, CONV_W), lambda i: (0, 0))
    return pl.pallas_call(
        body, name="conv_fwd", grid=(nblk,),
        in_specs=_halo_specs(t, 0, CONV_W) + _halo_specs(t, 1, CONV_W)
        + [pl.BlockSpec((32, CONV_W), lambda i: (0, 0)), vec, vec, vec],
        out_specs=(pl.BlockSpec((ROWS, CONV_W), lambda i: (i, 0)),
                   pl.BlockSpec((ROWS, CONV_W), lambda i: (i, 0))),
        out_shape=(jax.ShapeDtypeStruct((t, CONV_W), BF16), jax.ShapeDtypeStruct((t, CONV_W), F32)),
        scratch_shapes=[pltpu.VMEM((ROWS + 2 * HALO, CONV_W), F32)],
        compiler_params=_params("parallel"),
    )(proj, proj, proj, proj, proj, proj, w_dw, b_dw, ln_g, ln_b)


def _conv_bwd(proj, uc, dcat, w_dw, ln_g, ln_b):
    t = proj.shape[0]
    nblk = t // ROWS
    pad = HALO - CONV_K // 2

    def body(ac, ap, an, gc, gp, gn, uc_c, uc_p, uc_n, dy_c, dy_p, dy_n, w_ref, lg_ref, lb_ref,
             dag_ref, dw_ref, dvec_ref, uext, dext):
        i = pl.program_id(0)
        _fill_ext(uext, _glu(ac, gc), _glu(ap, gp), _glu(an, gn), i, nblk)

        def ln_bwd(u_ref, d_ref):
            u = u_ref[...]
            mu = jnp.mean(u, axis=-1, keepdims=True)
            xc = u - mu
            rstd = lax.rsqrt(jnp.mean(xc * xc, axis=-1, keepdims=True) + LN_EPS)
            xhat = xc * rstd
            yln = xhat * lg_ref[...] + lb_ref[...]
            sg = jax.nn.sigmoid(yln)
            dyln = d_ref[...] * (sg * (1.0 + yln * (1.0 - sg)))
            dxh = dyln * lg_ref[...]
            du = rstd * (dxh - jnp.mean(dxh, axis=-1, keepdims=True)
                         - xhat * jnp.mean(dxh * xhat, axis=-1, keepdims=True))
            return du, dyln, xhat

        duc, dyln, xhat = ln_bwd(uc_c, dy_c)
        _fill_ext(dext, duc, ln_bwd(uc_p, dy_p)[0], ln_bwd(uc_n, dy_n)[0], i, nblk)

        @pl.when(i == 0)
        def _():
            dw_ref[...] = jnp.zeros_like(dw_ref)
            dvec_ref[...] = jnp.zeros_like(dvec_ref)

        dvec_ref[pl.ds(0, 1), :] += jnp.sum(duc, axis=0, keepdims=True)
        dvec_ref[pl.ds(1, 1), :] += jnp.sum(dyln * xhat, axis=0, keepdims=True)
        dvec_ref[pl.ds(2, 1), :] += jnp.sum(dyln, axis=0, keepdims=True)

        du = jnp.zeros((ROWS, CONV_W), F32)
        for j in range(CONV_K):
            du = du + dext[pl.ds(2 * HALO - pad - j, ROWS), :] * w_ref[pl.ds(j, 1), :]
            dw_ref[pl.ds(j, 1), :] += jnp.sum(duc * uext[pl.ds(j + pad, ROWS), :], axis=0, keepdims=True)

        a = ac[...].astype(F32)
        sg = jax.nn.sigmoid(gc[...].astype(F32))
        dag_ref[:, pl.ds(0, CONV_W)] = (du * sg).astype(BF16)
        dag_ref[:, pl.ds(CONV_W, CONV_W)] = (du * a * sg * (1.0 - sg)).astype(BF16)

    vec = pl.BlockSpec((1, CONV_W), lambda i: (0, 0))
    return pl.pallas_call(
        body, name="conv_bwd", grid=(nblk,),
        in_specs=_halo_specs(t, 0, CONV_W) + _halo_specs(t, 1, CONV_W) + _halo_specs(t, 0, CONV_W)
        + _halo_specs(t, 0, CONV_W) + [pl.BlockSpec((32, CONV_W), lambda i: (0, 0)), vec, vec],
        out_specs=(pl.BlockSpec((ROWS, 2 * CONV_W), lambda i: (i, 0)),
                   pl.BlockSpec((32, CONV_W), lambda i: (0, 0)),
                   pl.BlockSpec((8, CONV_W), lambda i: (0, 0))),
        out_shape=(jax.ShapeDtypeStruct((t, 2 * CONV_W), BF16),
                   jax.ShapeDtypeStruct((32, CONV_W), F32),
                   jax.ShapeDtypeStruct((8, CONV_W), F32)),
        scratch_shapes=[pltpu.VMEM((ROWS + 2 * HALO, CONV_W), F32),
                        pltpu.VMEM((ROWS + 2 * HALO, CONV_W), F32)],
        compiler_params=_params("arbitrary"),
    )(proj, proj, proj, proj, proj, proj, uc, uc, uc, dcat, dcat, dcat, w_dw, ln_g, ln_b)


Q_BLK, K_BLK, V_BLK = 16, 24, 32


def _head_masks():
    lane = lax.broadcasted_iota(jnp.int32, (1, 2 * HEAD_D), 1)
    return [lane < HEAD_D, lane >= HEAD_D]


def _band(r, rows):
    rs = jnp.clip(r - WIN_ROWS // 2, 0, rows - WIN_ROWS)
    return rs, r - rs, pl.multiple_of(r * GRID_W, GRID_W), pl.multiple_of(rs * GRID_W, GRID_W)


ROWS_PER_STEP = 2
NT_DIMS = (((1,), (1,)), ((), ()))
TN_DIMS = (((0,), (0,)), ((), ()))
PAIR_SPEC = pl.BlockSpec((None, WIN_ROWS, 2 * GRID_W, BAND), lambda h: (h, 0, 0, 0))


def _stack_heads(block, masks):
    z = jnp.zeros_like(block)
    return jnp.concatenate([jnp.where(masks[0], block, z), jnp.where(masks[1], block, z)], axis=0)


def _unstack_heads(both, masks):
    return jnp.where(masks[0], both[:GRID_W], both[GRID_W:])


def _softmax(s, bias):
    s = s * (HEAD_D ** -0.5) + bias
    e = jnp.exp(s - jnp.max(s, axis=-1, keepdims=True))
    return e * (1.0 / jnp.sum(e, axis=-1, keepdims=True))


def _natten_fwd(proj, bias, comm=None):
    t = proj.shape[0]
    rows = t // GRID_W

    def body(q_ref, k_ref, v_ref, b_ref, o_ref):
        masks = _head_masks()

        def step(i, carry):
            bands = [_band(i * ROWS_PER_STEP + u, rows) for u in range(ROWS_PER_STEP)]
            vbs = [v_ref[pl.ds(k0, BAND), :] for _, _, _, k0 in bands]
            ss = [lax.dot_general(_stack_heads(q_ref[pl.ds(q0, GRID_W), :], masks), k_ref[pl.ds(k0, BAND), :],
                                  NT_DIMS, preferred_element_type=F32) for _, _, q0, k0 in bands]
            ps = [_softmax(s, b_ref[off]) for s, (_, off, _, _) in zip(ss, bands)]
            os = [jnp.dot(p.astype(BF16), vb, preferred_element_type=F32) for p, vb in zip(ps, vbs)]
            for o, (_, _, q0, _) in zip(os, bands):
                o_ref[pl.ds(q0, GRID_W), :] = _unstack_heads(o, masks).astype(BF16)
            return carry

        lax.fori_loop(0, rows // ROWS_PER_STEP, step, 0)

    col = lambda b: pl.BlockSpec((t, 2 * HEAD_D), lambda h: (0, b + h))
    return _call(
        body, name="natten_fwd", grid=(N_HEADS // 2,),
        in_specs=[col(Q_BLK), col(K_BLK), col(V_BLK), PAIR_SPEC],
        out_specs=[pl.BlockSpec((t, 2 * HEAD_D), lambda h: (0, h))],
        out_shape=[jax.ShapeDtypeStruct((t, NA_W), BF16)],
        sem=("parallel",), args=(proj, proj, proj, bias), comm=comm)


def _natten_bwd(proj, bias, dcat, comm=None):
    t = proj.shape[0]
    rows = t // GRID_W
    scale = HEAD_D ** -0.5

    def body(q_ref, k_ref, v_ref, b_ref, do_ref, dq_ref, dk_ref, dv_ref, db_ref, dk_acc, dv_acc):
        masks = _head_masks()
        dk_acc[...] = jnp.zeros_like(dk_acc)
        dv_acc[...] = jnp.zeros_like(dv_acc)
        db_ref[...] = jnp.zeros_like(db_ref)

        def step(i, carry):
            bands = [_band(i * ROWS_PER_STEP + u, rows) for u in range(ROWS_PER_STEP)]
            kbs = [k_ref[pl.ds(k0, BAND), :] for _, _, _, k0 in bands]
            q2s = [_stack_heads(q_ref[pl.ds(q0, GRID_W), :], masks) for _, _, q0, _ in bands]
            do2s = [_stack_heads(do_ref[pl.ds(q0, GRID_W), :].astype(BF16), masks) for _, _, q0, _ in bands]
            ss = [lax.dot_general(q2, kb, NT_DIMS, preferred_element_type=F32) for q2, kb in zip(q2s, kbs)]
            dps = [lax.dot_general(do2, v_ref[pl.ds(k0, BAND), :], NT_DIMS, preferred_element_type=F32)
                   for do2, (_, _, _, k0) in zip(do2s, bands)]
            ps = [_softmax(s, b_ref[off]) for s, (_, off, _, _) in zip(ss, bands)]
            dss = [p * (dp - jnp.sum(p * dp, axis=-1, keepdims=True)) for p, dp in zip(ps, dps)]
            dsbs = [ds.astype(BF16) for ds in dss]
            dqs = [jnp.dot(dsb, kb, preferred_element_type=F32) for dsb, kb in zip(dsbs, kbs)]
            dks = [lax.dot_general(dsb, q2, TN_DIMS, preferred_element_type=F32) for dsb, q2 in zip(dsbs, q2s)]
            dvs = [lax.dot_general(p.astype(BF16), do2, TN_DIMS, preferred_element_type=F32)
                   for p, do2 in zip(ps, do2s)]
            for ds, dq, dk, dv, (_, off, q0, k0) in zip(dss, dqs, dks, dvs, bands):
                db_ref[off] += ds
                dq_ref[pl.ds(q0, GRID_W), :] = (_unstack_heads(dq, masks) * scale).astype(BF16)
                dk_acc[pl.ds(k0, BAND), :] += dk * scale
                dv_acc[pl.ds(k0, BAND), :] += dv
            return carry

        lax.fori_loop(0, rows // ROWS_PER_STEP, step, 0)
        dk_ref[...] = dk_acc[...].astype(BF16)
        dv_ref[...] = dv_acc[...].astype(BF16)

    col = lambda b: pl.BlockSpec((t, 2 * HEAD_D), lambda h: (0, b + h))
    o_spec = pl.BlockSpec((t, 2 * HEAD_D), lambda h: (0, h))
    o_shape = jax.ShapeDtypeStruct((t, NA_W), BF16)
    return _call(
        body, name="natten_bwd", grid=(N_HEADS // 2,),
        in_specs=[col(Q_BLK), col(K_BLK), col(V_BLK), PAIR_SPEC, col(CONV_W // (2 * HEAD_D))],
        out_specs=[o_spec, o_spec, o_spec, PAIR_SPEC],
        out_shape=[o_shape, o_shape, o_shape,
                   jax.ShapeDtypeStruct((N_HEADS // 2, WIN_ROWS, 2 * GRID_W, BAND), F32)],
        scratch_shapes=[pltpu.VMEM((t, 2 * HEAD_D), F32), pltpu.VMEM((t, 2 * HEAD_D), F32)],
        sem=("parallel",), args=(proj, proj, proj, bias, dcat), comm=comm)


def _bias_index_tables():
    off = np.arange(WIN_ROWS)
    kr = np.arange(WIN_ROWS)
    c = np.arange(GRID_W)
    dr = kr[None, :] - off[:, None] + (WIN_ROWS - 1)
    dc = np.clip(c[None, :] - c[:, None], -(WIN_COLS - 1), WIN_COLS - 1) + (WIN_COLS - 1)
    cs = np.clip(c - WIN_COLS // 2, 0, GRID_W - WIN_COLS)
    mask = (c[None, :] >= cs[:, None]) & (c[None, :] < cs[:, None] + WIN_COLS)
    return dr, dc, mask


def _bias_table(rpb_l):
    _, dc, mask = _bias_index_tables()
    n_dc = 32
    et = np.zeros((n_dc, GRID_W * GRID_W), np.float32)
    et[dc.reshape(-1), np.arange(GRID_W * GRID_W)] = 1.0
    rp = jnp.pad(rpb_l, ((0, 0), (0, 1), (0, 1))).reshape(N_HEADS * 16, n_dc)

    def body(r_ref, e_ref, o_ref):
        o_ref[...] = jnp.dot(r_ref[...], e_ref[...], preferred_element_type=F32, precision=lax.Precision.HIGHEST)

    vm = pl.BlockSpec(memory_space=pltpu.VMEM)
    ge = pl.pallas_call(
        body, name="rpb_expand", in_specs=[vm, vm], out_specs=vm,
        out_shape=jax.ShapeDtypeStruct((N_HEADS * 16, GRID_W * GRID_W), F32),
        compiler_params=pltpu.CompilerParams(vmem_limit_bytes=VMEM_LIMIT),
    )(rp, jnp.asarray(et))
    ge = ge.reshape(N_HEADS, 16, GRID_W, GRID_W)
    last = WIN_ROWS - 1
    b = jnp.stack([ge[:, last - off:last - off + WIN_ROWS] for off in range(WIN_ROWS)], axis=1)
    b = jnp.where(mask[None, None, None, :, :], b, NEG_INF)
    b = b.reshape(N_HEADS // 2, 2, WIN_ROWS, WIN_ROWS, GRID_W, GRID_W).transpose(0, 2, 1, 4, 3, 5)
    return b.reshape(N_HEADS // 2, WIN_ROWS, 2 * GRID_W, BAND)


def _rpb_grad(dbias):
    dr, dc, _ = _bias_index_tables()
    e = np.zeros((GRID_W * GRID_W, 128), np.float32)
    e[np.arange(GRID_W * GRID_W), dc.reshape(-1)] = 1.0
    f = np.zeros((16, WIN_ROWS * WIN_ROWS), np.float32)
    f[dr.reshape(-1), np.arange(WIN_ROWS * WIN_ROWS)] = 1.0
    z = dbias.reshape(N_HEADS // 2, WIN_ROWS, 2, GRID_W, WIN_ROWS, GRID_W).transpose(0, 2, 1, 4, 3, 5)
    z = z.reshape(N_HEADS, WIN_ROWS * WIN_ROWS, GRID_W * GRID_W)

    def body(z_ref, e_ref, f_ref, o_ref):
        hi = lax.Precision.HIGHEST
        zd = jnp.dot(z_ref[...], e_ref[...], preferred_element_type=F32, precision=hi)
        o_ref[...] = jnp.dot(f_ref[...], zd, preferred_element_type=F32, precision=hi)

    out = pl.pallas_call(
        body, name="rpb_grad", grid=(N_HEADS,),
        in_specs=[pl.BlockSpec((None, WIN_ROWS * WIN_ROWS, GRID_W * GRID_W), lambda h: (h, 0, 0)),
                  pl.BlockSpec((GRID_W * GRID_W, 128), lambda h: (0, 0)),
                  pl.BlockSpec((16, WIN_ROWS * WIN_ROWS), lambda h: (0, 0))],
        out_specs=pl.BlockSpec((None, 16, 128), lambda h: (h, 0, 0)),
        out_shape=jax.ShapeDtypeStruct((N_HEADS, 16, 128), F32),
        compiler_params=_params("parallel"),
    )(z, jnp.asarray(e), jnp.asarray(f))
    return out[:, :2 * WIN_ROWS - 1, :2 * WIN_COLS - 1]


def _cast_bf16(w2d, *, name):
    r, c = w2d.shape
    tr = _tile(r, (512, 256))

    def body(w_ref, o_ref):
        o_ref[...] = w_ref[...].astype(BF16)

    return pl.pallas_call(
        body, name=name, grid=(r // tr,),
        in_specs=[pl.BlockSpec((tr, c), lambda i: (i, 0))],
        out_specs=pl.BlockSpec((tr, c), lambda i: (i, 0)),
        out_shape=jax.ShapeDtypeStruct((r, c), BF16),
        compiler_params=_params("parallel"),
    )(w2d)


def _core_index():
    return lax.axis_index("c").astype(jnp.int32).reshape(1)


def _pair_sum(dw, other, *, name):
    s, r, c = dw.shape
    h = r // 2
    tr = _tile(h, (256, 128))
    nb = h // tr

    def body(c_ref, a_ref, b_ref, o_ref):
        del c_ref
        o_ref[...] = (a_ref[...].astype(F32) + b_ref[...].astype(F32)).astype(BF16)

    return pl.pallas_call(
        body, name=name,
        grid_spec=pltpu.PrefetchScalarGridSpec(
            num_scalar_prefetch=1, grid=(s, nb),
            in_specs=[pl.BlockSpec((None, tr, c), lambda si, i, cr: (si, cr[0] * nb + i, 0)),
                      pl.BlockSpec((None, tr, c), lambda si, i, cr: (si, i, 0))],
            out_specs=pl.BlockSpec((None, tr, c), lambda si, i, cr: (si, i, 0))),
        out_shape=jax.ShapeDtypeStruct((s, h, c), BF16),
        compiler_params=_params("parallel", "parallel"),
    )(_core_index(), dw, other)


def _chip_sum(parts, grad, layer, *, name):
    s, h, c = parts.shape
    tr = _tile(h, (256, 128))
    nb = h // tr

    def body(c_ref, p_ref, g_in, o_ref):
        del c_ref, g_in
        acc = p_ref[0].astype(F32) + p_ref[1].astype(F32)
        acc = acc + p_ref[2].astype(F32)
        o_ref[...] = acc + p_ref[3].astype(F32)

    return pl.pallas_call(
        body, name=name,
        grid_spec=pltpu.PrefetchScalarGridSpec(
            num_scalar_prefetch=1, grid=(nb,),
            in_specs=[pl.BlockSpec((s, tr, c), lambda i, cr: (0, i, 0)), ANY],
            out_specs=pl.BlockSpec((None, tr, c), lambda i, cr: (layer, cr[0] * nb + i, 0))),
        out_shape=jax.ShapeDtypeStruct(grad.shape, F32),
        input_output_aliases={2: 0},
        compiler_params=_params("parallel"),
    )(_core_index(), parts, grad)


def _adamw(w, g, m, v, *, name):
    r, c = w.shape
    tr = _tile(r, (256, 128, 8))
    tr = tr if r % tr == 0 else r
    bc1 = 1.0 - ADAM_B1 ** ADAM_STEP
    bc2 = 1.0 - ADAM_B2 ** ADAM_STEP

    def body(w_ref, g_ref, m_ref, v_ref, go_ref, d_ref, mo_ref, vo_ref):
        gv = g_ref[...]
        mn = ADAM_B1 * m_ref[...] + (1.0 - ADAM_B1) * gv
        vn = ADAM_B2 * v_ref[...] + (1.0 - ADAM_B2) * (gv * gv)
        go_ref[...] = gv
        mo_ref[...] = mn
        vo_ref[...] = vn
        d_ref[...] = -ADAM_LR * ((mn / bc1) / (jnp.sqrt(vn / bc2) + ADAM_EPS) + ADAM_WD * w_ref[...])

    spec = pl.BlockSpec((tr, c), lambda i: (i, 0))
    shape = jax.ShapeDtypeStruct((r, c), F32)
    return pl.pallas_call(
        body, name=name, grid=(r // tr,),
        in_specs=[spec] * 4, out_specs=(spec,) * 4, out_shape=(shape,) * 4,
        compiler_params=_params("parallel"),
    )(w, g, m, v)


def _me():
    return lax.axis_index("x"), lax.axis_index("y"), lax.axis_index("c")


def _other_chips(x, y):
    return [(1 - x, y), (x, 1 - y), (1 - x, 1 - y)]


def _remote(src, dst, send, recv, k, to):
    return pltpu.make_async_remote_copy(src_ref=src, dst_ref=dst, send_sem=send.at[k], recv_sem=recv.at[k],
                                        device_id=to, device_id_type=MESH)


def _run_comm(comm, *, name):
    def body(*refs):
        ni, no = len(comm.ins), len(comm.out_shape)
        ins, outs, sems = refs[:ni], refs[ni:ni + no], refs[ni + no:]
        comm.start(ins, outs, sems)
        comm.finish(ins, outs, sems)

    return pl.pallas_call(
        body, name=name,
        in_specs=[ANY] * len(comm.ins), out_specs=[ANY] * len(comm.out_shape),
        out_shape=list(comm.out_shape), scratch_shapes=list(comm.sems),
    )(*comm.ins)


def _gather_comm(layer, shards):
    nw = len(shards)

    def half(ref, chip, hc):
        hr = ref.shape[1] // 2
        return ref.at[chip, pl.ds(hc * hr, hr), :]

    def sent(ins, outs, sems):
        send, recv, local = sems
        x, y, c = _me()
        me_chip = 2 * x + y
        mine = [pltpu.make_async_copy(ins[w].at[layer], outs[w].at[me_chip], local.at[w]) for w in range(nw)]
        first = []
        for w in range(nw):
            hr = ins[w].shape[1] // 2
            src = ins[w].at[layer, pl.ds(c * hr, hr), :]
            for j, chip in enumerate(_other_chips(x, y)):
                first.append(_remote(src, half(outs[w], me_chip, c), send, recv, 6 * w + j, (*chip, c)))
        return mine, first

    def start(ins, outs, sems):
        mine, first = sent(ins, outs, sems)
        for cp in mine + first:
            cp.start()

    def finish(ins, outs, sems):
        send, recv, _ = sems
        mine, first = sent(ins, outs, sems)
        x, y, c = _me()
        sib = (x, y, 1 - c)
        chips = _other_chips(x, y)
        passed = []
        for w in range(nw):
            for j, chip in enumerate(chips):
                landed = half(outs[w], 2 * chip[0] + chip[1], c)
                _remote(landed, landed, send, recv, 6 * w + j, sib).wait_recv()
                cp = _remote(landed, landed, send, recv, 6 * w + 3 + j, sib)
                cp.start()
                passed.append(cp)
        for w in range(nw):
            for j, chip in enumerate(chips):
                theirs = half(outs[w], 2 * chip[0] + chip[1], 1 - c)
                _remote(theirs, theirs, send, recv, 6 * w + 3 + j, sib).wait_recv()
        for cp in first + passed:
            cp.wait_send()
        for cp in mine:
            cp.wait()

    return _Comm(list(shards), [jax.ShapeDtypeStruct((N_CHIPS,) + s.shape[1:], BF16) for s in shards],
                 [pltpu.SemaphoreType.DMA((6 * nw,)), pltpu.SemaphoreType.DMA((6 * nw,)),
                  pltpu.SemaphoreType.DMA((nw,))], start, finish)


def _pair_exchange(dws):
    nw = len(dws)

    def body(*refs):
        ins, outs = refs[:nw], refs[nw:2 * nw]
        send, recv = refs[2 * nw:]
        x, y, c = _me()
        cps = []
        for w in range(nw):
            hr = ins[w].shape[1] // 2
            src = ins[w].at[:, pl.ds((1 - c) * hr, hr), :]
            cps.append(_remote(src, outs[w], send, recv, w, (x, y, 1 - c)))
        for cp in cps:
            cp.start()
        for cp in cps:
            cp.wait()

    return pl.pallas_call(
        body, name="grad_pair_exchange",
        in_specs=[ANY] * nw, out_specs=[ANY] * nw,
        out_shape=[jax.ShapeDtypeStruct((d.shape[0], d.shape[1] // 2, d.shape[2]), BF16) for d in dws],
        scratch_shapes=[pltpu.SemaphoreType.DMA((nw,)), pltpu.SemaphoreType.DMA((nw,))],
    )(*dws)


def _scatter_comm(qs):
    nw = len(qs)

    def sent(ins, outs, sems):
        send, recv, local = sems
        x, y, c = _me()
        me_chip = 2 * x + y
        mine = [pltpu.make_async_copy(ins[w].at[me_chip], outs[w].at[me_chip], local.at[w]) for w in range(nw)]
        cps = []
        for w in range(nw):
            for j, chip in enumerate(_other_chips(x, y)):
                cps.append(_remote(ins[w].at[2 * chip[0] + chip[1]], outs[w].at[me_chip], send, recv,
                                   3 * w + j, (*chip, c)))
        return mine, cps

    def start(ins, outs, sems):
        mine, cps = sent(ins, outs, sems)
        for cp in mine + cps:
            cp.start()

    def finish(ins, outs, sems):
        send, recv, _ = sems
        mine, cps = sent(ins, outs, sems)
        x, y, c = _me()
        for w in range(nw):
            for j, chip in enumerate(_other_chips(x, y)):
                theirs = outs[w].at[2 * chip[0] + chip[1]]
                _remote(theirs, theirs, send, recv, 3 * w + j, (*chip, c)).wait_recv()
        for cp in cps:
            cp.wait_send()
        for cp in mine:
            cp.wait()

    return _Comm(list(qs), [jax.ShapeDtypeStruct(q.shape, BF16) for q in qs],
                 [pltpu.SemaphoreType.DMA((3 * nw,)), pltpu.SemaphoreType.DMA((3 * nw,)),
                  pltpu.SemaphoreType.DMA((nw,))], start, finish)


def _sibling_fill(layer, grads):
    nw = len(grads)

    def body(*refs):
        ins, outs = refs[:nw], refs[nw:2 * nw]
        send, recv = refs[2 * nw:]
        del ins
        x, y, c = _me()
        cps = []
        for w in range(nw):
            hr = outs[w].shape[1] // 2
            mine = outs[w].at[layer, pl.ds(c * hr, hr), :]
            cps.append(_remote(mine, mine, send, recv, w, (x, y, 1 - c)))
        for cp in cps:
            cp.start()
        for w in range(nw):
            hr = outs[w].shape[1] // 2
            theirs = outs[w].at[layer, pl.ds((1 - c) * hr, hr), :]
            _remote(theirs, theirs, send, recv, w, (x, y, 1 - c)).wait_recv()
        for cp in cps:
            cp.wait_send()

    return pl.pallas_call(
        body, name=f"grad_sibling_fill_{layer}",
        in_specs=[ANY] * nw, out_specs=[ANY] * nw,
        out_shape=[jax.ShapeDtypeStruct(g.shape, F32) for g in grads],
        input_output_aliases={w: w for w in range(nw)},
        scratch_shapes=[pltpu.SemaphoreType.DMA((nw,)), pltpu.SemaphoreType.DMA((nw,))],
    )(*grads)


def _all_devices(pack, *, reduce, name):
    r = pack.shape[0]

    def body(p_ref, o_ref, buf, send, recv):
        x, y, c = _me()
        me = 4 * x + 2 * y + c
        buf[me] = p_ref[...]
        flips = [(fx, fy, fc) for fx in (0, 1) for fy in (0, 1) for fc in (0, 1)][1:]
        peers = [(x ^ fx, y ^ fy, c ^ fc) for fx, fy, fc in flips]
        cps = [_remote(p_ref, buf.at[me], send, recv, k, peer) for k, peer in enumerate(peers)]
        for cp in cps:
            cp.start()
        for k, (px, py, pc) in enumerate(peers):
            theirs = buf.at[4 * px + 2 * py + pc]
            _remote(theirs, theirs, send, recv, k, (px, py, pc)).wait_recv()
        for cp in cps:
            cp.wait_send()
        if reduce:
            acc = buf[0]
            for d in range(1, N_DEV):
                acc = acc + buf[d]
            o_ref[...] = acc
        else:
            o_ref[...] = buf[...]

    vm = pl.BlockSpec(memory_space=pltpu.VMEM)
    return pl.pallas_call(
        body, name=name,
        in_specs=[vm], out_specs=vm,
        out_shape=jax.ShapeDtypeStruct((r, 128) if reduce else (N_DEV, r, 128), F32),
        scratch_shapes=[pltpu.VMEM((N_DEV, r, 128), F32), pltpu.SemaphoreType.DMA((N_DEV - 1,)),
                        pltpu.SemaphoreType.DMA((N_DEV - 1,))],
        compiler_params=pltpu.CompilerParams(vmem_limit_bytes=VMEM_LIMIT),
    )(pack)


def _to_rows(flat):
    v = flat.reshape(-1)
    rows = -(-v.shape[0] // 1024) * 8
    return jnp.pad(v, (0, rows * 128 - v.shape[0])).reshape(rows, 128)


def kernel(x, w_in, w_dw, b_dw, conv_ln_g, conv_ln_b, rpb, w_out, w_up, w_down, pre_mix_g, post_mix_g, pre_mlp_g, post_mlp_g, loss_target, m_w_in, m_w_dw, m_b_dw, m_conv_ln_g, m_conv_ln_b, m_rpb, m_w_out, m_w_up, m_w_down, m_pre_mix_g, m_post_mix_g, m_pre_mlp_g, m_post_mlp_g, v_w_in, v_w_dw, v_b_dw, v_conv_ln_g, v_conv_ln_b, v_rpb, v_w_out, v_w_up, v_w_down, v_pre_mix_g, v_post_mix_g, v_pre_mlp_g, v_post_mlp_g):
    depth = w_in.shape[0]
    t = x.shape[1]
    xl = x.reshape(t, D_MODEL)
    target = loss_target.reshape(t, D_MODEL)
    chip = 2 * lax.axis_index("x") + lax.axis_index("y")

    big = {"w_in": w_in, "w_out": w_out, "w_up": w_up, "w_down": w_down}
    big_names = list(big)
    shards = [_cast_bf16(big[n].reshape(-1, big[n].shape[-1]), name=f"cast_{n}").reshape(big[n].shape)
              for n in big_names]

    wdw_all = _all_devices(_to_rows(w_dw), reduce=False, name="gather_w_dw")[::2]
    wdw_all = wdw_all.reshape(N_CHIPS, -1)[:, :w_dw.size].reshape((N_CHIPS,) + w_dw.shape)
    wdw_full = jnp.moveaxis(wdw_all, 0, 2).reshape(depth, CONV_K, CONV_W)
    wdw_pad = jnp.pad(wdw_full, ((0, 0), (0, 32 - CONV_K), (0, 0)))

    vec = lambda a, l: a[l].reshape(1, -1)

    saved = []
    h = _rms_fwd(xl, vec(pre_mix_g, 0), name="rms_first")
    (g_in,) = _run_comm(_gather_comm(0, shards[:1]), name="gather_w_in_first")
    g_out = None
    for l in range(depth):
        more = l + 1 < depth
        bias = _bias_table(rpb[l])
        if l == 0:
            proj, g_out = _mm_nn(h, g_in, out_dtype=BF16, name="proj", comm=_gather_comm(0, shards[1:2]))
        else:
            (proj,) = _mm_nn(h, g_in, out_dtype=BF16, name="proj")
        g_out = g_out.reshape(1, D_MODEL, D_MODEL)
        yc, uc = _conv_fwd(proj, wdw_pad[l], vec(b_dw, l), vec(conv_ln_g, l), vec(conv_ln_b, l))
        ya, g_up = _natten_fwd(proj, bias, _gather_comm(l, shards[2:3]))
        cat = jnp.concatenate([yc, ya], axis=1)
        (mix,) = _mm_nn(cat, g_out, out_dtype=F32, name="out_proj")
        x1, h2 = _resid_rms(xl, mix, vec(post_mix_g, l), vec(pre_mlp_g, l), name="mix_resid")
        up, act, g_down = _mm_nn(h2, g_up, out_dtype=BF16, relu2=True, name="mlp_up",
                                 comm=_gather_comm(l, shards[3:]))
        g_down = g_down.reshape(1, D_FF, D_MODEL)
        f, *nxt = _mm_nn(act, g_down, out_dtype=F32, name="mlp_down",
                         comm=_gather_comm(l + 1, shards[:2]) if more else None)
        saved.append(dict(x=xl, h=h, proj=proj, uc=uc, cat=cat, mix=mix, x1=x1, h2=h2, up=up, act=act, f=f,
                          bias=bias, w=(g_in, g_out, g_up, g_down)))
        if more:
            g_in, g_out = nxt
            xl, h = _resid_rms(x1, f, vec(post_mlp_g, l), vec(pre_mix_g, l + 1), name="mlp_resid")
        else:
            (xl,) = _resid_rms(x1, f, vec(post_mlp_g, l), None, name="mlp_resid_last")

    dx, loss_blk = _loss_head(xl, target)
    loss = lax.psum(loss_blk[0, 0], ("x", "y", "c"))

    grads = {n: lax.empty(big[n].shape, F32) for n in big_names}
    small = [None] * depth

    def pair_reduce(n, dw):
        (other,) = _pair_exchange([dw])
        return _pair_sum(dw, other, name=f"pair_sum_{n}")

    def finish(n, layer, part):
        (grads[n],) = _sibling_fill(layer, [_chip_sum(part, grads[n], layer, name=f"chip_sum_{n}_{layer}")])

    q_in = None
    for l in reversed(range(depth)):
        sv = saved[l]
        g_in, g_out, g_up, g_down = sv["w"]
        df, dg_post_mlp = _rms_bwd(dx, sv["f"], vec(post_mlp_g, l), None, out_dtype=BF16, name="rms_bwd_mlp_post")
        dup, *part = _mm_nt(df, g_down, out_dtype=BF16, up=sv["up"], name="d_act",
                            comm=_scatter_comm([q_in]) if q_in is not None else None)
        if part:
            finish("w_in", l + 1, part[0])
        q_down = pair_reduce("w_down", _mm_tn(sv["act"], df, 1, name="dw_down").reshape(N_CHIPS, -1, D_MODEL))
        dh2, part = _mm_nt(dup, g_up, out_dtype=F32, name="d_h2", comm=_scatter_comm([q_down]))
        finish("w_down", l, part)
        q_up = pair_reduce("w_up", _mm_tn(sv["h2"], dup, N_CHIPS, name="dw_up"))
        dx1, dg_pre_mlp = _rms_bwd(dh2, sv["x1"], vec(pre_mlp_g, l), dx, out_dtype=F32, name="rms_bwd_mlp_pre")
        dmix, dg_post_mix = _rms_bwd(dx1, sv["mix"], vec(post_mix_g, l), None, out_dtype=BF16, name="rms_bwd_mix_post")
        (dcat,) = _mm_nt(dmix, g_out, out_dtype=F32, name="d_cat")
        q_out = pair_reduce("w_out", _mm_tn(sv["cat"], dmix, 1, name="dw_out").reshape(N_CHIPS, -1, D_MODEL))
        dq, dk, dv, dbias, part = _natten_bwd(sv["proj"], sv["bias"], dcat, _scatter_comm([q_up]))
        finish("w_up", l, part)
        dag, dwdw, dvec = _conv_bwd(sv["proj"], sv["uc"], dcat, wdw_pad[l], vec(conv_ln_g, l), vec(conv_ln_b, l))
        dproj = jnp.concatenate([dag, dq, dk, dv], axis=1)
        dh, part = _mm_nt(dproj, g_in, out_dtype=F32, name="d_h", comm=_scatter_comm([q_out]))
        finish("w_out", l, part)
        q_in = pair_reduce("w_in", _mm_tn(sv["h"], dproj, N_CHIPS, name="dw_in"))
        dx, dg_pre_mix = _rms_bwd(dh, sv["x"], vec(pre_mix_g, l), dx1, out_dtype=F32, name="rms_bwd_mix_pre")

        drpb = _rpb_grad(dbias)
        small[l] = jnp.concatenate([
            dvec[0], dvec[1], dvec[2], dg_pre_mix[0], dg_post_mix[0], dg_pre_mlp[0], dg_post_mlp[0],
            drpb.reshape(-1), dwdw[:CONV_K].reshape(-1)])

    finish("w_in", 0, _run_comm(_scatter_comm([q_in]), name="grad_scatter_w_in_last")[0])
    grads = [grads[n] for n in big_names]

    small_sum = _all_devices(_to_rows(jnp.stack(small)), reduce=True, name="reduce_small_grads")
    small_sum = small_sum.reshape(-1)[:depth * small[0].shape[0]].reshape(depth, -1)
    sizes = [CONV_W, CONV_W, CONV_W, D_MODEL, D_MODEL, D_MODEL, D_MODEL, rpb[0].size, CONV_K * CONV_W]
    offs = np.concatenate([[0], np.cumsum(sizes)])
    pieces = [small_sum[:, offs[i]:offs[i + 1]] for i in range(len(sizes))]
    g_small = {
        "b_dw": pieces[0], "conv_ln_g": pieces[1], "conv_ln_b": pieces[2],
        "pre_mix_g": pieces[3], "post_mix_g": pieces[4], "pre_mlp_g": pieces[5], "post_mlp_g": pieces[6],
        "rpb": pieces[7].reshape(rpb.shape),
        "w_dw": lax.dynamic_slice_in_dim(pieces[8].reshape(depth, CONV_K, CONV_W), chip * w_dw.shape[2],
                                         w_dw.shape[2], axis=2),
    }

    given = dict(w_in=(w_in, m_w_in, v_w_in), w_dw=(w_dw, m_w_dw, v_w_dw), b_dw=(b_dw, m_b_dw, v_b_dw),
                 conv_ln_g=(conv_ln_g, m_conv_ln_g, v_conv_ln_g), conv_ln_b=(conv_ln_b, m_conv_ln_b, v_conv_ln_b),
                 rpb=(rpb, m_rpb, v_rpb), w_out=(w_out, m_w_out, v_w_out), w_up=(w_up, m_w_up, v_w_up),
                 w_down=(w_down, m_w_down, v_w_down), pre_mix_g=(pre_mix_g, m_pre_mix_g, v_pre_mix_g),
                 post_mix_g=(post_mix_g, m_post_mix_g, v_post_mix_g), pre_mlp_g=(pre_mlp_g, m_pre_mlp_g, v_pre_mlp_g),
                 post_mlp_g=(post_mlp_g, m_post_mlp_g, v_post_mlp_g))
    results = {}
    for n, g in zip(big_names, grads):
        w, m, v = given[n]
        flat = lambda a: a.reshape(-1, a.shape[-1])
        outs = _adamw(flat(w), flat(g), flat(m), flat(v), name=f"adamw_{n}")
        results[n] = [o.reshape(w.shape) for o in outs]

    small_names = list(g_small)
    pack = lambda arrs: _to_rows(jnp.concatenate([a.reshape(-1) for a in arrs]))
    outs = _adamw(pack([given[n][0] for n in small_names]), pack([g_small[n] for n in small_names]),
                  pack([given[n][1] for n in small_names]), pack([given[n][2] for n in small_names]),
                  name="adamw_small")
    pos = 0
    for n in small_names:
        w = given[n][0]
        results[n] = [o.reshape(-1)[pos:pos + w.size].reshape(w.shape) for o in outs]
        pos += w.size

    order = ["w_in", "w_dw", "b_dw", "conv_ln_g", "conv_ln_b", "rpb", "w_out", "w_up", "w_down",
             "pre_mix_g", "post_mix_g", "pre_mlp_g", "post_mlp_g"]
    return (loss, dx.reshape(x.shape), *[results[n][0] for n in order], *[results[n][1] for n in order],
            *[results[n][2] for n in order], *[results[n][3] for n in order])
```

```python
import functools

import numpy as np
import jax
import jax.numpy as jnp
from jax import lax
from jax.experimental import pallas as pl
from jax.experimental.pallas import tpu as pltpu

F32 = jnp.float32
BF16 = jnp.bfloat16

D_MODEL = 2048
CONV_W = 1024
NA_W = 1024
N_HEADS = 16
HEAD_D = 64
GRID_W = 64
WIN_ROWS = 8
WIN_COLS = 16
CONV_K = 31
D_FF = 4 * D_MODEL
IN_COLS = 2 * CONV_W + 3 * NA_W
RMS_EPS = 1e-6
LN_EPS = 1e-5
NEG_INF = -1e30
N_CHIPS = 4
N_DEV = 8
HALO = 16
BAND = WIN_ROWS * GRID_W

ADAM_LR = 0.001
ADAM_B1 = 0.9
ADAM_B2 = 0.999
ADAM_EPS = 1e-08
ADAM_WD = 0.01
ADAM_STEP = 10

VMEM_LIMIT = 48 * 1024 * 1024
MESH = pl.DeviceIdType.MESH
ANY = pl.BlockSpec(memory_space=pl.ANY)


def _params(*sem):
    return pltpu.CompilerParams(dimension_semantics=sem, vmem_limit_bytes=VMEM_LIMIT)


class _Comm:
    def __init__(self, ins, out_shape, sems, start, finish, aliased=False):
        self.ins, self.out_shape, self.sems, self.start, self.finish = ins, out_shape, sems, start, finish
        self.aliased = aliased


def _call(body, *, name, grid, in_specs, out_specs, out_shape, scratch_shapes=(), sem, args, comm=None):
    in_specs, out_specs, out_shape = list(in_specs), list(out_specs), list(out_shape)
    scratch_shapes = list(scratch_shapes)
    if comm is None:
        return pl.pallas_call(body, name=name, grid=grid, in_specs=in_specs, out_specs=out_specs,
                              out_shape=out_shape, scratch_shapes=scratch_shapes,
                              compiler_params=_params(*sem))(*args)
    ni, no, ns = len(in_specs), len(out_specs), len(scratch_shapes)
    nci, nco = len(comm.ins), len(comm.out_shape)

    def full(*refs):
        ins, refs = refs[:ni], refs[ni:]
        cins, refs = refs[:nci], refs[nci:]
        outs, refs = refs[:no], refs[no:]
        couts, refs = refs[:nco], refs[nco:]
        scr, csems = refs[:ns], refs[ns:]
        ids = [pl.program_id(d) for d in range(len(grid))]
        first = functools.reduce(jnp.logical_and, [i == 0 for i in ids])
        last = functools.reduce(jnp.logical_and, [i == g - 1 for i, g in zip(ids, grid)])

        @pl.when(first)
        def _():
            comm.start(cins, couts, csems)

        body(*ins, *outs, *scr)

        @pl.when(last)
        def _():
            comm.finish(cins, couts, csems)

    return pl.pallas_call(
        full, name=name, grid=grid,
        in_specs=in_specs + [ANY] * nci, out_specs=out_specs + [ANY] * nco,
        out_shape=out_shape + list(comm.out_shape),
        scratch_shapes=scratch_shapes + list(comm.sems),
        input_output_aliases={ni + i: no + i for i in range(nci)} if comm.aliased else {},
        compiler_params=_params(*(["arbitrary"] * len(grid))),
    )(*args, *comm.ins)


K_TILES = (2048, 1280, 1024, 512)


def _tile(n, pref):
    for t in pref:
        if n % t == 0:
            return t
    return n


def _accumulate(acc, step, n_steps, finish, part):
    if n_steps == 1:
        finish(part())
        return

    @pl.when(step == 0)
    def _():
        acc[...] = part()

    @pl.when(jnp.logical_and(step > 0, step < n_steps - 1))
    def _():
        acc[...] += part()

    @pl.when(step == n_steps - 1)
    def _():
        finish(acc[...] + part())


def _mm_nn(a, w, *, out_dtype, relu2=False, name, comm=None):
    m, k = a.shape
    s, _, n = w.shape
    tm = _tile(m, (1024, 512, 256))
    tn = _tile(n, (1024, 1280, 512))
    tk = _tile(k, K_TILES)
    nps = n // tn
    nk = k // tk

    def body(a_ref, w_ref, *rest):
        outs, acc = rest[:-1], rest[-1]

        def finish(r):
            outs[0][...] = r.astype(outs[0].dtype)
            if relu2:
                p = jnp.maximum(r, 0.0)
                outs[1][...] = (p * p).astype(outs[1].dtype)

        _accumulate(acc, pl.program_id(2), nk, finish,
                    lambda: jnp.dot(a_ref[...], w_ref[...], preferred_element_type=F32))

    o_spec = pl.BlockSpec((tm, tn), lambda i, j, kk: (i, j))
    o_shape = jax.ShapeDtypeStruct((m, s * n), out_dtype)
    return _call(
        body, name=name,
        grid=(m // tm, s * nps, nk),
        in_specs=[pl.BlockSpec((tm, tk), lambda i, j, kk: (i, kk)),
                  pl.BlockSpec((None, tk, tn), lambda i, j, kk: (j // nps, kk, j % nps))],
        out_specs=[o_spec, o_spec] if relu2 else [o_spec],
        out_shape=[o_shape, o_shape] if relu2 else [o_shape],
        scratch_shapes=[pltpu.VMEM((tm, tn), F32)],
        sem=("parallel", "parallel", "arbitrary"), args=(a, w), comm=comm)


def _mm_nt(dy, w, *, out_dtype, up=None, name, comm=None):
    m = dy.shape[0]
    s, k, n = w.shape
    tm = _tile(m, (1024, 512, 256))
    tko = _tile(k, (1024, 512))
    tn = _tile(n, K_TILES)
    nps = n // tn
    nn = s * nps

    def body(dy_ref, w_ref, *rest):
        if up is None:
            o_ref, acc = rest
        else:
            up_ref, o_ref, acc = rest

        def finish(r):
            if up is not None:
                r = r * (2.0 * jnp.maximum(up_ref[...].astype(F32), 0.0))
            o_ref[...] = r.astype(o_ref.dtype)

        _accumulate(acc, pl.program_id(2), nn, finish,
                    lambda: lax.dot_general(dy_ref[...], w_ref[...], (((1,), (1,)), ((), ())),
                                            preferred_element_type=F32))

    in_specs = [pl.BlockSpec((tm, tn), lambda i, j, kk: (i, kk)),
                pl.BlockSpec((None, tko, tn), lambda i, j, kk: (kk // nps, j, kk % nps))]
    args = [dy, w]
    if up is not None:
        in_specs.append(pl.BlockSpec((tm, tko), lambda i, j, kk: (i, j)))
        args.append(up)
    return _call(
        body, name=name,
        grid=(m // tm, k // tko, nn),
        in_specs=in_specs,
        out_specs=[pl.BlockSpec((tm, tko), lambda i, j, kk: (i, j))],
        out_shape=[jax.ShapeDtypeStruct((m, k), out_dtype)],
        scratch_shapes=[pltpu.VMEM((tm, tko), F32)],
        sem=("parallel", "parallel", "arbitrary"), args=args, comm=comm)


def _mm_tn(a, dy, s, *, name):
    m, k = a.shape
    n = dy.shape[1] // s
    tk = _tile(k, (1024, 512))
    tn = _tile(n, (1024, 1280, 512))
    tm = _tile(m, K_TILES)
    nps = n // tn
    nm = m // tm

    def body(a_ref, dy_ref, o_ref, acc):
        def finish(r):
            o_ref[...] = r.astype(o_ref.dtype)

        _accumulate(acc, pl.program_id(2), nm, finish,
                    lambda: lax.dot_general(a_ref[...], dy_ref[...], (((0,), (0,)), ((), ())),
                                            preferred_element_type=F32))

    return pl.pallas_call(
        body, name=name,
        grid=(k // tk, s * nps, nm),
        in_specs=[pl.BlockSpec((tm, tk), lambda i, j, mm: (mm, i)),
                  pl.BlockSpec((tm, tn), lambda i, j, mm: (mm, j))],
        out_specs=pl.BlockSpec((None, tk, tn), lambda i, j, mm: (j // nps, i, j % nps)),
        out_shape=jax.ShapeDtypeStruct((s, k, n), BF16),
        scratch_shapes=[pltpu.VMEM((tk, tn), F32)],
        compiler_params=_params("parallel", "parallel", "arbitrary"),
    )(a, dy)


ROWS = 256


def _row_spec(d):
    return pl.BlockSpec((ROWS, d), lambda i: (i, 0))


def _vec_spec(d):
    return pl.BlockSpec((1, d), lambda i: (0, 0))


def _rstd(v):
    return lax.rsqrt(jnp.mean(v * v, axis=-1, keepdims=True) + RMS_EPS)


def _rms_fwd(x, g, *, name):
    t, d = x.shape

    def body(x_ref, g_ref, h_ref):
        xv = x_ref[...]
        h_ref[...] = ((xv * _rstd(xv)) * g_ref[...]).astype(BF16)

    return pl.pallas_call(
        body, name=name, grid=(t // ROWS,),
        in_specs=[_row_spec(d), _vec_spec(d)],
        out_specs=_row_spec(d),
        out_shape=jax.ShapeDtypeStruct((t, d), BF16),
        compiler_params=_params("parallel"),
    )(x, g)


def _resid_rms(res, y, g_post, g_next, *, name, comm=None):
    t, d = res.shape
    with_next = g_next is not None

    def body(res_ref, y_ref, gp_ref, *rest):
        yv = y_ref[...]
        xn = res_ref[...] + (yv * _rstd(yv)) * gp_ref[...]
        if with_next:
            gn_ref, xo_ref, h_ref = rest
            h_ref[...] = ((xn * _rstd(xn)) * gn_ref[...]).astype(BF16)
        else:
            (xo_ref,) = rest
        xo_ref[...] = xn

    in_specs = [_row_spec(d), _row_spec(d), _vec_spec(d)]
    args = [res, y, g_post]
    out_specs = [_row_spec(d)]
    out_shape = [jax.ShapeDtypeStruct((t, d), F32)]
    if with_next:
        in_specs.append(_vec_spec(d))
        args.append(g_next)
        out_specs.append(_row_spec(d))
        out_shape.append(jax.ShapeDtypeStruct((t, d), BF16))
    return _call(body, name=name, grid=(t // ROWS,), in_specs=in_specs, out_specs=out_specs, out_shape=out_shape,
                 sem=("parallel",), args=args, comm=comm)


def _rms_bwd(dy, xin, g, res, *, out_dtype, name):
    t, d = xin.shape
    with_res = res is not None

    def body(dy_ref, x_ref, g_ref, *rest):
        if with_res:
            res_ref, dx_ref, dg_ref = rest
        else:
            dx_ref, dg_ref = rest
        xv = x_ref[...]
        r = _rstd(xv)
        nrm = xv * r
        dyv = dy_ref[...]
        dn = dyv * g_ref[...]
        dx = r * (dn - nrm * jnp.mean(dn * nrm, axis=-1, keepdims=True))
        if with_res:
            dx = dx + res_ref[...]
        dx_ref[...] = dx.astype(dx_ref.dtype)

        @pl.when(pl.program_id(0) == 0)
        def _():
            dg_ref[...] = jnp.zeros_like(dg_ref)

        dg_ref[...] += jnp.sum(dyv * nrm, axis=0, keepdims=True)

    in_specs = [_row_spec(d), _row_spec(d), _vec_spec(d)]
    args = [dy, xin, g]
    if with_res:
        in_specs.append(_row_spec(d))
        args.append(res)
    return pl.pallas_call(
        body, name=name, grid=(t // ROWS,),
        in_specs=in_specs,
        out_specs=(_row_spec(d), _vec_spec(d)),
        out_shape=(jax.ShapeDtypeStruct((t, d), out_dtype), jax.ShapeDtypeStruct((1, d), F32)),
        compiler_params=_params("arbitrary"),
    )(*args)


def _loss_head(y, target):
    t, d = y.shape

    def body(y_ref, t_ref, dy_ref, loss_ref):
        diff = y_ref[...] - t_ref[...]
        dy_ref[...] = diff * (1.0 / d)

        @pl.when(pl.program_id(0) == 0)
        def _():
            loss_ref[...] = jnp.zeros_like(loss_ref)

        loss_ref[...] += jnp.sum(diff * diff) * (0.5 / d)

    return pl.pallas_call(
        body, name="loss_head", grid=(t // ROWS,),
        in_specs=[_row_spec(d), _row_spec(d)],
        out_specs=(_row_spec(d), pl.BlockSpec((8, 128), lambda i: (0, 0))),
        out_shape=(jax.ShapeDtypeStruct((t, d), F32), jax.ShapeDtypeStruct((8, 128), F32)),
        compiler_params=_params("arbitrary"),
    )(y, target)


def _halo_specs(t, col, width):
    rb = ROWS // HALO
    last = t // HALO - 1
    return [pl.BlockSpec((ROWS, width), lambda i: (i, col)),
            pl.BlockSpec((HALO, width), lambda i: (jnp.maximum(i * rb - 1, 0), col)),
            pl.BlockSpec((HALO, width), lambda i: (jnp.minimum((i + 1) * rb, last), col))]


def _glu(a_ref, g_ref):
    return a_ref[...].astype(F32) * jax.nn.sigmoid(g_ref[...].astype(F32))


def _fill_ext(ext, cur, prev, nxt, i, nblk):
    ext[pl.ds(HALO, ROWS), :] = cur
    ext[pl.ds(0, HALO), :] = jnp.where(i > 0, prev, 0.0)
    ext[pl.ds(HALO + ROWS, HALO), :] = jnp.where(i < nblk - 1, nxt, 0.0)


def _conv_fwd(proj, w_dw, b_dw, ln_g, ln_b, comm=None):
    t = proj.shape[0]
    nblk = t // ROWS
    pad = HALO - CONV_K // 2

    def body(ac, ap, an, gc, gp, gn, w_ref, b_ref, lg_ref, lb_ref, yc_ref, uc_ref, uext):
        i = pl.program_id(0)
        _fill_ext(uext, _glu(ac, gc), _glu(ap, gp), _glu(an, gn), i, nblk)
        acc = jnp.broadcast_to(b_ref[...], (ROWS, CONV_W))
        for j in range(CONV_K):
            acc = acc + uext[pl.ds(j + pad, ROWS), :] * w_ref[pl.ds(j, 1), :]
        uc_ref[...] = acc
        mu = jnp.mean(acc, axis=-1, keepdims=True)
        xc = acc - mu
        var = jnp.mean(xc * xc, axis=-1, keepdims=True)
        yln = xc * lax.rsqrt(var + LN_EPS) * lg_ref[...] + lb_ref[...]
        yc_ref[...] = (yln * jax.nn.sigmoid(yln)).astype(BF16)

    vec = pl.BlockSpec((1, CONV_W), lambda i: (0, 0))
    return _call(
        body, name="conv_fwd", grid=(nblk,),
        in_specs=_halo_specs(t, 0, CONV_W) + _halo_specs(t, 1, CONV_W)
        + [pl.BlockSpec((32, CONV_W), lambda i: (0, 0)), vec, vec, vec],
        out_specs=[pl.BlockSpec((ROWS, CONV_W), lambda i: (i, 0)),
                   pl.BlockSpec((ROWS, CONV_W), lambda i: (i, 0))],
        out_shape=[jax.ShapeDtypeStruct((t, CONV_W), BF16), jax.ShapeDtypeStruct((t, CONV_W), F32)],
        scratch_shapes=[pltpu.VMEM((ROWS + 2 * HALO, CONV_W), F32)],
        sem=("parallel",), args=(proj, proj, proj, proj, proj, proj, w_dw, b_dw, ln_g, ln_b), comm=comm)


def _conv_bwd(proj, uc, dcat, w_dw, ln_g, ln_b):
    t = proj.shape[0]
    nblk = t // ROWS
    pad = HALO - CONV_K // 2

    def body(ac, ap, an, gc, gp, gn, uc_c, uc_p, uc_n, dy_c, dy_p, dy_n, w_ref, lg_ref, lb_ref,
             dag_ref, dw_ref, dvec_ref, uext, dext):
        i = pl.program_id(0)
        _fill_ext(uext, _glu(ac, gc), _glu(ap, gp), _glu(an, gn), i, nblk)

        def ln_bwd(u_ref, d_ref):
            u = u_ref[...]
            mu = jnp.mean(u, axis=-1, keepdims=True)
            xc = u - mu
            rstd = lax.rsqrt(jnp.mean(xc * xc, axis=-1, keepdims=True) + LN_EPS)
            xhat = xc * rstd
            yln = xhat * lg_ref[...] + lb_ref[...]
            sg = jax.nn.sigmoid(yln)
            dyln = d_ref[...] * (sg * (1.0 + yln * (1.0 - sg)))
            dxh = dyln * lg_ref[...]
            du = rstd * (dxh - jnp.mean(dxh, axis=-1, keepdims=True)
                         - xhat * jnp.mean(dxh * xhat, axis=-1, keepdims=True))
            return du, dyln, xhat

        duc, dyln, xhat = ln_bwd(uc_c, dy_c)
        _fill_ext(dext, duc, ln_bwd(uc_p, dy_p)[0], ln_bwd(uc_n, dy_n)[0], i, nblk)

        @pl.when(i == 0)
        def _():
            dw_ref[...] = jnp.zeros_like(dw_ref)
            dvec_ref[...] = jnp.zeros_like(dvec_ref)

        dvec_ref[pl.ds(0, 1), :] += jnp.sum(duc, axis=0, keepdims=True)
        dvec_ref[pl.ds(1, 1), :] += jnp.sum(dyln * xhat, axis=0, keepdims=True)
        dvec_ref[pl.ds(2, 1), :] += jnp.sum(dyln, axis=0, keepdims=True)

        du = jnp.zeros((ROWS, CONV_W), F32)
        for j in range(CONV_K):
            du = du + dext[pl.ds(2 * HALO - pad - j, ROWS), :] * w_ref[pl.ds(j, 1), :]
            dw_ref[pl.ds(j, 1), :] += jnp.sum(duc * uext[pl.ds(j + pad, ROWS), :], axis=0, keepdims=True)

        a = ac[...].astype(F32)
        sg = jax.nn.sigmoid(gc[...].astype(F32))
        dag_ref[:, pl.ds(0, CONV_W)] = (du * sg).astype(BF16)
        dag_ref[:, pl.ds(CONV_W, CONV_W)] = (du * a * sg * (1.0 - sg)).astype(BF16)

    vec = pl.BlockSpec((1, CONV_W), lambda i: (0, 0))
    return pl.pallas_call(
        body, name="conv_bwd", grid=(nblk,),
        in_specs=_halo_specs(t, 0, CONV_W) + _halo_specs(t, 1, CONV_W) + _halo_specs(t, 0, CONV_W)
        + _halo_specs(t, 0, CONV_W) + [pl.BlockSpec((32, CONV_W), lambda i: (0, 0)), vec, vec],
        out_specs=(pl.BlockSpec((ROWS, 2 * CONV_W), lambda i: (i, 0)),
                   pl.BlockSpec((32, CONV_W), lambda i: (0, 0)),
                   pl.BlockSpec((8, CONV_W), lambda i: (0, 0))),
        out_shape=(jax.ShapeDtypeStruct((t, 2 * CONV_W), BF16),
                   jax.ShapeDtypeStruct((32, CONV_W), F32),
                   jax.ShapeDtypeStruct((8, CONV_W), F32)),
        scratch_shapes=[pltpu.VMEM((ROWS + 2 * HALO, CONV_W), F32),
                        pltpu.VMEM((ROWS + 2 * HALO, CONV_W), F32)],
        compiler_params=_params("arbitrary"),
    )(proj, proj, proj, proj, proj, proj, uc, uc, uc, dcat, dcat, dcat, w_dw, ln_g, ln_b)


Q_BLK, K_BLK, V_BLK = 16, 24, 32


def _head_masks():
    lane = lax.broadcasted_iota(jnp.int32, (1, 2 * HEAD_D), 1)
    return [lane < HEAD_D, lane >= HEAD_D]


def _band(r, rows):
    rs = jnp.clip(r - WIN_ROWS // 2, 0, rows - WIN_ROWS)
    return rs, r - rs, pl.multiple_of(r * GRID_W, GRID_W), pl.multiple_of(rs * GRID_W, GRID_W)


ROWS_PER_STEP = 2
NT_DIMS = (((1,), (1,)), ((), ()))
TN_DIMS = (((0,), (0,)), ((), ()))
N_DR = 16
GE_SPEC = pl.BlockSpec((2, N_DR, GRID_W, GRID_W), lambda h: (h, 0, 0, 0))
BIAS_SCRATCH = pltpu.VMEM((WIN_ROWS, 2 * GRID_W, BAND), F32)


def _stack_heads(block, masks):
    z = jnp.zeros_like(block)
    return jnp.concatenate([jnp.where(masks[0], block, z), jnp.where(masks[1], block, z)], axis=0)


def _unstack_heads(both, masks):
    return jnp.where(masks[0], both[:GRID_W], both[GRID_W:])


def _softmax(s, bias):
    s = s * (HEAD_D ** -0.5) + bias
    e = jnp.exp(s - jnp.max(s, axis=-1, keepdims=True))
    return e * (1.0 / jnp.sum(e, axis=-1, keepdims=True))


def _fill_bias(ge_ref, bias):
    wq = lax.broadcasted_iota(jnp.int32, (GRID_W, BAND), 0)
    wk = jnp.bitwise_and(lax.broadcasted_iota(jnp.int32, (GRID_W, BAND), 1), GRID_W - 1)
    cs = jnp.clip(wq - WIN_COLS // 2, 0, GRID_W - WIN_COLS)
    inside = jnp.logical_and(wk >= cs, wk < cs + WIN_COLS)
    for hh in range(2):
        for off in range(WIN_ROWS):
            tile = jnp.concatenate([ge_ref[hh, WIN_ROWS - 1 - off + kr] for kr in range(WIN_ROWS)], axis=-1)
            bias[off, pl.ds(hh * GRID_W, GRID_W), :] = jnp.where(inside, tile, NEG_INF)


def _natten_fwd(proj, ge, comm=None):
    t = proj.shape[0]
    rows = t // GRID_W

    def body(q_ref, k_ref, v_ref, ge_ref, o_ref, bias):
        masks = _head_masks()
        _fill_bias(ge_ref, bias)

        def step(i, carry):
            bands = [_band(i * ROWS_PER_STEP + u, rows) for u in range(ROWS_PER_STEP)]
            vbs = [v_ref[pl.ds(k0, BAND), :] for _, _, _, k0 in bands]
            ss = [lax.dot_general(_stack_heads(q_ref[pl.ds(q0, GRID_W), :], masks), k_ref[pl.ds(k0, BAND), :],
                                  NT_DIMS, preferred_element_type=F32) for _, _, q0, k0 in bands]
            ps = [_softmax(s, bias[off]) for s, (_, off, _, _) in zip(ss, bands)]
            os = [jnp.dot(p.astype(BF16), vb, preferred_element_type=F32) for p, vb in zip(ps, vbs)]
            for o, (_, _, q0, _) in zip(os, bands):
                o_ref[pl.ds(q0, GRID_W), :] = _unstack_heads(o, masks).astype(BF16)
            return carry

        lax.fori_loop(0, rows // ROWS_PER_STEP, step, 0)

    col = lambda b: pl.BlockSpec((t, 2 * HEAD_D), lambda h: (0, b + h))
    return _call(
        body, name="natten_fwd", grid=(N_HEADS // 2,),
        in_specs=[col(Q_BLK), col(K_BLK), col(V_BLK), GE_SPEC],
        out_specs=[pl.BlockSpec((t, 2 * HEAD_D), lambda h: (0, h))],
        out_shape=[jax.ShapeDtypeStruct((t, NA_W), BF16)],
        scratch_shapes=[BIAS_SCRATCH],
        sem=("parallel",), args=(proj, proj, proj, ge), comm=comm)


def _natten_bwd(proj, ge, dcat, comm=None):
    t = proj.shape[0]
    rows = t // GRID_W
    scale = HEAD_D ** -0.5

    def body(q_ref, k_ref, v_ref, ge_ref, do_ref, dq_ref, dk_ref, dv_ref, dge_ref, bias, dbias, dk_acc, dv_acc):
        masks = _head_masks()
        _fill_bias(ge_ref, bias)
        dk_acc[...] = jnp.zeros_like(dk_acc)
        dv_acc[...] = jnp.zeros_like(dv_acc)
        dbias[...] = jnp.zeros_like(dbias)

        def step(i, carry):
            bands = [_band(i * ROWS_PER_STEP + u, rows) for u in range(ROWS_PER_STEP)]
            kbs = [k_ref[pl.ds(k0, BAND), :] for _, _, _, k0 in bands]
            q2s = [_stack_heads(q_ref[pl.ds(q0, GRID_W), :], masks) for _, _, q0, _ in bands]
            do2s = [_stack_heads(do_ref[pl.ds(q0, GRID_W), :].astype(BF16), masks) for _, _, q0, _ in bands]
            ss = [lax.dot_general(q2, kb, NT_DIMS, preferred_element_type=F32) for q2, kb in zip(q2s, kbs)]
            dps = [lax.dot_general(do2, v_ref[pl.ds(k0, BAND), :], NT_DIMS, preferred_element_type=F32)
                   for do2, (_, _, _, k0) in zip(do2s, bands)]
            ps = [_softmax(s, bias[off]) for s, (_, off, _, _) in zip(ss, bands)]
            dss =[p * (dp - jnp.sum(p * dp, axis=-1, keepdims=True)) for p, dp in zip(ps, dps)]
            dsbs = [ds.astype(BF16) for ds in dss]
            dqs = [jnp.dot(dsb, kb, preferred_element_type=F32) for dsb, kb in zip(dsbs, kbs)]
            dks = [lax.dot_general(dsb, q2, TN_DIMS, preferred_element_type=F32) for dsb, q2 in zip(dsbs, q2s)]
            dvs = [lax.dot_general(p.astype(BF16), do2, TN_DIMS, preferred_element_type=F32)
                   for p, do2 in zip(ps, do2s)]
            for ds, dq, dk, dv, (_, off, q0, k0) in zip(dss, dqs, dks, dvs, bands):
                dbias[off] += ds
                dq_ref[pl.ds(q0, GRID_W), :] = (_unstack_heads(dq, masks) * scale).astype(BF16)
                dk_acc[pl.ds(k0, BAND), :] += dk * scale
                dv_acc[pl.ds(k0, BAND), :] += dv
            return carry

        lax.fori_loop(0, rows // ROWS_PER_STEP, step, 0)
        dk_ref[...] = dk_acc[...].astype(BF16)
        dv_ref[...] = dv_acc[...].astype(BF16)
        for hh in range(2):
            for dr in range(N_DR):
                pieces = [dbias[off, pl.ds(hh * GRID_W, GRID_W), pl.ds((dr + off - WIN_ROWS + 1) * GRID_W, GRID_W)]
                          for off in range(WIN_ROWS) if 0 <= dr + off - WIN_ROWS + 1 < WIN_ROWS]
                dge_ref[hh, dr] = functools.reduce(jnp.add, pieces) if pieces else jnp.zeros((GRID_W, GRID_W), F32)

    col = lambda b: pl.BlockSpec((t, 2 * HEAD_D), lambda h: (0, b + h))
    o_spec = pl.BlockSpec((t, 2 * HEAD_D), lambda h: (0, h))
    o_shape = jax.ShapeDtypeStruct((t, NA_W), BF16)
    return _call(
        body, name="natten_bwd", grid=(N_HEADS // 2,),
        in_specs=[col(Q_BLK), col(K_BLK), col(V_BLK), GE_SPEC, col(CONV_W // (2 * HEAD_D))],
        out_specs=[o_spec, o_spec, o_spec, GE_SPEC],
        out_shape=[o_shape, o_shape, o_shape, jax.ShapeDtypeStruct((N_HEADS, N_DR, GRID_W, GRID_W), F32)],
        scratch_shapes=[BIAS_SCRATCH, BIAS_SCRATCH, pltpu.VMEM((t, 2 * HEAD_D), F32),
                        pltpu.VMEM((t, 2 * HEAD_D), F32)],
        sem=("parallel",), args=(proj, proj, proj, ge, dcat), comm=comm)


def _column_offsets():
    c = np.arange(GRID_W)
    return (np.clip(c[None, :] - c[:, None], -(WIN_COLS - 1), WIN_COLS - 1) + (WIN_COLS - 1)).reshape(-1)


def _rpb_expand(rpb_l):
    n_dc = 32
    et = np.zeros((n_dc, GRID_W * GRID_W), np.float32)
    et[_column_offsets(), np.arange(GRID_W * GRID_W)] = 1.0
    rp = jnp.pad(rpb_l, ((0, 0), (0, N_DR - rpb_l.shape[1]), (0, n_dc - rpb_l.shape[2]))).reshape(N_HEADS * N_DR, n_dc)

    def body(r_ref, e_ref, o_ref):
        o_ref[...] = jnp.dot(r_ref[...], e_ref[...], preferred_element_type=F32, precision=lax.Precision.HIGHEST)

    vm = pl.BlockSpec(memory_space=pltpu.VMEM)
    ge = pl.pallas_call(
        body, name="rpb_expand", in_specs=[vm, vm], out_specs=vm,
        out_shape=jax.ShapeDtypeStruct((N_HEADS * N_DR, GRID_W * GRID_W), F32),
        compiler_params=pltpu.CompilerParams(vmem_limit_bytes=VMEM_LIMIT),
    )(rp, jnp.asarray(et))
    return ge.reshape(N_HEADS, N_DR, GRID_W, GRID_W)


def _rpb_grad(dge):
    e = np.zeros((GRID_W * GRID_W, 128), np.float32)
    e[np.arange(GRID_W * GRID_W), _column_offsets()] = 1.0

    def body(z_ref, e_ref, o_ref):
        o_ref[...] = jnp.dot(z_ref[...], e_ref[...], preferred_element_type=F32, precision=lax.Precision.HIGHEST)

    vm = pl.BlockSpec(memory_space=pltpu.VMEM)
    out = pl.pallas_call(
        body, name="rpb_grad", in_specs=[vm, vm], out_specs=vm,
        out_shape=jax.ShapeDtypeStruct((N_HEADS * N_DR, 128), F32),
        compiler_params=pltpu.CompilerParams(vmem_limit_bytes=VMEM_LIMIT),
    )(dge.reshape(N_HEADS * N_DR, GRID_W * GRID_W), jnp.asarray(e))
    return out.reshape(N_HEADS, N_DR, 128)[:, :2 * WIN_ROWS - 1, :2 * WIN_COLS - 1]


def _cast_bf16(w2d, *, name):
    r, c = w2d.shape
    tr = _tile(r, (512, 256))

    def body(w_ref, o_ref):
        o_ref[...] = w_ref[...].astype(BF16)

    return pl.pallas_call(
        body, name=name, grid=(r // tr,),
        in_specs=[pl.BlockSpec((tr, c), lambda i: (i, 0))],
        out_specs=pl.BlockSpec((tr, c), lambda i: (i, 0)),
        out_shape=jax.ShapeDtypeStruct((r, c), BF16),
        compiler_params=_params("parallel"),
    )(w2d)


def _core_index():
    return lax.axis_index("c").astype(jnp.int32).reshape(1)


def _pair_sum(dw, other, *, name):
    s, r, c = dw.shape
    h = r // 2
    tr = _tile(h, (256, 128))
    nb = h // tr

    def body(c_ref, a_ref, b_ref, o_ref):
        del c_ref
        o_ref[...] = (a_ref[...].astype(F32) + b_ref[...].astype(F32)).astype(BF16)

    return pl.pallas_call(
        body, name=name,
        grid_spec=pltpu.PrefetchScalarGridSpec(
            num_scalar_prefetch=1, grid=(s, nb),
            in_specs=[pl.BlockSpec((None, tr, c), lambda si, i, cr: (si, cr[0] * nb + i, 0)),
                      pl.BlockSpec((None, tr, c), lambda si, i, cr: (si, i, 0))],
            out_specs=pl.BlockSpec((None, tr, c), lambda si, i, cr: (si, i, 0))),
        out_shape=jax.ShapeDtypeStruct((s, h, c), BF16),
        compiler_params=_params("parallel", "parallel"),
    )(_core_index(), dw, other)


def _chip_sum(parts, grad, layer, *, name):
    s, h, c = parts.shape
    tr = _tile(h, (256, 128))
    nb = h // tr

    def body(c_ref, p_ref, g_in, o_ref):
        del c_ref, g_in
        acc = p_ref[0].astype(F32) + p_ref[1].astype(F32)
        acc = acc + p_ref[2].astype(F32)
        o_ref[...] = acc + p_ref[3].astype(F32)

    return pl.pallas_call(
        body, name=name,
        grid_spec=pltpu.PrefetchScalarGridSpec(
            num_scalar_prefetch=1, grid=(nb,),
            in_specs=[pl.BlockSpec((s, tr, c), lambda i, cr: (0, i, 0)), ANY],
            out_specs=pl.BlockSpec((None, tr, c), lambda i, cr: (layer, cr[0] * nb + i, 0))),
        out_shape=jax.ShapeDtypeStruct(grad.shape, F32),
        input_output_aliases={2: 0},
        compiler_params=_params("parallel"),
    )(_core_index(), parts, grad)


def _adamw(w, g, m, v, *, name):
    r, c = w.shape
    tr = _tile(r, (256, 128, 8))
    tr = tr if r % tr == 0 else r
    bc1 = 1.0 - ADAM_B1 ** ADAM_STEP
    bc2 = 1.0 - ADAM_B2 ** ADAM_STEP

    def body(w_ref, g_ref, m_ref, v_ref, go_ref, d_ref, mo_ref, vo_ref):
        gv = g_ref[...]
        mn = ADAM_B1 * m_ref[...] + (1.0 - ADAM_B1) * gv
        vn = ADAM_B2 * v_ref[...] + (1.0 - ADAM_B2) * (gv * gv)
        go_ref[...] = gv
        mo_ref[...] = mn
        vo_ref[...] = vn
        d_ref[...] = -ADAM_LR * ((mn / bc1) / (jnp.sqrt(vn / bc2) + ADAM_EPS) + ADAM_WD * w_ref[...])

    spec = pl.BlockSpec((tr, c), lambda i: (i, 0))
    shape = jax.ShapeDtypeStruct((r, c), F32)
    return pl.pallas_call(
        body, name=name, grid=(r // tr,),
        in_specs=[spec] * 4, out_specs=(spec,) * 4, out_shape=(shape,) * 4,
        compiler_params=_params("parallel"),
    )(w, g, m, v)


def _me():
    return lax.axis_index("x"), lax.axis_index("y"), lax.axis_index("c")


def _other_chips(x, y):
    return [(1 - x, y), (x, 1 - y), (1 - x, 1 - y)]


def _remote(src, dst, send, recv, k, to):
    return pltpu.make_async_remote_copy(src_ref=src, dst_ref=dst, send_sem=send.at[k], recv_sem=recv.at[k],
                                        device_id=to, device_id_type=MESH)


def _run_comm(comm, *, name):
    def body(*refs):
        ni, no = len(comm.ins), len(comm.out_shape)
        ins, outs, sems = refs[:ni], refs[ni:ni + no], refs[ni + no:]
        comm.start(ins, outs, sems)
        comm.finish(ins, outs, sems)

    return pl.pallas_call(
        body, name=name,
        in_specs=[ANY] * len(comm.ins), out_specs=[ANY] * len(comm.out_shape),
        out_shape=list(comm.out_shape), scratch_shapes=list(comm.sems),
        input_output_aliases={i: i for i in range(len(comm.ins))} if comm.aliased else {},
    )(*comm.ins)


def _half(ref, chip, hc):
    hr = ref.shape[1] // 2
    return ref.at[chip, pl.ds(hc * hr, hr), :]


def _gather_comm(layer, shards):
    nw = len(shards)

    def sent(ins, outs, sems):
        send, recv, local = sems
        x, y, c = _me()
        me_chip = 2 * x + y
        mine = [pltpu.make_async_copy(ins[w].at[layer], outs[w].at[me_chip], local.at[w]) for w in range(nw)]
        first = []
        for w in range(nw):
            hr = ins[w].shape[1] // 2
            src = ins[w].at[layer, pl.ds(c * hr, hr), :]
            for j, chip in enumerate(_other_chips(x, y)):
                first.append(_remote(src, _half(outs[w], me_chip, c), send, recv, 3 * w + j, (*chip, c)))
        return mine, first

    def start(ins, outs, sems):
        mine, first = sent(ins, outs, sems)
        for cp in mine + first:
            cp.start()

    def finish(ins, outs, sems):
        send, recv, _ = sems
        mine, first = sent(ins, outs, sems)
        x, y, c = _me()
        for w in range(nw):
            for j, chip in enumerate(_other_chips(x, y)):
                landed = _half(outs[w], 2 * chip[0] + chip[1], c)
                _remote(landed, landed, send, recv, 3 * w + j, (*chip, c)).wait_recv()
        for cp in first:
            cp.wait_send()
        for cp in mine:
            cp.wait()

    return _Comm(list(shards), [jax.ShapeDtypeStruct((N_CHIPS,) + s.shape[1:], BF16) for s in shards],
                 [pltpu.SemaphoreType.DMA((3 * nw,)), pltpu.SemaphoreType.DMA((3 * nw,)),
                  pltpu.SemaphoreType.DMA((nw,))], start, finish)


def _forward_comm(gathered):
    nw = len(gathered)

    def sent(outs, sems):
        send, recv = sems
        x, y, c = _me()
        cps = []
        for w in range(nw):
            for j, chip in enumerate(_other_chips(x, y)):
                landed = _half(outs[w], 2 * chip[0] + chip[1], c)
                cps.append(_remote(landed, landed, send, recv, 3 * w + j, (x, y, 1 - c)))
        return cps

    def start(ins, outs, sems):
        for cp in sent(outs, sems):
            cp.start()

    def finish(ins, outs, sems):
        send, recv = sems
        x, y, c = _me()
        for w in range(nw):
            for j, chip in enumerate(_other_chips(x, y)):
                theirs = _half(outs[w], 2 * chip[0] + chip[1], 1 - c)
                _remote(theirs, theirs, send, recv, 3 * w + j, (x, y, 1 - c)).wait_recv()
        for cp in sent(outs, sems):
            cp.wait_send()

    return _Comm(list(gathered), [jax.ShapeDtypeStruct(g.shape, BF16) for g in gathered],
                 [pltpu.SemaphoreType.DMA((3 * nw,)), pltpu.SemaphoreType.DMA((3 * nw,))], start, finish,
                 aliased=True)


def _pair_exchange(dws):
    nw = len(dws)

    def body(*refs):
        ins, outs = refs[:nw], refs[nw:2 * nw]
        send, recv = refs[2 * nw:]
        x, y, c = _me()
        cps = []
        for w in range(nw):
            hr = ins[w].shape[1] // 2
            src = ins[w].at[:, pl.ds((1 - c) * hr, hr), :]
            cps.append(_remote(src, outs[w], send, recv, w, (x, y, 1 - c)))
        for cp in cps:
            cp.start()
        for cp in cps:
            cp.wait()

    return pl.pallas_call(
        body, name="grad_pair_exchange",
        in_specs=[ANY] * nw, out_specs=[ANY] * nw,
        out_shape=[jax.ShapeDtypeStruct((d.shape[0], d.shape[1] // 2, d.shape[2]), BF16) for d in dws],
        scratch_shapes=[pltpu.SemaphoreType.DMA((nw,)), pltpu.SemaphoreType.DMA((nw,))],
    )(*dws)


def _scatter_comm(qs):
    nw = len(qs)

    def sent(ins, outs, sems):
        send, recv, local = sems
        x, y, c = _me()
        me_chip = 2 * x + y
        mine = [pltpu.make_async_copy(ins[w].at[me_chip], outs[w].at[me_chip], local.at[w]) for w in range(nw)]
        cps = []
        for w in range(nw):
            for j, chip in enumerate(_other_chips(x, y)):
                cps.append(_remote(ins[w].at[2 * chip[0] + chip[1]], outs[w].at[me_chip], send, recv,
                                   3 * w + j, (*chip, c)))
        return mine, cps

    def start(ins, outs, sems):
        mine, cps = sent(ins, outs, sems)
        for cp in mine + cps:
            cp.start()

    def finish(ins, outs, sems):
        send, recv, _ = sems
        mine, cps = sent(ins, outs, sems)
        x, y, c = _me()
        for w in range(nw):
            for j, chip in enumerate(_other_chips(x, y)):
                theirs = outs[w].at[2 * chip[0] + chip[1]]
                _remote(theirs, theirs, send, recv, 3 * w + j, (*chip, c)).wait_recv()
        for cp in cps:
            cp.wait_send()
        for cp in mine:
            cp.wait()

    return _Comm(list(qs), [jax.ShapeDtypeStruct(q.shape, BF16) for q in qs],
                 [pltpu.SemaphoreType.DMA((3 * nw,)), pltpu.SemaphoreType.DMA((3 * nw,)),
                  pltpu.SemaphoreType.DMA((nw,))], start, finish)


def _sibling_fill(layer, grads):
    nw = len(grads)

    def body(*refs):
        ins, outs = refs[:nw], refs[nw:2 * nw]
        send, recv = refs[2 * nw:]
        del ins
        x, y, c = _me()
        cps = []
        for w in range(nw):
            hr = outs[w].shape[1] // 2
            mine = outs[w].at[layer, pl.ds(c * hr, hr), :]
            cps.append(_remote(mine, mine, send, recv, w, (x, y, 1 - c)))
        for cp in cps:
            cp.start()
        for w in range(nw):
            hr = outs[w].shape[1] // 2
            theirs = outs[w].at[layer, pl.ds((1 - c) * hr, hr), :]
            _remote(theirs, theirs, send, recv, w, (x, y, 1 - c)).wait_recv()
        for cp in cps:
            cp.wait_send()

    return pl.pallas_call(
        body, name=f"grad_sibling_fill_{layer}",
        in_specs=[ANY] * nw, out_specs=[ANY] * nw,
        out_shape=[jax.ShapeDtypeStruct(g.shape, F32) for g in grads],
        input_output_aliases={w: w for w in range(nw)},
        scratch_shapes=[pltpu.SemaphoreType.DMA((nw,)), pltpu.SemaphoreType.DMA((nw,))],
    )(*grads)


def _all_devices(pack, *, reduce, name):
    r = pack.shape[0]

    def body(p_ref, o_ref, buf, send, recv):
        x, y, c = _me()
        me = 4 * x + 2 * y + c
        buf[me] = p_ref[...]
        flips = [(fx, fy, fc) for fx in (0, 1) for fy in (0, 1) for fc in (0, 1)][1:]
        peers = [(x ^ fx, y ^ fy, c ^ fc) for fx, fy, fc in flips]
        cps = [_remote(p_ref, buf.at[me], send, recv, k, peer) for k, peer in enumerate(peers)]
        for cp in cps:
            cp.start()
        for k, (px, py, pc) in enumerate(peers):
            theirs = buf.at[4 * px + 2 * py + pc]
            _remote(theirs, theirs, send, recv, k, (px, py, pc)).wait_recv()
        for cp in cps:
            cp.wait_send()
        if reduce:
            acc = buf[0]
            for d in range(1, N_DEV):
                acc = acc + buf[d]
            o_ref[...] = acc
        else:
            o_ref[...] = buf[...]

    vm = pl.BlockSpec(memory_space=pltpu.VMEM)
    return pl.pallas_call(
        body, name=name,
        in_specs=[vm], out_specs=vm,
        out_shape=jax.ShapeDtypeStruct((r, 128) if reduce else (N_DEV, r, 128), F32),
        scratch_shapes=[pltpu.VMEM((N_DEV, r, 128), F32), pltpu.SemaphoreType.DMA((N_DEV - 1,)),
                        pltpu.SemaphoreType.DMA((N_DEV - 1,))],
        compiler_params=pltpu.CompilerParams(vmem_limit_bytes=VMEM_LIMIT),
    )(pack)


def _to_rows(flat):
    v = flat.reshape(-1)
    rows = -(-v.shape[0] // 1024) * 8
    return jnp.pad(v, (0, rows * 128 - v.shape[0])).reshape(rows, 128)


def kernel(x, w_in, w_dw, b_dw, conv_ln_g, conv_ln_b, rpb, w_out, w_up, w_down, pre_mix_g, post_mix_g, pre_mlp_g, post_mlp_g, loss_target, m_w_in, m_w_dw, m_b_dw, m_conv_ln_g, m_conv_ln_b, m_rpb, m_w_out, m_w_up, m_w_down, m_pre_mix_g, m_post_mix_g, m_pre_mlp_g, m_post_mlp_g, v_w_in, v_w_dw, v_b_dw, v_conv_ln_g, v_conv_ln_b, v_rpb, v_w_out, v_w_up, v_w_down, v_pre_mix_g, v_post_mix_g, v_pre_mlp_g, v_post_mlp_g):
    depth = w_in.shape[0]
    t = x.shape[1]
    xl = x.reshape(t, D_MODEL)
    target = loss_target.reshape(t, D_MODEL)
    chip = 2 * lax.axis_index("x") + lax.axis_index("y")

    big = {"w_in": w_in, "w_out": w_out, "w_up": w_up, "w_down": w_down}
    big_names = list(big)
    shards = [_cast_bf16(big[n].reshape(-1, big[n].shape[-1]), name=f"cast_{n}").reshape(big[n].shape)
              for n in big_names]

    wdw_all = _all_devices(_to_rows(w_dw), reduce=False, name="gather_w_dw")[::2]
    wdw_all = wdw_all.reshape(N_CHIPS, -1)[:, :w_dw.size].reshape((N_CHIPS,) + w_dw.shape)
    wdw_full = jnp.moveaxis(wdw_all, 0, 2).reshape(depth, CONV_K, CONV_W)
    wdw_pad = jnp.pad(wdw_full, ((0, 0), (0, 32 - CONV_K), (0, 0)))

    vec = lambda a, l: a[l].reshape(1, -1)

    saved = []
    h = _rms_fwd(xl, vec(pre_mix_g, 0), name="rms_first")
    (g_in,) = _run_comm(_forward_comm(_run_comm(_gather_comm(0, shards[:1]), name="gather_w_in_first")),
                        name="forward_w_in_first")
    g_out = None
    for l in range(depth):
        more = l + 1 < depth
        bias = _rpb_expand(rpb[l])
        conv_args = (wdw_pad[l], vec(b_dw, l), vec(conv_ln_g, l), vec(conv_ln_b, l))
        if l == 0:
            proj, part = _mm_nn(h, g_in, out_dtype=BF16, name="proj", comm=_gather_comm(0, shards[1:2]))
            yc, uc, g_out = _conv_fwd(proj, *conv_args, comm=_forward_comm([part]))
        else:
            (proj,) = _mm_nn(h, g_in, out_dtype=BF16, name="proj")
            yc, uc = _conv_fwd(proj, *conv_args)
        g_out = g_out.reshape(1, D_MODEL, D_MODEL)
        ya, part = _natten_fwd(proj, bias, _gather_comm(l, shards[2:3]))
        cat = jnp.concatenate([yc, ya], axis=1)
        mix, g_up = _mm_nn(cat, g_out, out_dtype=F32, name="out_proj", comm=_forward_comm([part]))
        x1, h2 = _resid_rms(xl, mix, vec(post_mix_g, l), vec(pre_mlp_g, l), name="mix_resid")
        up, act, part = _mm_nn(h2, g_up, out_dtype=BF16, relu2=True, name="mlp_up",
                               comm=_gather_comm(l, shards[3:]))
        (g_down,) = _run_comm(_forward_comm([part]), name="forward_w_down")
        g_down = g_down.reshape(1, D_FF, D_MODEL)
        f, *parts = _mm_nn(act, g_down, out_dtype=F32, name="mlp_down",
                           comm=_gather_comm(l + 1, shards[:2]) if more else None)
        saved.append(dict(x=xl, h=h, proj=proj, uc=uc, cat=cat, mix=mix, x1=x1, h2=h2, up=up, act=act, f=f,
                          bias=bias, w=(g_in, g_out, g_up, g_down)))
        if more:
            xl, h, g_in, g_out = _resid_rms(x1, f, vec(post_mlp_g, l), vec(pre_mix_g, l + 1), name="mlp_resid",
                                            comm=_forward_comm(parts))
        else:
            (xl,) = _resid_rms(x1, f, vec(post_mlp_g, l), None, name="mlp_resid_last")

    dx, loss_blk = _loss_head(xl, target)
    loss = lax.psum(loss_blk[0, 0], ("x", "y", "c"))

    grads = {n: lax.empty(big[n].shape, F32) for n in big_names}
    small = [None] * depth

    def pair_reduce(n, dw):
        (other,) = _pair_exchange([dw])
        return _pair_sum(dw, other, name=f"pair_sum_{n}")

    def finish(n, layer, part):
        (grads[n],) = _sibling_fill(layer, [_chip_sum(part, grads[n], layer, name=f"chip_sum_{n}_{layer}")])

    q_in = None
    for l in reversed(range(depth)):
        sv = saved[l]
        g_in, g_out, g_up, g_down = sv["w"]
        df, dg_post_mlp = _rms_bwd(dx, sv["f"], vec(post_mlp_g, l), None, out_dtype=BF16, name="rms_bwd_mlp_post")
        dup, *part = _mm_nt(df, g_down, out_dtype=BF16, up=sv["up"], name="d_act",
                            comm=_scatter_comm([q_in]) if q_in is not None else None)
        if part:
            finish("w_in", l + 1, part[0])
        q_down = pair_reduce("w_down", _mm_tn(sv["act"], df, 1, name="dw_down").reshape(N_CHIPS, -1, D_MODEL))
        dh2, part = _mm_nt(dup, g_up, out_dtype=F32, name="d_h2", comm=_scatter_comm([q_down]))
        finish("w_down", l, part)
        q_up = pair_reduce("w_up", _mm_tn(sv["h2"], dup, N_CHIPS, name="dw_up"))
        dx1, dg_pre_mlp = _rms_bwd(dh2, sv["x1"], vec(pre_mlp_g, l), dx, out_dtype=F32, name="rms_bwd_mlp_pre")
        dmix, dg_post_mix = _rms_bwd(dx1, sv["mix"], vec(post_mix_g, l), None, out_dtype=BF16, name="rms_bwd_mix_post")
        (dcat,) = _mm_nt(dmix, g_out, out_dtype=F32, name="d_cat")
        q_out = pair_reduce("w_out", _mm_tn(sv["cat"], dmix, 1, name="dw_out").reshape(N_CHIPS, -1, D_MODEL))
        dq, dk, dv, dbias, part = _natten_bwd(sv["proj"], sv["bias"], dcat, _scatter_comm([q_up]))
        finish("w_up", l, part)
        dag, dwdw, dvec = _conv_bwd(sv["proj"], sv["uc"], dcat, wdw_pad[l], vec(conv_ln_g, l), vec(conv_ln_b, l))
        dproj = jnp.concatenate([dag, dq, dk, dv], axis=1)
        dh, part = _mm_nt(dproj, g_in, out_dtype=F32, name="d_h", comm=_scatter_comm([q_out]))
        finish("w_out", l, part)
        q_in = pair_reduce("w_in", _mm_tn(sv["h"], dproj, N_CHIPS, name="dw_in"))
        dx, dg_pre_mix = _rms_bwd(dh, sv["x"], vec(pre_mix_g, l), dx1, out_dtype=F32, name="rms_bwd_mix_pre")

        drpb = _rpb_grad(dbias)
        small[l] = jnp.concatenate([
            dvec[0], dvec[1], dvec[2], dg_pre_mix[0], dg_post_mix[0], dg_pre_mlp[0], dg_post_mlp[0],
            drpb.reshape(-1), dwdw[:CONV_K].reshape(-1)])

    finish("w_in", 0, _run_comm(_scatter_comm([q_in]), name="grad_scatter_w_in_last")[0])
    grads = [grads[n] for n in big_names]

    small_sum = _all_devices(_to_rows(jnp.stack(small)), reduce=True, name="reduce_small_grads")
    small_sum = small_sum.reshape(-1)[:depth * small[0].shape[0]].reshape(depth, -1)
    sizes = [CONV_W, CONV_W, CONV_W, D_MODEL, D_MODEL, D_MODEL, D_MODEL, rpb[0].size, CONV_K * CONV_W]
    offs = np.concatenate([[0], np.cumsum(sizes)])
    pieces = [small_sum[:, offs[i]:offs[i + 1]] for i in range(len(sizes))]
    g_small = {
        "b_dw": pieces[0], "conv_ln_g": pieces[1], "conv_ln_b": pieces[2],
        "pre_mix_g": pieces[3], "post_mix_g": pieces[4], "pre_mlp_g": pieces[5], "post_mlp_g": pieces[6],
        "rpb": pieces[7].reshape(rpb.shape),
        "w_dw": lax.dynamic_slice_in_dim(pieces[8].reshape(depth, CONV_K, CONV_W), chip * w_dw.shape[2],
                                         w_dw.shape[2], axis=2),
    }

    given = dict(w_in=(w_in, m_w_in, v_w_in), w_dw=(w_dw, m_w_dw, v_w_dw), b_dw=(b_dw, m_b_dw, v_b_dw),
                 conv_ln_g=(conv_ln_g, m_conv_ln_g, v_conv_ln_g), conv_ln_b=(conv_ln_b, m_conv_ln_b, v_conv_ln_b),
                 rpb=(rpb, m_rpb, v_rpb), w_out=(w_out, m_w_out, v_w_out), w_up=(w_up, m_w_up, v_w_up),
                 w_down=(w_down, m_w_down, v_w_down), pre_mix_g=(pre_mix_g, m_pre_mix_g, v_pre_mix_g),
                 post_mix_g=(post_mix_g, m_post_mix_g, v_post_mix_g), pre_mlp_g=(pre_mlp_g, m_pre_mlp_g, v_pre_mlp_g),
                 post_mlp_g=(post_mlp_g, m_post_mlp_g, v_post_mlp_g))
    results = {}
    for n, g in zip(big_names, grads):
        w, m, v = given[n]
        flat = lambda a: a.reshape(-1, a.shape[-1])
        outs = _adamw(flat(w), flat(g), flat(m), flat(v), name=f"adamw_{n}")
        results[n] = [o.reshape(w.shape) for o in outs]

    small_names = list(g_small)
    pack = lambda arrs: _to_rows(jnp.concatenate([a.reshape(-1) for a in arrs]))
    outs = _adamw(pack([given[n][0] for n in small_names]), pack([g_small[n] for n in small_names]),
                  pack([given[n][1] for n in small_names]), pack([given[n][2] for n in small_names]),
                  name="adamw_small")
    pos = 0
    for n in small_names:
        w = given[n][0]
        results[n] = [o.reshape(-1)[pos:pos + w.size].reshape(w.shape) for o in outs]
        pos += w.size

    order = ["w_in", "w_dw", "b_dw", "conv_ln_g", "conv_ln_b", "rpb", "w_out", "w_up", "w_down",
             "pre_mix_g", "post_mix_g", "pre_mlp_g", "post_mlp_g"]
    return (loss, dx.reshape(x.shape), *[results[n][0] for n in order], *[results[n][1] for n in order],
            *[results[n][2] for n in order], *[results[n][3] for n in order])
```

```python
import functools

import numpy as np
import jax
import jax.numpy as jnp
from jax import lax
from jax.experimental import pallas as pl
from jax.experimental.pallas import tpu as pltpu

F32 = jnp.float32
BF16 = jnp.bfloat16

D_MODEL = 2048
CONV_W = 1024
NA_W = 1024
N_HEADS = 16
HEAD_D = 64
GRID_W = 64
WIN_ROWS = 8
WIN_COLS = 16
CONV_K = 31
D_FF = 4 * D_MODEL
IN_COLS = 2 * CONV_W + 3 * NA_W
RMS_EPS = 1e-6
LN_EPS = 1e-5
NEG_INF = -1e30
N_CHIPS = 4
N_DEV = 8
HALO = 16
BAND = WIN_ROWS * GRID_W

ADAM_LR = 0.001
ADAM_B1 = 0.9
ADAM_B2 = 0.999
ADAM_EPS = 1e-08
ADAM_WD = 0.01
ADAM_STEP = 10

VMEM_LIMIT = 48 * 1024 * 1024
MESH = pl.DeviceIdType.MESH
ANY = pl.BlockSpec(memory_space=pl.ANY)


def _params(*sem):
    return pltpu.CompilerParams(dimension_semantics=sem, vmem_limit_bytes=VMEM_LIMIT)


class _Comm:
    def __init__(self, ins, out_shape, sems, start, finish, aliased=False):
        self.ins, self.out_shape, self.sems, self.start, self.finish = ins, out_shape, sems, start, finish
        self.aliased = aliased


def _call(body, *, name, grid, in_specs, out_specs, out_shape, scratch_shapes=(), sem, args, comm=None):
    in_specs, out_specs, out_shape = list(in_specs), list(out_specs), list(out_shape)
    scratch_shapes = list(scratch_shapes)
    if comm is None:
        return pl.pallas_call(body, name=name, grid=grid, in_specs=in_specs, out_specs=out_specs,
                              out_shape=out_shape, scratch_shapes=scratch_shapes,
                              compiler_params=_params(*sem))(*args)
    ni, no, ns = len(in_specs), len(out_specs), len(scratch_shapes)
    nci, nco = len(comm.ins), len(comm.out_shape)

    def full(*refs):
        ins, refs = refs[:ni], refs[ni:]
        cins, refs = refs[:nci], refs[nci:]
        outs, refs = refs[:no], refs[no:]
        couts, refs = refs[:nco], refs[nco:]
        scr, csems = refs[:ns], refs[ns:]
        ids = [pl.program_id(d) for d in range(len(grid))]
        first = functools.reduce(jnp.logical_and, [i == 0 for i in ids])
        last = functools.reduce(jnp.logical_and, [i == g - 1 for i, g in zip(ids, grid)])

        @pl.when(first)
        def _():
            comm.start(cins, couts, csems)

        body(*ins, *outs, *scr)

        @pl.when(last)
        def _():
            comm.finish(cins, couts, csems)

    return pl.pallas_call(
        full, name=name, grid=grid,
        in_specs=in_specs + [ANY] * nci, out_specs=out_specs + [ANY] * nco,
        out_shape=out_shape + list(comm.out_shape),
        scratch_shapes=scratch_shapes + list(comm.sems),
        input_output_aliases={ni + i: no + i for i in range(nci)} if comm.aliased else {},
        compiler_params=_params(*(["arbitrary"] * len(grid))),
    )(*args, *comm.ins)


K_TILES = (2048, 1280, 1024, 512)


def _tile(n, pref):
    for t in pref:
        if n % t == 0:
            return t
    return n


def _accumulate(acc, step, n_steps, finish, part):
    if n_steps == 1:
        finish(part())
        return

    @pl.when(step == 0)
    def _():
        acc[...] = part()

    @pl.when(jnp.logical_and(step > 0, step < n_steps - 1))
    def _():
        acc[...] += part()

    @pl.when(step == n_steps - 1)
    def _():
        finish(acc[...] + part())


def _mm_nn(a, w, *, out_dtype, relu2=False, name, comm=None):
    m, k = a.shape
    s, _, n = w.shape
    tm = _tile(m, (1024, 512, 256))
    tn = _tile(n, (1024, 1280, 512))
    tk = _tile(k, K_TILES)
    nps = n // tn
    nk = k // tk

    def body(a_ref, w_ref, *rest):
        outs, acc = rest[:-1], rest[-1]

        def finish(r):
            outs[0][...] = r.astype(outs[0].dtype)
            if relu2:
                p = jnp.maximum(r, 0.0)
                outs[1][...] = (p * p).astype(outs[1].dtype)

        _accumulate(acc, pl.program_id(2), nk, finish,
                    lambda: jnp.dot(a_ref[...], w_ref[...], preferred_element_type=F32))

    o_spec = pl.BlockSpec((tm, tn), lambda i, j, kk: (i, j))
    o_shape = jax.ShapeDtypeStruct((m, s * n), out_dtype)
    return _call(
        body, name=name,
        grid=(m // tm, s * nps, nk),
        in_specs=[pl.BlockSpec((tm, tk), lambda i, j, kk: (i, kk)),
                  pl.BlockSpec((None, tk, tn), lambda i, j, kk: (j // nps, kk, j % nps))],
        out_specs=[o_spec, o_spec] if relu2 else [o_spec],
        out_shape=[o_shape, o_shape] if relu2 else [o_shape],
        scratch_shapes=[pltpu.VMEM((tm, tn), F32)],
        sem=("parallel", "parallel", "arbitrary"), args=(a, w), comm=comm)


def _mm_nt(dy, w, *, out_dtype, up=None, name, comm=None):
    m = dy.shape[0]
    s, k, n = w.shape
    tm = _tile(m, (1024, 512, 256))
    tko = _tile(k, (1024, 512))
    tn = _tile(n, K_TILES)
    nps = n // tn
    nn = s * nps

    def body(dy_ref, w_ref, *rest):
        if up is None:
            o_ref, acc = rest
        else:
            up_ref, o_ref, acc = rest

        def finish(r):
            if up is not None:
                r = r * (2.0 * jnp.maximum(up_ref[...].astype(F32), 0.0))
            o_ref[...] = r.astype(o_ref.dtype)

        _accumulate(acc, pl.program_id(2), nn, finish,
                    lambda: lax.dot_general(dy_ref[...], w_ref[...], (((1,), (1,)), ((), ())),
                                            preferred_element_type=F32))

    in_specs = [pl.BlockSpec((tm, tn), lambda i, j, kk: (i, kk)),
                pl.BlockSpec((None, tko, tn), lambda i, j, kk: (kk // nps, j, kk % nps))]
    args = [dy, w]
    if up is not None:
        in_specs.append(pl.BlockSpec((tm, tko), lambda i, j, kk: (i, j)))
        args.append(up)
    return _call(
        body, name=name,
        grid=(m // tm, k // tko, nn),
        in_specs=in_specs,
        out_specs=[pl.BlockSpec((tm, tko), lambda i, j, kk: (i, j))],
        out_shape=[jax.ShapeDtypeStruct((m, k), out_dtype)],
        scratch_shapes=[pltpu.VMEM((tm, tko), F32)],
        sem=("parallel", "parallel", "arbitrary"), args=args, comm=comm)


def _mm_tn(a, dy, s, *, name):
    m, k = a.shape
    n = dy.shape[1] // s
    tk = _tile(k, (1024, 512))
    tn = _tile(n, (1024, 1280, 512))
    tm = _tile(m, K_TILES)
    nps = n // tn
    nm = m // tm

    def body(a_ref, dy_ref, o_ref, acc):
        def finish(r):
            o_ref[...] = r.astype(o_ref.dtype)

        _accumulate(acc, pl.program_id(2), nm, finish,
                    lambda: lax.dot_general(a_ref[...], dy_ref[...], (((0,), (0,)), ((), ())),
                                            preferred_element_type=F32))

    return pl.pallas_call(
        body, name=name,
        grid=(k // tk, s * nps, nm),
        in_specs=[pl.BlockSpec((tm, tk), lambda i, j, mm: (mm, i)),
                  pl.BlockSpec((tm, tn), lambda i, j, mm: (mm, j))],
        out_specs=pl.BlockSpec((None, tk, tn), lambda i, j, mm: (j // nps, i, j % nps)),
        out_shape=jax.ShapeDtypeStruct((s, k, n), BF16),
        scratch_shapes=[pltpu.VMEM((tk, tn), F32)],
        compiler_params=_params("parallel", "parallel", "arbitrary"),
    )(a, dy)


ROWS = 256


def _row_spec(d):
    return pl.BlockSpec((ROWS, d), lambda i: (i, 0))


def _vec_spec(d):
    return pl.BlockSpec((1, d), lambda i: (0, 0))


def _rstd(v):
    return lax.rsqrt(jnp.mean(v * v, axis=-1, keepdims=True) + RMS_EPS)


def _rms_fwd(x, g, *, name):
    t, d = x.shape

    def body(x_ref, g_ref, h_ref):
        xv = x_ref[...]
        h_ref[...] = ((xv * _rstd(xv)) * g_ref[...]).astype(BF16)

    return pl.pallas_call(
        body, name=name, grid=(t // ROWS,),
        in_specs=[_row_spec(d), _vec_spec(d)],
        out_specs=_row_spec(d),
        out_shape=jax.ShapeDtypeStruct((t, d), BF16),
        compiler_params=_params("parallel"),
    )(x, g)


def _resid_rms(res, y, g_post, g_next, *, name, comm=None):
    t, d = res.shape
    with_next = g_next is not None

    def body(res_ref, y_ref, gp_ref, *rest):
        yv = y_ref[...]
        xn = res_ref[...] + (yv * _rstd(yv)) * gp_ref[...]
        if with_next:
            gn_ref, xo_ref, h_ref = rest
            h_ref[...] = ((xn * _rstd(xn)) * gn_ref[...]).astype(BF16)
        else:
            (xo_ref,) = rest
        xo_ref[...] = xn

    in_specs = [_row_spec(d), _row_spec(d), _vec_spec(d)]
    args = [res, y, g_post]
    out_specs = [_row_spec(d)]
    out_shape = [jax.ShapeDtypeStruct((t, d), F32)]
    if with_next:
        in_specs.append(_vec_spec(d))
        args.append(g_next)
        out_specs.append(_row_spec(d))
        out_shape.append(jax.ShapeDtypeStruct((t, d), BF16))
    return _call(body, name=name, grid=(t // ROWS,), in_specs=in_specs, out_specs=out_specs, out_shape=out_shape,
                 sem=("parallel",), args=args, comm=comm)


def _rms_bwd(dy, xin, g, res, *, out_dtype, name):
    t, d = xin.shape
    with_res = res is not None

    def body(dy_ref, x_ref, g_ref, *rest):
        if with_res:
            res_ref, dx_ref, dg_ref = rest
        else:
            dx_ref, dg_ref = rest
        xv = x_ref[...]
        r = _rstd(xv)
        nrm = xv * r
        dyv = dy_ref[...]
        dn = dyv * g_ref[...]
        dx = r * (dn - nrm * jnp.mean(dn * nrm, axis=-1, keepdims=True))
        if with_res:
            dx = dx + res_ref[...]
        dx_ref[...] = dx.astype(dx_ref.dtype)

        @pl.when(pl.program_id(0) == 0)
        def _():
            dg_ref[...] = jnp.zeros_like(dg_ref)

        dg_ref[...] += jnp.sum(dyv * nrm, axis=0, keepdims=True)

    in_specs = [_row_spec(d), _row_spec(d), _vec_spec(d)]
    args = [dy, xin, g]
    if with_res:
        in_specs.append(_row_spec(d))
        args.append(res)
    return pl.pallas_call(
        body, name=name, grid=(t // ROWS,),
        in_specs=in_specs,
        out_specs=(_row_spec(d), _vec_spec(d)),
        out_shape=(jax.ShapeDtypeStruct((t, d), out_dtype), jax.ShapeDtypeStruct((1, d), F32)),
        compiler_params=_params("arbitrary"),
    )(*args)


def _loss_head(y, target):
    t, d = y.shape

    def body(y_ref, t_ref, dy_ref, loss_ref):
        diff = y_ref[...] - t_ref[...]
        dy_ref[...] = diff * (1.0 / d)

        @pl.when(pl.program_id(0) == 0)
        def _():
            loss_ref[...] = jnp.zeros_like(loss_ref)

        loss_ref[...] += jnp.sum(diff * diff) * (0.5 / d)

    return pl.pallas_call(
        body, name="loss_head", grid=(t // ROWS,),
        in_specs=[_row_spec(d), _row_spec(d)],
        out_specs=(_row_spec(d), pl.BlockSpec((8, 128), lambda i: (0, 0))),
        out_shape=(jax.ShapeDtypeStruct((t, d), F32), jax.ShapeDtypeStruct((8, 128), F32)),
        compiler_params=_params("arbitrary"),
    )(y, target)


def _halo_specs(t, col, width):
    rb = ROWS // HALO
    last = t // HALO - 1
    return [pl.BlockSpec((ROWS, width), lambda i: (i, col)),
            pl.BlockSpec((HALO, width), lambda i: (jnp.maximum(i * rb - 1, 0), col)),
            pl.BlockSpec((HALO, width), lambda i: (jnp.minimum((i + 1) * rb, last), col))]


def _glu(a_ref, g_ref):
    return a_ref[...].astype(F32) * jax.nn.sigmoid(g_ref[...].astype(F32))


def _fill_ext(ext, cur, prev, nxt, i, nblk):
    ext[pl.ds(HALO, ROWS), :] = cur
    ext[pl.ds(0, HALO), :] = jnp.where(i > 0, prev, 0.0)
    ext[pl.ds(HALO + ROWS, HALO), :] = jnp.where(i < nblk - 1, nxt, 0.0)


LANES = 128
SUBLANES = 8
EXT_ROWS = ROWS + 2 * HALO
EXT_SCRATCH = pltpu.VMEM((EXT_ROWS, CONV_W), F32)
SHIFT_SCRATCH = pltpu.VMEM((SUBLANES, EXT_ROWS - SUBLANES, CONV_W), F32)


def _shift_copies(ext, shifted):
    for s in range(SUBLANES):
        shifted[s] = ext[pl.ds(s, EXT_ROWS - SUBLANES), :]


def _window(shifted, offset, rows, lanes, start=0):
    row0 = start + offset - offset % SUBLANES
    if not isinstance(row0, int):
        row0 = pl.multiple_of(row0, SUBLANES)
    return shifted[offset % SUBLANES, pl.ds(row0, rows), lanes]


def _conv_fwd(proj, w_dw, b_dw, ln_g, ln_b, comm=None):
    t = proj.shape[0]
    nblk = t // ROWS
    pad = HALO - CONV_K // 2

    def body(ac, ap, an, gc, gp, gn, w_ref, b_ref, lg_ref, lb_ref, yc_ref, uc_ref, uext, ushift):
        i = pl.program_id(0)
        _fill_ext(uext, _glu(ac, gc), _glu(ap, gp), _glu(an, gn), i, nblk)
        _shift_copies(uext, ushift)

        def chunk(cc, carry):
            lanes = pl.ds(pl.multiple_of(cc * LANES, LANES), LANES)
            acc = jnp.broadcast_to(b_ref[:, lanes], (ROWS, LANES))
            for j in range(CONV_K):
                acc = acc + _window(ushift, j + pad, ROWS, lanes) * w_ref[pl.ds(j, 1), lanes]
            uc_ref[:, lanes] = acc
            return carry

        lax.fori_loop(0, CONV_W // LANES, chunk, 0)
        acc = uc_ref[...]
        mu = jnp.mean(acc, axis=-1, keepdims=True)
        xc = acc - mu
        var = jnp.mean(xc * xc, axis=-1, keepdims=True)
        yln = xc * lax.rsqrt(var + LN_EPS) * lg_ref[...] + lb_ref[...]
        yc_ref[...] = (yln * jax.nn.sigmoid(yln)).astype(BF16)

    vec = pl.BlockSpec((1, CONV_W), lambda i: (0, 0))
    return _call(
        body, name="conv_fwd", grid=(nblk,),
        in_specs=_halo_specs(t, 0, CONV_W) + _halo_specs(t, 1, CONV_W)
        + [pl.BlockSpec((32, CONV_W), lambda i: (0, 0)), vec, vec, vec],
        out_specs=[pl.BlockSpec((ROWS, CONV_W), lambda i: (i, 0)),
                   pl.BlockSpec((ROWS, CONV_W), lambda i: (i, 0))],
        out_shape=[jax.ShapeDtypeStruct((t, CONV_W), BF16), jax.ShapeDtypeStruct((t, CONV_W), F32)],
        scratch_shapes=[EXT_SCRATCH, SHIFT_SCRATCH],
        sem=("parallel",), args=(proj, proj, proj, proj, proj, proj, w_dw, b_dw, ln_g, ln_b), comm=comm)


def _conv_bwd(proj, uc, dcat, w_dw, ln_g, ln_b):
    t = proj.shape[0]
    nblk = t // ROWS
    pad = HALO - CONV_K // 2

    half = ROWS // 2

    def body(ac, ap, an, gc, gp, gn, uc_c, uc_p, uc_n, dy_c, dy_p, dy_n, w_ref, lg_ref, lb_ref,
             dag_ref, dw_ref, dvec_ref, uext, dext, ushift, dshift, dw_part):
        i = pl.program_id(0)
        _fill_ext(uext, _glu(ac, gc), _glu(ap, gp), _glu(an, gn), i, nblk)
        _shift_copies(uext, ushift)

        def ln_bwd(u_ref, d_ref):
            u = u_ref[...]
            mu = jnp.mean(u, axis=-1, keepdims=True)
            xc = u - mu
            rstd = lax.rsqrt(jnp.mean(xc * xc, axis=-1, keepdims=True) + LN_EPS)
            xhat = xc * rstd
            yln = xhat * lg_ref[...] + lb_ref[...]
            sg = jax.nn.sigmoid(yln)
            dyln = d_ref[...] * (sg * (1.0 + yln * (1.0 - sg)))
            dxh = dyln * lg_ref[...]
            du = rstd * (dxh - jnp.mean(dxh, axis=-1, keepdims=True)
                         - xhat * jnp.mean(dxh * xhat, axis=-1, keepdims=True))
            return du, dyln, xhat

        duc, dyln, xhat = ln_bwd(uc_c, dy_c)
        _fill_ext(dext, duc, ln_bwd(uc_p, dy_p)[0], ln_bwd(uc_n, dy_n)[0], i, nblk)
        _shift_copies(dext, dshift)

        @pl.when(i == 0)
        def _():
            dw_part[...] = jnp.zeros_like(dw_part)
            dvec_ref[...] = jnp.zeros_like(dvec_ref)

        dvec_ref[pl.ds(0, 1), :] += jnp.sum(duc, axis=0, keepdims=True)
        dvec_ref[pl.ds(1, 1), :] += jnp.sum(dyln * xhat, axis=0, keepdims=True)
        dvec_ref[pl.ds(2, 1), :] += jnp.sum(dyln, axis=0, keepdims=True)

        def input_grad(cc, carry):
            lanes = pl.ds(pl.multiple_of(cc * LANES, LANES), LANES)
            gate_lanes = pl.ds(pl.multiple_of(CONV_W + cc * LANES, LANES), LANES)
            for r0 in (0, half):
                du = jnp.zeros((half, LANES), F32)
                for j in range(CONV_K):
                    du = du + _window(dshift, 2 * HALO - pad - j, half, lanes, r0) * w_ref[pl.ds(j, 1), lanes]
                rows = pl.ds(r0, half)
                a = ac[rows, lanes].astype(F32)
                sg = jax.nn.sigmoid(gc[rows, lanes].astype(F32))
                dag_ref[rows, lanes] = (du * sg).astype(BF16)
                dag_ref[rows, gate_lanes] = (du * a * sg * (1.0 - sg)).astype(BF16)
            return carry

        def tap_grad(cc, carry):
            lanes = pl.ds(pl.multiple_of(cc * LANES, LANES), LANES)
            for r0 in (0, half):
                duc_c = dext[pl.ds(HALO + r0, half), lanes]
                parts = [jnp.sum((duc_c * _window(ushift, j + pad, half, lanes, r0))
                                 .reshape(half // SUBLANES, SUBLANES, LANES), axis=0) for j in range(CONV_K)]
                parts.append(jnp.zeros((SUBLANES, LANES), F32))
                dw_part[:, lanes] += jnp.concatenate(parts, axis=0)
            return carry

        lax.fori_loop(0, CONV_W // LANES, input_grad, 0)
        lax.fori_loop(0, CONV_W // LANES, tap_grad, 0)

        @pl.when(i == nblk - 1)
        def _():
            dw_ref[...] = jnp.sum(dw_part[...].reshape(32, SUBLANES, CONV_W), axis=1)

    vec = pl.BlockSpec((1, CONV_W), lambda i: (0, 0))
    return pl.pallas_call(
        body, name="conv_bwd", grid=(nblk,),
        in_specs=_halo_specs(t, 0, CONV_W) + _halo_specs(t, 1, CONV_W) + _halo_specs(t, 0, CONV_W)
        + _halo_specs(t, 0, CONV_W) + [pl.BlockSpec((32, CONV_W), lambda i: (0, 0)), vec, vec],
        out_specs=(pl.BlockSpec((ROWS, 2 * CONV_W), lambda i: (i, 0)),
                   pl.BlockSpec((32, CONV_W), lambda i: (0, 0)),
                   pl.BlockSpec((8, CONV_W), lambda i: (0, 0))),
        out_shape=(jax.ShapeDtypeStruct((t, 2 * CONV_W), BF16),
                   jax.ShapeDtypeStruct((32, CONV_W), F32),
                   jax.ShapeDtypeStruct((8, CONV_W), F32)),
        scratch_shapes=[EXT_SCRATCH, EXT_SCRATCH, SHIFT_SCRATCH, SHIFT_SCRATCH,
                        pltpu.VMEM((32 * SUBLANES, CONV_W), F32)],
        compiler_params=_params("arbitrary"),
    )(proj, proj, proj, proj, proj, proj, uc, uc, uc, dcat, dcat, dcat, w_dw, ln_g, ln_b)


Q_BLK, K_BLK, V_BLK = 16, 24, 32


def _head_masks():
    lane = lax.broadcasted_iota(jnp.int32, (1, 2 * HEAD_D), 1)
    return [lane < HEAD_D, lane >= HEAD_D]


def _band(r, rows):
    rs = jnp.clip(r - WIN_ROWS // 2, 0, rows - WIN_ROWS)
    return rs, r - rs, pl.multiple_of(r * GRID_W, GRID_W), pl.multiple_of(rs * GRID_W, GRID_W)


ROWS_PER_STEP = 2
NT_DIMS = (((1,), (1,)), ((), ()))
TN_DIMS = (((0,), (0,)), ((), ()))
N_DR = 16
GE_SPEC = pl.BlockSpec((2, N_DR, GRID_W, GRID_W), lambda h: (h, 0, 0, 0))
BIAS_SCRATCH = pltpu.VMEM((WIN_ROWS, 2 * GRID_W, BAND), F32)


def _stack_heads(block, masks):
    z = jnp.zeros_like(block)
    return jnp.concatenate([jnp.where(masks[0], block, z), jnp.where(masks[1], block, z)], axis=0)


def _unstack_heads(both, masks):
    return jnp.where(masks[0], both[:GRID_W], both[GRID_W:])


def _softmax(s, bias):
    s = s * (HEAD_D ** -0.5) + bias
    e = jnp.exp(s - jnp.max(s, axis=-1, keepdims=True))
    return e * (1.0 / jnp.sum(e, axis=-1, keepdims=True))


def _fill_bias(ge_ref, bias):
    wq = lax.broadcasted_iota(jnp.int32, (GRID_W, BAND), 0)
    wk = jnp.bitwise_and(lax.broadcasted_iota(jnp.int32, (GRID_W, BAND), 1), GRID_W - 1)
    cs = jnp.clip(wq - WIN_COLS // 2, 0, GRID_W - WIN_COLS)
    inside = jnp.logical_and(wk >= cs, wk < cs + WIN_COLS)
    for hh in range(2):
        for off in range(WIN_ROWS):
            tile = jnp.concatenate([ge_ref[hh, WIN_ROWS - 1 - off + kr] for kr in range(WIN_ROWS)], axis=-1)
            bias[off, pl.ds(hh * GRID_W, GRID_W), :] = jnp.where(inside, tile, NEG_INF)


def _natten_fwd(proj, ge, comm=None):
    t = proj.shape[0]
    rows = t // GRID_W

    def body(q_ref, k_ref, v_ref, ge_ref, o_ref, bias):
        masks = _head_masks()
        _fill_bias(ge_ref, bias)

        def step(i, carry):
            bands = [_band(i * ROWS_PER_STEP + u, rows) for u in range(ROWS_PER_STEP)]
            vbs = [v_ref[pl.ds(k0, BAND), :] for _, _, _, k0 in bands]
            ss = [lax.dot_general(_stack_heads(q_ref[pl.ds(q0, GRID_W), :], masks), k_ref[pl.ds(k0, BAND), :],
                                  NT_DIMS, preferred_element_type=F32) for _, _, q0, k0 in bands]
            ps = [_softmax(s, bias[off]) for s, (_, off, _, _) in zip(ss, bands)]
            os = [jnp.dot(p.astype(BF16), vb, preferred_element_type=F32) for p, vb in zip(ps, vbs)]
            for o, (_, _, q0, _) in zip(os, bands):
                o_ref[pl.ds(q0, GRID_W), :] = _unstack_heads(o, masks).astype(BF16)
            return carry

        lax.fori_loop(0, rows // ROWS_PER_STEP, step, 0)

    col = lambda b: pl.BlockSpec((t, 2 * HEAD_D), lambda h: (0, b + h))
    return _call(
        body, name="natten_fwd", grid=(N_HEADS // 2,),
        in_specs=[col(Q_BLK), col(K_BLK), col(V_BLK), GE_SPEC],
        out_specs=[pl.BlockSpec((t, 2 * HEAD_D), lambda h: (0, h))],
        out_shape=[jax.ShapeDtypeStruct((t, NA_W), BF16)],
        scratch_shapes=[BIAS_SCRATCH],
        sem=("parallel",), args=(proj, proj, proj, ge), comm=comm)


def _natten_bwd(proj, ge, dcat, comm=None):
    t = proj.shape[0]
    rows = t // GRID_W
    scale = HEAD_D ** -0.5

    def body(q_ref, k_ref, v_ref, ge_ref, do_ref, dq_ref, dk_ref, dv_ref, dge_ref, bias, dbias, dk_acc, dv_acc):
        masks = _head_masks()
        _fill_bias(ge_ref, bias)
        dk_acc[...] = jnp.zeros_like(dk_acc)
        dv_acc[...] = jnp.zeros_like(dv_acc)
        dbias[...] = jnp.zeros_like(dbias)

        def step(i, carry):
            bands = [_band(i * ROWS_PER_STEP + u, rows) for u in range(ROWS_PER_STEP)]
            kbs = [k_ref[pl.ds(k0, BAND), :] for _, _, _, k0 in bands]
            q2s = [_stack_heads(q_ref[pl.ds(q0, GRID_W), :], masks) for _, _, q0, _ in bands]
            do2s = [_stack_heads(do_ref[pl.ds(q0, GRID_W), :].astype(BF16), masks) for _, _, q0, _ in bands]
            ss = [lax.dot_general(q2, kb, NT_DIMS, preferred_element_type=F32) for q2, kb in zip(q2s, kbs)]
            dps = [lax.dot_general(do2, v_ref[pl.ds(k0, BAND), :], NT_DIMS, preferred_element_type=F32)
                   for do2, (_, _, _, k0) in zip(do2s, bands)]
            ps = [_softmax(s, bias[off]) for s, (_, off, _, _) in zip(ss, bands)]
            dss =[p * (dp - jnp.sum(p * dp, axis=-1, keepdims=True)) for p, dp in zip(ps, dps)]
            dsbs = [ds.astype(BF16) for ds in dss]
            dqs = [jnp.dot(dsb, kb, preferred_element_type=F32) for dsb, kb in zip(dsbs, kbs)]
            dks = [lax.dot_general(dsb, q2, TN_DIMS, preferred_element_type=F32) for dsb, q2 in zip(dsbs, q2s)]
            dvs = [lax.dot_general(p.astype(BF16), do2, TN_DIMS, preferred_element_type=F32)
                   for p, do2 in zip(ps, do2s)]
            for ds, dq, dk, dv, (_, off, q0, k0) in zip(dss, dqs, dks, dvs, bands):
                dbias[off] += ds
                dq_ref[pl.ds(q0, GRID_W), :] = (_unstack_heads(dq, masks) * scale).astype(BF16)
                dk_acc[pl.ds(k0, BAND), :] += dk * scale
                dv_acc[pl.ds(k0, BAND), :] += dv
            return carry

        lax.fori_loop(0, rows // ROWS_PER_STEP, step, 0)
        dk_ref[...] = dk_acc[...].astype(BF16)
        dv_ref[...] = dv_acc[...].astype(BF16)
        for hh in range(2):
            for dr in range(N_DR):
                pieces = [dbias[off, pl.ds(hh * GRID_W, GRID_W), pl.ds((dr + off - WIN_ROWS + 1) * GRID_W, GRID_W)]
                          for off in range(WIN_ROWS) if 0 <= dr + off - WIN_ROWS + 1 < WIN_ROWS]
                dge_ref[hh, dr] = functools.reduce(jnp.add, pieces) if pieces else jnp.zeros((GRID_W, GRID_W), F32)

    col = lambda b: pl.BlockSpec((t, 2 * HEAD_D), lambda h: (0, b + h))
    o_spec = pl.BlockSpec((t, 2 * HEAD_D), lambda h: (0, h))
    o_shape = jax.ShapeDtypeStruct((t, NA_W), BF16)
    return _call(
        body, name="natten_bwd", grid=(N_HEADS // 2,),
        in_specs=[col(Q_BLK), col(K_BLK), col(V_BLK), GE_SPEC, col(CONV_W // (2 * HEAD_D))],
        out_specs=[o_spec, o_spec, o_spec, GE_SPEC],
        out_shape=[o_shape, o_shape, o_shape, jax.ShapeDtypeStruct((N_HEADS, N_DR, GRID_W, GRID_W), F32)],
        scratch_shapes=[BIAS_SCRATCH, BIAS_SCRATCH, pltpu.VMEM((t, 2 * HEAD_D), F32),
                        pltpu.VMEM((t, 2 * HEAD_D), F32)],
        sem=("parallel",), args=(proj, proj, proj, ge, dcat), comm=comm)


def _column_offsets():
    c = np.arange(GRID_W)
    return (np.clip(c[None, :] - c[:, None], -(WIN_COLS - 1), WIN_COLS - 1) + (WIN_COLS - 1)).reshape(-1)


def _rpb_expand(rpb_l):
    n_dc = 32
    et = np.zeros((n_dc, GRID_W * GRID_W), np.float32)
    et[_column_offsets(), np.arange(GRID_W * GRID_W)] = 1.0
    rp = jnp.pad(rpb_l, ((0, 0), (0, N_DR - rpb_l.shape[1]), (0, n_dc - rpb_l.shape[2]))).reshape(N_HEADS * N_DR, n_dc)

    def body(r_ref, e_ref, o_ref):
        o_ref[...] = jnp.dot(r_ref[...], e_ref[...], preferred_element_type=F32, precision=lax.Precision.HIGHEST)

    vm = pl.BlockSpec(memory_space=pltpu.VMEM)
    ge = pl.pallas_call(
        body, name="rpb_expand", in_specs=[vm, vm], out_specs=vm,
        out_shape=jax.ShapeDtypeStruct((N_HEADS * N_DR, GRID_W * GRID_W), F32),
        compiler_params=pltpu.CompilerParams(vmem_limit_bytes=VMEM_LIMIT),
    )(rp, jnp.asarray(et))
    return ge.reshape(N_HEADS, N_DR, GRID_W, GRID_W)


def _rpb_grad(dge):
    e = np.zeros((GRID_W * GRID_W, 128), np.float32)
    e[np.arange(GRID_W * GRID_W), _column_offsets()] = 1.0

    def body(z_ref, e_ref, o_ref):
        o_ref[...] = jnp.dot(z_ref[...], e_ref[...], preferred_element_type=F32, precision=lax.Precision.HIGHEST)

    vm = pl.BlockSpec(memory_space=pltpu.VMEM)
    out = pl.pallas_call(
        body, name="rpb_grad", in_specs=[vm, vm], out_specs=vm,
        out_shape=jax.ShapeDtypeStruct((N_HEADS * N_DR, 128), F32),
        compiler_params=pltpu.CompilerParams(vmem_limit_bytes=VMEM_LIMIT),
    )(dge.reshape(N_HEADS * N_DR, GRID_W * GRID_W), jnp.asarray(e))
    return out.reshape(N_HEADS, N_DR, 128)[:, :2 * WIN_ROWS - 1, :2 * WIN_COLS - 1]


def _cast_bf16(w2d, *, name):
    r, c = w2d.shape
    tr = _tile(r, (512, 256))

    def body(w_ref, a_ref, b_ref):
        v = w_ref[...].astype(BF16)
        a_ref[...] = v
        b_ref[...] = v

    spec = pl.BlockSpec((tr, c), lambda i: (i, 0))
    shape = jax.ShapeDtypeStruct((r, c), BF16)
    return pl.pallas_call(
        body, name=name, grid=(r // tr,),
        in_specs=[spec], out_specs=(spec, spec), out_shape=(shape, shape),
        compiler_params=_params("parallel"),
    )(w2d)


def _core_index():
    return lax.axis_index("c").astype(jnp.int32).reshape(1)


def _pair_sum(dw, other, *, name):
    s, r, c = dw.shape
    h = r // 2
    tr = _tile(h, (256, 128))
    nb = h // tr

    def body(c_ref, a_ref, b_ref, o_ref):
        del c_ref
        o_ref[...] = (a_ref[...].astype(F32) + b_ref[...].astype(F32)).astype(BF16)

    return pl.pallas_call(
        body, name=name,
        grid_spec=pltpu.PrefetchScalarGridSpec(
            num_scalar_prefetch=1, grid=(s, nb),
            in_specs=[pl.BlockSpec((None, tr, c), lambda si, i, cr: (si, cr[0] * nb + i, 0)),
                      pl.BlockSpec((None, tr, c), lambda si, i, cr: (si, i, 0))],
            out_specs=pl.BlockSpec((None, tr, c), lambda si, i, cr: (si, i, 0))),
        out_shape=jax.ShapeDtypeStruct((s, h, c), BF16),
        compiler_params=_params("parallel", "parallel"),
    )(_core_index(), dw, other)


def _chip_sum(parts, grad, layer, *, name):
    s, h, c = parts.shape
    tr = _tile(h, (256, 128))
    nb = h // tr

    def body(c_ref, p_ref, g_in, o_ref):
        del c_ref, g_in
        acc = p_ref[0].astype(F32) + p_ref[1].astype(F32)
        acc = acc + p_ref[2].astype(F32)
        o_ref[...] = acc + p_ref[3].astype(F32)

    return pl.pallas_call(
        body, name=name,
        grid_spec=pltpu.PrefetchScalarGridSpec(
            num_scalar_prefetch=1, grid=(nb,),
            in_specs=[pl.BlockSpec((s, tr, c), lambda i, cr: (0, i, 0)), ANY],
            out_specs=pl.BlockSpec((None, tr, c), lambda i, cr: (layer, cr[0] * nb + i, 0))),
        out_shape=jax.ShapeDtypeStruct(grad.shape, F32),
        input_output_aliases={2: 0},
        compiler_params=_params("parallel"),
    )(_core_index(), parts, grad)


def _adamw(w, g, m, v, *, name):
    r, c = w.shape
    tr = _tile(r, (256, 128, 8))
    tr = tr if r % tr == 0 else r
    bc1 = 1.0 - ADAM_B1 ** ADAM_STEP
    bc2 = 1.0 - ADAM_B2 ** ADAM_STEP

    def body(w_ref, g_ref, m_ref, v_ref, go_ref, d_ref, mo_ref, vo_ref):
        gv = g_ref[...]
        mn = ADAM_B1 * m_ref[...] + (1.0 - ADAM_B1) * gv
        vn = ADAM_B2 * v_ref[...] + (1.0 - ADAM_B2) * (gv * gv)
        go_ref[...] = gv
        mo_ref[...] = mn
        vo_ref[...] = vn
        d_ref[...] = -ADAM_LR * ((mn / bc1) / (jnp.sqrt(vn / bc2) + ADAM_EPS) + ADAM_WD * w_ref[...])

    spec = pl.BlockSpec((tr, c), lambda i: (i, 0))
    shape = jax.ShapeDtypeStruct((r, c), F32)
    return pl.pallas_call(
        body, name=name, grid=(r // tr,),
        in_specs=[spec] * 4, out_specs=(spec,) * 4, out_shape=(shape,) * 4,
        compiler_params=_params("parallel"),
    )(w, g, m, v)


def _me():
    return lax.axis_index("x"), lax.axis_index("y"), lax.axis_index("c")


def _other_chips(x, y):
    return [(1 - x, y), (x, 1 - y), (1 - x, 1 - y)]


def _remote(src, dst, send, recv, k, to):
    return pltpu.make_async_remote_copy(src_ref=src, dst_ref=dst, send_sem=send.at[k], recv_sem=recv.at[k],
                                        device_id=to, device_id_type=MESH)


def _run_comm(comm, *, name):
    def body(*refs):
        ni, no = len(comm.ins), len(comm.out_shape)
        ins, outs, sems = refs[:ni], refs[ni:ni + no], refs[ni + no:]
        comm.start(ins, outs, sems)
        comm.finish(ins, outs, sems)

    return pl.pallas_call(
        body, name=name,
        in_specs=[ANY] * len(comm.ins), out_specs=[ANY] * len(comm.out_shape),
        out_shape=list(comm.out_shape), scratch_shapes=list(comm.sems),
        input_output_aliases={i: i for i in range(len(comm.ins))} if comm.aliased else {},
    )(*comm.ins)


def _half(ref, chip, hc):
    hr = ref.shape[1] // 2
    return ref.at[chip, pl.ds(hc * hr, hr), :]


def _gather_comm(layer, shards):
    nw = len(shards)

    def sent(ins, outs, sems):
        send, recv, _ = sems
        x, y, c = _me()
        me_chip = 2 * x + y
        first = []
        for w in range(nw):
            hr = outs[w].shape[1] // 2
            rows = pl.ds(c * hr, hr)
            srcs = [ins[2 * w].at[layer, rows, :], ins[2 * w + 1].at[layer, rows, :], _half(outs[w], me_chip, c)]
            for j, chip in enumerate(_other_chips(x, y)):
                first.append(_remote(srcs[j], _half(outs[w], me_chip, c), send, recv, 3 * w + j, (*chip, c)))
        return first

    def start(ins, outs, sems):
        x, y, _ = _me()
        mine = [pltpu.make_async_copy(ins[2 * w].at[layer], outs[w].at[2 * x + y], sems[2].at[w]) for w in range(nw)]
        for cp in mine:
            cp.start()
        for cp in mine:
            cp.wait()
        for cp in sent(ins, outs, sems):
            cp.start()

    def finish(ins, outs, sems):
        send, recv, _ = sems
        first = sent(ins, outs, sems)
        x, y, c = _me()
        for w in range(nw):
            for j, chip in enumerate(_other_chips(x, y)):
                landed = _half(outs[w], 2 * chip[0] + chip[1], c)
                _remote(landed, landed, send, recv, 3 * w + j, (*chip, c)).wait_recv()
        for cp in first:
            cp.wait_send()

    flat = [s for pair in shards for s in pair]
    shards = [pair[0] for pair in shards]
    return _Comm(flat, [jax.ShapeDtypeStruct((N_CHIPS,) + s.shape[1:], BF16) for s in shards],
                 [pltpu.SemaphoreType.DMA((3 * nw,)), pltpu.SemaphoreType.DMA((3 * nw,)),
                  pltpu.SemaphoreType.DMA((nw,))], start, finish)


def _forward_comm(gathered):
    nw = len(gathered)

    def sent(outs, sems):
        send, recv = sems
        x, y, c = _me()
        cps = []
        for w in range(nw):
            for j, chip in enumerate(_other_chips(x, y)):
                landed = _half(outs[w], 2 * chip[0] + chip[1], c)
                cps.append(_remote(landed, landed, send, recv, 3 * w + j, (x, y, 1 - c)))
        return cps

    def start(ins, outs, sems):
        for cp in sent(outs, sems):
            cp.start()

    def finish(ins, outs, sems):
        send, recv = sems
        x, y, c = _me()
        for w in range(nw):
            for j, chip in enumerate(_other_chips(x, y)):
                theirs = _half(outs[w], 2 * chip[0] + chip[1], 1 - c)
                _remote(theirs, theirs, send, recv, 3 * w + j, (x, y, 1 - c)).wait_recv()
        for cp in sent(outs, sems):
            cp.wait_send()

    return _Comm(list(gathered), [jax.ShapeDtypeStruct(g.shape, BF16) for g in gathered],
                 [pltpu.SemaphoreType.DMA((3 * nw,)), pltpu.SemaphoreType.DMA((3 * nw,))], start, finish,
                 aliased=True)


def _pair_exchange(dws):
    nw = len(dws)

    def body(*refs):
        ins, outs = refs[:nw], refs[nw:2 * nw]
        send, recv = refs[2 * nw:]
        x, y, c = _me()
        cps = []
        for w in range(nw):
            hr = ins[w].shape[1] // 2
            src = ins[w].at[:, pl.ds((1 - c) * hr, hr), :]
            cps.append(_remote(src, outs[w], send, recv, w, (x, y, 1 - c)))
        for cp in cps:
            cp.start()
        for cp in cps:
            cp.wait()

    return pl.pallas_call(
        body, name="grad_pair_exchange",
        in_specs=[ANY] * nw, out_specs=[ANY] * nw,
        out_shape=[jax.ShapeDtypeStruct((d.shape[0], d.shape[1] // 2, d.shape[2]), BF16) for d in dws],
        scratch_shapes=[pltpu.SemaphoreType.DMA((nw,)), pltpu.SemaphoreType.DMA((nw,))],
    )(*dws)


def _scatter_comm(qs):
    nw = len(qs)

    def sent(ins, outs, sems):
        send, recv, local = sems
        x, y, c = _me()
        me_chip = 2 * x + y
        mine = [pltpu.make_async_copy(ins[w].at[me_chip], outs[w].at[me_chip], local.at[w]) for w in range(nw)]
        cps = []
        for w in range(nw):
            for j, chip in enumerate(_other_chips(x, y)):
                cps.append(_remote(ins[w].at[2 * chip[0] + chip[1]], outs[w].at[me_chip], send, recv,
                                   3 * w + j, (*chip, c)))
        return mine, cps

    def start(ins, outs, sems):
        mine, cps = sent(ins, outs, sems)
        for cp in mine + cps:
            cp.start()

    def finish(ins, outs, sems):
        send, recv, _ = sems
        mine, cps = sent(ins, outs, sems)
        x, y, c = _me()
        for w in range(nw):
            for j, chip in enumerate(_other_chips(x, y)):
                theirs = outs[w].at[2 * chip[0] + chip[1]]
                _remote(theirs, theirs, send, recv, 3 * w + j, (*chip, c)).wait_recv()
        for cp in cps:
            cp.wait_send()
        for cp in mine:
            cp.wait()

    return _Comm(list(qs), [jax.ShapeDtypeStruct(q.shape, BF16) for q in qs],
                 [pltpu.SemaphoreType.DMA((3 * nw,)), pltpu.SemaphoreType.DMA((3 * nw,)),
                  pltpu.SemaphoreType.DMA((nw,))], start, finish)


def _sibling_fill(layer, grads):
    nw = len(grads)

    def body(*refs):
        ins, outs = refs[:nw], refs[nw:2 * nw]
        send, recv = refs[2 * nw:]
        del ins
        x, y, c = _me()
        cps = []
        for w in range(nw):
            hr = outs[w].shape[1] // 2
            mine = outs[w].at[layer, pl.ds(c * hr, hr), :]
            cps.append(_remote(mine, mine, send, recv, w, (x, y, 1 - c)))
        for cp in cps:
            cp.start()
        for w in range(nw):
            hr = outs[w].shape[1] // 2
            theirs = outs[w].at[layer, pl.ds((1 - c) * hr, hr), :]
            _remote(theirs, theirs, send, recv, w, (x, y, 1 - c)).wait_recv()
        for cp in cps:
            cp.wait_send()

    return pl.pallas_call(
        body, name=f"grad_sibling_fill_{layer}",
        in_specs=[ANY] * nw, out_specs=[ANY] * nw,
        out_shape=[jax.ShapeDtypeStruct(g.shape, F32) for g in grads],
        input_output_aliases={w: w for w in range(nw)},
        scratch_shapes=[pltpu.SemaphoreType.DMA((nw,)), pltpu.SemaphoreType.DMA((nw,))],
    )(*grads)


def _all_devices(pack, *, reduce, name):
    r = pack.shape[0]

    def body(p_ref, o_ref, buf, send, recv):
        x, y, c = _me()
        me = 4 * x + 2 * y + c
        buf[me] = p_ref[...]
        flips = [(fx, fy, fc) for fx in (0, 1) for fy in (0, 1) for fc in (0, 1)][1:]
        peers = [(x ^ fx, y ^ fy, c ^ fc) for fx, fy, fc in flips]
        cps = [_remote(p_ref, buf.at[me], send, recv, k, peer) for k, peer in enumerate(peers)]
        for cp in cps:
            cp.start()
        for k, (px, py, pc) in enumerate(peers):
            theirs = buf.at[4 * px + 2 * py + pc]
            _remote(theirs, theirs, send, recv, k, (px, py, pc)).wait_recv()
        for cp in cps:
            cp.wait_send()
        if reduce:
            acc = buf[0]
            for d in range(1, N_DEV):
                acc = acc + buf[d]
            o_ref[...] = acc
        else:
            o_ref[...] = buf[...]

    vm = pl.BlockSpec(memory_space=pltpu.VMEM)
    return pl.pallas_call(
        body, name=name,
        in_specs=[vm], out_specs=vm,
        out_shape=jax.ShapeDtypeStruct((r, 128) if reduce else (N_DEV, r, 128), F32),
        scratch_shapes=[pltpu.VMEM((N_DEV, r, 128), F32), pltpu.SemaphoreType.DMA((N_DEV - 1,)),
                        pltpu.SemaphoreType.DMA((N_DEV - 1,))],
        compiler_params=pltpu.CompilerParams(vmem_limit_bytes=VMEM_LIMIT),
    )(pack)


def _to_rows(flat):
    v = flat.reshape(-1)
    rows = -(-v.shape[0] // 1024) * 8
    return jnp.pad(v, (0, rows * 128 - v.shape[0])).reshape(rows, 128)


def kernel(x, w_in, w_dw, b_dw, conv_ln_g, conv_ln_b, rpb, w_out, w_up, w_down, pre_mix_g, post_mix_g, pre_mlp_g, post_mlp_g, loss_target, m_w_in, m_w_dw, m_b_dw, m_conv_ln_g, m_conv_ln_b, m_rpb, m_w_out, m_w_up, m_w_down, m_pre_mix_g, m_post_mix_g, m_pre_mlp_g, m_post_mlp_g, v_w_in, v_w_dw, v_b_dw, v_conv_ln_g, v_conv_ln_b, v_rpb, v_w_out, v_w_up, v_w_down, v_pre_mix_g, v_post_mix_g, v_pre_mlp_g, v_post_mlp_g):
    depth = w_in.shape[0]
    t = x.shape[1]
    xl = x.reshape(t, D_MODEL)
    target = loss_target.reshape(t, D_MODEL)
    chip = 2 * lax.axis_index("x") + lax.axis_index("y")

    big = {"w_in": w_in, "w_out": w_out, "w_up": w_up, "w_down": w_down}
    big_names = list(big)
    shards = [tuple(s.reshape(big[n].shape)
                    for s in _cast_bf16(big[n].reshape(-1, big[n].shape[-1]), name=f"cast_{n}"))
              for n in big_names]

    wdw_all = _all_devices(_to_rows(w_dw), reduce=False, name="gather_w_dw")[::2]
    wdw_all = wdw_all.reshape(N_CHIPS, -1)[:, :w_dw.size].reshape((N_CHIPS,) + w_dw.shape)
    wdw_full = jnp.moveaxis(wdw_all, 0, 2).reshape(depth, CONV_K, CONV_W)
    wdw_pad = jnp.pad(wdw_full, ((0, 0), (0, 32 - CONV_K), (0, 0)))

    vec = lambda a, l: a[l].reshape(1, -1)

    saved = []
    h = _rms_fwd(xl, vec(pre_mix_g, 0), name="rms_first")
    (g_in,) = _run_comm(_forward_comm(_run_comm(_gather_comm(0, shards[:1]), name="gather_w_in_first")),
                        name="forward_w_in_first")
    g_out = None
    for l in range(depth):
        more = l + 1 < depth
        bias = _rpb_expand(rpb[l])
        conv_args = (wdw_pad[l], vec(b_dw, l), vec(conv_ln_g, l), vec(conv_ln_b, l))
        if l == 0:
            proj, part = _mm_nn(h, g_in, out_dtype=BF16, name="proj", comm=_gather_comm(0, shards[1:2]))
            yc, uc, g_out = _conv_fwd(proj, *conv_args, comm=_forward_comm([part]))
        else:
            (proj,) = _mm_nn(h, g_in, out_dtype=BF16, name="proj")
            yc, uc = _conv_fwd(proj, *conv_args)
        g_out = g_out.reshape(1, D_MODEL, D_MODEL)
        ya, part = _natten_fwd(proj, bias, _gather_comm(l, shards[2:3]))
        cat = jnp.concatenate([yc, ya], axis=1)
        mix, g_up = _mm_nn(cat, g_out, out_dtype=F32, name="out_proj", comm=_forward_comm([part]))
        x1, h2 = _resid_rms(xl, mix, vec(post_mix_g, l), vec(pre_mlp_g, l), name="mix_resid")
        up, act, part = _mm_nn(h2, g_up, out_dtype=BF16, relu2=True, name="mlp_up",
                               comm=_gather_comm(l, shards[3:]))
        (g_down,) = _run_comm(_forward_comm([part]), name="forward_w_down")
        g_down = g_down.reshape(1, D_FF, D_MODEL)
        f, *parts = _mm_nn(act, g_down, out_dtype=F32, name="mlp_down",
                           comm=_gather_comm(l + 1, shards[:2]) if more else None)
        saved.append(dict(x=xl, h=h, proj=proj, uc=uc, cat=cat, mix=mix, x1=x1, h2=h2, up=up, act=act, f=f,
                          bias=bias, w=(g_in, g_out, g_up, g_down)))
        if more:
            xl, h, g_in, g_out = _resid_rms(x1, f, vec(post_mlp_g, l), vec(pre_mix_g, l + 1), name="mlp_resid",
                                            comm=_forward_comm(parts))
        else:
            (xl,) = _resid_rms(x1, f, vec(post_mlp_g, l), None, name="mlp_resid_last")

    dx, loss_blk = _loss_head(xl, target)
    loss = lax.psum(loss_blk[0, 0], ("x", "y", "c"))

    grads = {n: lax.empty(big[n].shape, F32) for n in big_names}
    small = [None] * depth

    def pair_reduce(n, dw):
        (other,) = _pair_exchange([dw])
        return _pair_sum(dw, other, name=f"pair_sum_{n}")

    def finish(n, layer, part):
        (grads[n],) = _sibling_fill(layer, [_chip_sum(part, grads[n], layer, name=f"chip_sum_{n}_{layer}")])

    q_in = None
    for l in reversed(range(depth)):
        sv = saved[l]
        g_in, g_out, g_up, g_down = sv["w"]
        df, dg_post_mlp = _rms_bwd(dx, sv["f"], vec(post_mlp_g, l), None, out_dtype=BF16, name="rms_bwd_mlp_post")
        dup, *part = _mm_nt(df, g_down, out_dtype=BF16, up=sv["up"], name="d_act",
                            comm=_scatter_comm([q_in]) if q_in is not None else None)
        if part:
            finish("w_in", l + 1, part[0])
        q_down = pair_reduce("w_down", _mm_tn(sv["act"], df, 1, name="dw_down").reshape(N_CHIPS, -1, D_MODEL))
        dh2, part = _mm_nt(dup, g_up, out_dtype=F32, name="d_h2", comm=_scatter_comm([q_down]))
        finish("w_down", l, part)
        q_up = pair_reduce("w_up", _mm_tn(sv["h2"], dup, N_CHIPS, name="dw_up"))
        dx1, dg_pre_mlp = _rms_bwd(dh2, sv["x1"], vec(pre_mlp_g, l), dx, out_dtype=F32, name="rms_bwd_mlp_pre")
        dmix, dg_post_mix = _rms_bwd(dx1, sv["mix"], vec(post_mix_g, l), None, out_dtype=BF16, name="rms_bwd_mix_post")
        (dcat,) = _mm_nt(dmix, g_out, out_dtype=F32, name="d_cat")
        q_out = pair_reduce("w_out", _mm_tn(sv["cat"], dmix, 1, name="dw_out").reshape(N_CHIPS, -1, D_MODEL))
        dq, dk, dv, dbias, part = _natten_bwd(sv["proj"], sv["bias"], dcat, _scatter_comm([q_up]))
        finish("w_up", l, part)
        dag, dwdw, dvec = _conv_bwd(sv["proj"], sv["uc"], dcat, wdw_pad[l], vec(conv_ln_g, l), vec(conv_ln_b, l))
        dproj = jnp.concatenate([dag, dq, dk, dv], axis=1)
        dh, part = _mm_nt(dproj, g_in, out_dtype=F32, name="d_h", comm=_scatter_comm([q_out]))
        finish("w_out", l, part)
        q_in = pair_reduce("w_in", _mm_tn(sv["h"], dproj, N_CHIPS, name="dw_in"))
        dx, dg_pre_mix = _rms_bwd(dh, sv["x"], vec(pre_mix_g, l), dx1, out_dtype=F32, name="rms_bwd_mix_pre")

        drpb = _rpb_grad(dbias)
        small[l] = jnp.concatenate([
            dvec[0], dvec[1], dvec[2], dg_pre_mix[0], dg_post_mix[0], dg_pre_mlp[0], dg_post_mlp[0],
            drpb.reshape(-1), dwdw[:CONV_K].reshape(-1)])

    finish("w_in", 0, _run_comm(_scatter_comm([q_in]), name="grad_scatter_w_in_last")[0])
    grads = [grads[n] for n in big_names]

    small_sum = _all_devices(_to_rows(jnp.stack(small)), reduce=True, name="reduce_small_grads")
    small_sum = small_sum.reshape(-1)[:depth * small[0].shape[0]].reshape(depth, -1)
    sizes = [CONV_W, CONV_W, CONV_W, D_MODEL, D_MODEL, D_MODEL, D_MODEL, rpb[0].size, CONV_K * CONV_W]
    offs = np.concatenate([[0], np.cumsum(sizes)])
    pieces = [small_sum[:, offs[i]:offs[i + 1]] for i in range(len(sizes))]
    g_small = {
        "b_dw": pieces[0], "conv_ln_g": pieces[1], "conv_ln_b": pieces[2],
        "pre_mix_g": pieces[3], "post_mix_g": pieces[4], "pre_mlp_g": pieces[5], "post_mlp_g": pieces[6],
        "rpb": pieces[7].reshape(rpb.shape),
        "w_dw": lax.dynamic_slice_in_dim(pieces[8].reshape(depth, CONV_K, CONV_W), chip * w_dw.shape[2],
                                         w_dw.shape[2], axis=2),
    }

    given = dict(w_in=(w_in, m_w_in, v_w_in), w_dw=(w_dw, m_w_dw, v_w_dw), b_dw=(b_dw, m_b_dw, v_b_dw),
                 conv_ln_g=(conv_ln_g, m_conv_ln_g, v_conv_ln_g), conv_ln_b=(conv_ln_b, m_conv_ln_b, v_conv_ln_b),
                 rpb=(rpb, m_rpb, v_rpb), w_out=(w_out, m_w_out, v_w_out), w_up=(w_up, m_w_up, v_w_up),
                 w_down=(w_down, m_w_down, v_w_down), pre_mix_g=(pre_mix_g, m_pre_mix_g, v_pre_mix_g),
                 post_mix_g=(post_mix_g, m_post_mix_g, v_post_mix_g), pre_mlp_g=(pre_mlp_g, m_pre_mlp_g, v_pre_mlp_g),
                 post_mlp_g=(post_mlp_g, m_post_mlp_g, v_post_mlp_g))
    results = {}
    for n, g in zip(big_names, grads):
        w, m, v = given[n]
        flat = lambda a: a.reshape(-1, a.shape[-1])
        outs = _adamw(flat(w), flat(g), flat(m), flat(v), name=f"adamw_{n}")
        results[n] = [o.reshape(w.shape) for o in outs]

    small_names = list(g_small)
    pack = lambda arrs: _to_rows(jnp.concatenate([a.reshape(-1) for a in arrs]))
    outs = _adamw(pack([given[n][0] for n in small_names]), pack([g_small[n] for n in small_names]),
                  pack([given[n][1] for n in small_names]), pack([given[n][2] for n in small_names]),
                  name="adamw_small")
    pos = 0
    for n in small_names:
        w = given[n][0]
        results[n] = [o.reshape(-1)[pos:pos + w.size].reshape(w.shape) for o in outs]
        pos += w.size

    order = ["w_in", "w_dw", "b_dw", "conv_ln_g", "conv_ln_b", "rpb", "w_out", "w_up", "w_down",
             "pre_mix_g", "post_mix_g", "pre_mlp_g", "post_mlp_g"]
    return (loss, dx.reshape(x.shape), *[results[n][0] for n in order], *[results[n][1] for n in order],
            *[results[n][2] for n in order], *[results[n][3] for n in order])
```

```python
import functools

import numpy as np
import jax
import jax.numpy as jnp
from jax import lax
from jax.experimental import pallas as pl
from jax.experimental.pallas import tpu as pltpu

F32 = jnp.float32
BF16 = jnp.bfloat16

D_MODEL = 2048
CONV_W = 1024
NA_W = 1024
N_HEADS = 16
HEAD_D = 64
GRID_W = 64
WIN_ROWS = 8
WIN_COLS = 16
CONV_K = 31
D_FF = 4 * D_MODEL
IN_COLS = 2 * CONV_W + 3 * NA_W
RMS_EPS = 1e-6
LN_EPS = 1e-5
NEG_INF = -1e30
N_CHIPS = 4
N_DEV = 8
HALO = 16
BAND = WIN_ROWS * GRID_W

ADAM_LR = 0.001
ADAM_B1 = 0.9
ADAM_B2 = 0.999
ADAM_EPS = 1e-08
ADAM_WD = 0.01
ADAM_STEP = 10

VMEM_LIMIT = 48 * 1024 * 1024
MESH = pl.DeviceIdType.MESH
ANY = pl.BlockSpec(memory_space=pl.ANY)


def _params(*sem):
    return pltpu.CompilerParams(dimension_semantics=sem, vmem_limit_bytes=VMEM_LIMIT)


class _Comm:
    def __init__(self, ins, out_shape, sems, start, finish, aliased=False):
        self.ins, self.out_shape, self.sems, self.start, self.finish = ins, out_shape, sems, start, finish
        self.aliased = aliased


def _call(body, *, name, grid, in_specs, out_specs, out_shape, scratch_shapes=(), sem, args, comm=None):
    in_specs, out_specs, out_shape = list(in_specs), list(out_specs), list(out_shape)
    scratch_shapes = list(scratch_shapes)
    if comm is None:
        return pl.pallas_call(body, name=name, grid=grid, in_specs=in_specs, out_specs=out_specs,
                              out_shape=out_shape, scratch_shapes=scratch_shapes,
                              compiler_params=_params(*sem))(*args)
    ni, no, ns = len(in_specs), len(out_specs), len(scratch_shapes)
    nci, nco = len(comm.ins), len(comm.out_shape)

    def full(*refs):
        ins, refs = refs[:ni], refs[ni:]
        cins, refs = refs[:nci], refs[nci:]
        outs, refs = refs[:no], refs[no:]
        couts, refs = refs[:nco], refs[nco:]
        scr, csems = refs[:ns], refs[ns:]
        ids = [pl.program_id(d) for d in range(len(grid))]
        first = functools.reduce(jnp.logical_and, [i == 0 for i in ids])
        last = functools.reduce(jnp.logical_and, [i == g - 1 for i, g in zip(ids, grid)])

        @pl.when(first)
        def _():
            comm.start(cins, couts, csems)

        body(*ins, *outs, *scr)

        @pl.when(last)
        def _():
            comm.finish(cins, couts, csems)

    return pl.pallas_call(
        full, name=name, grid=grid,
        in_specs=in_specs + [ANY] * nci, out_specs=out_specs + [ANY] * nco,
        out_shape=out_shape + list(comm.out_shape),
        scratch_shapes=scratch_shapes + list(comm.sems),
        input_output_aliases={ni + i: no + i for i in range(nci)} if comm.aliased else {},
        compiler_params=_params(*(["arbitrary"] * len(grid))),
    )(*args, *comm.ins)


K_TILES = (2048, 1280, 1024, 512)


def _tile(n, pref):
    for t in pref:
        if n % t == 0:
            return t
    return n


def _accumulate(acc, step, n_steps, finish, part):
    if n_steps == 1:
        finish(part())
        return

    @pl.when(step == 0)
    def _():
        acc[...] = part()

    @pl.when(jnp.logical_and(step > 0, step < n_steps - 1))
    def _():
        acc[...] += part()

    @pl.when(step == n_steps - 1)
    def _():
        finish(acc[...] + part())


def _mm_nn(a, w, *, out_dtype, relu2=False, name, comm=None):
    m, k = a.shape
    s, _, n = w.shape
    tm = _tile(m, (1024, 512, 256))
    tn = _tile(n, (1024, 1280, 512))
    tk = _tile(k, K_TILES)
    nps = n // tn
    nk = k // tk

    def body(a_ref, w_ref, *rest):
        outs, acc = rest[:-1], rest[-1]

        def finish(r):
            outs[0][...] = r.astype(outs[0].dtype)
            if relu2:
                p = jnp.maximum(r, 0.0)
                outs[1][...] = (p * p).astype(outs[1].dtype)

        _accumulate(acc, pl.program_id(2), nk, finish,
                    lambda: jnp.dot(a_ref[...], w_ref[...], preferred_element_type=F32))

    o_spec = pl.BlockSpec((tm, tn), lambda i, j, kk: (i, j))
    o_shape = jax.ShapeDtypeStruct((m, s * n), out_dtype)
    return _call(
        body, name=name,
        grid=(m // tm, s * nps, nk),
        in_specs=[pl.BlockSpec((tm, tk), lambda i, j, kk: (i, kk)),
                  pl.BlockSpec((None, tk, tn), lambda i, j, kk: (j // nps, kk, j % nps))],
        out_specs=[o_spec, o_spec] if relu2 else [o_spec],
        out_shape=[o_shape, o_shape] if relu2 else [o_shape],
        scratch_shapes=[pltpu.VMEM((tm, tn), F32)],
        sem=("parallel", "parallel", "arbitrary"), args=(a, w), comm=comm)


def _mm_nt(dy, w, *, out_dtype, up=None, name, comm=None):
    m = dy.shape[0]
    s, k, n = w.shape
    tm = _tile(m, (1024, 512, 256))
    tko = _tile(k, (1024, 512))
    tn = _tile(n, K_TILES)
    nps = n // tn
    nn = s * nps

    def body(dy_ref, w_ref, *rest):
        if up is None:
            o_ref, acc = rest
        else:
            up_ref, o_ref, acc = rest

        def finish(r):
            if up is not None:
                r = r * (2.0 * jnp.maximum(up_ref[...].astype(F32), 0.0))
            o_ref[...] = r.astype(o_ref.dtype)

        _accumulate(acc, pl.program_id(2), nn, finish,
                    lambda: lax.dot_general(dy_ref[...], w_ref[...], (((1,), (1,)), ((), ())),
                                            preferred_element_type=F32))

    in_specs = [pl.BlockSpec((tm, tn), lambda i, j, kk: (i, kk)),
                pl.BlockSpec((None, tko, tn), lambda i, j, kk: (kk // nps, j, kk % nps))]
    args = [dy, w]
    if up is not None:
        in_specs.append(pl.BlockSpec((tm, tko), lambda i, j, kk: (i, j)))
        args.append(up)
    return _call(
        body, name=name,
        grid=(m // tm, k // tko, nn),
        in_specs=in_specs,
        out_specs=[pl.BlockSpec((tm, tko), lambda i, j, kk: (i, j))],
        out_shape=[jax.ShapeDtypeStruct((m, k), out_dtype)],
        scratch_shapes=[pltpu.VMEM((tm, tko), F32)],
        sem=("parallel", "parallel", "arbitrary"), args=args, comm=comm)


def _mm_tn(a, dy, s, *, name):
    m, k = a.shape
    n = dy.shape[1] // s
    tk = _tile(k, (1024, 512))
    tn = _tile(n, (1024, 1280, 512))
    tm = _tile(m, K_TILES)
    nps = n // tn
    nm = m // tm

    def body(a_ref, dy_ref, o_ref, acc):
        def finish(r):
            o_ref[...] = r.astype(o_ref.dtype)

        _accumulate(acc, pl.program_id(2), nm, finish,
                    lambda: lax.dot_general(a_ref[...], dy_ref[...], (((0,), (0,)), ((), ())),
                                            preferred_element_type=F32))

    return pl.pallas_call(
        body, name=name,
        grid=(k // tk, s * nps, nm),
        in_specs=[pl.BlockSpec((tm, tk), lambda i, j, mm: (mm, i)),
                  pl.BlockSpec((tm, tn), lambda i, j, mm: (mm, j))],
        out_specs=pl.BlockSpec((None, tk, tn), lambda i, j, mm: (j // nps, i, j % nps)),
        out_shape=jax.ShapeDtypeStruct((s, k, n), BF16),
        scratch_shapes=[pltpu.VMEM((tk, tn), F32)],
        compiler_params=_params("parallel", "parallel", "arbitrary"),
    )(a, dy)


ROWS = 256


def _row_spec(d):
    return pl.BlockSpec((ROWS, d), lambda i: (i, 0))


def _vec_spec(d):
    return pl.BlockSpec((1, d), lambda i: (0, 0))


def _rstd(v):
    return lax.rsqrt(jnp.mean(v * v, axis=-1, keepdims=True) + RMS_EPS)


def _rms_fwd(x, g, *, name):
    t, d = x.shape

    def body(x_ref, g_ref, h_ref):
        xv = x_ref[...]
        h_ref[...] = ((xv * _rstd(xv)) * g_ref[...]).astype(BF16)

    return pl.pallas_call(
        body, name=name, grid=(t // ROWS,),
        in_specs=[_row_spec(d), _vec_spec(d)],
        out_specs=_row_spec(d),
        out_shape=jax.ShapeDtypeStruct((t, d), BF16),
        compiler_params=_params("parallel"),
    )(x, g)


def _resid_rms(res, y, g_post, g_next, *, name, comm=None):
    t, d = res.shape
    with_next = g_next is not None

    def body(res_ref, y_ref, gp_ref, *rest):
        yv = y_ref[...]
        xn = res_ref[...] + (yv * _rstd(yv)) * gp_ref[...]
        if with_next:
            gn_ref, xo_ref, h_ref = rest
            h_ref[...] = ((xn * _rstd(xn)) * gn_ref[...]).astype(BF16)
        else:
            (xo_ref,) = rest
        xo_ref[...] = xn

    in_specs = [_row_spec(d), _row_spec(d), _vec_spec(d)]
    args = [res, y, g_post]
    out_specs = [_row_spec(d)]
    out_shape = [jax.ShapeDtypeStruct((t, d), F32)]
    if with_next:
        in_specs.append(_vec_spec(d))
        args.append(g_next)
        out_specs.append(_row_spec(d))
        out_shape.append(jax.ShapeDtypeStruct((t, d), BF16))
    return _call(body, name=name, grid=(t // ROWS,), in_specs=in_specs, out_specs=out_specs, out_shape=out_shape,
                 sem=("parallel",), args=args, comm=comm)


def _rms_bwd(dy, xin, g, res, *, out_dtype, name):
    t, d = xin.shape
    with_res = res is not None

    def body(dy_ref, x_ref, g_ref, *rest):
        if with_res:
            res_ref, dx_ref, dg_ref = rest
        else:
            dx_ref, dg_ref = rest
        xv = x_ref[...]
        r = _rstd(xv)
        nrm = xv * r
        dyv = dy_ref[...]
        dn = dyv * g_ref[...]
        dx = r * (dn - nrm * jnp.mean(dn * nrm, axis=-1, keepdims=True))
        if with_res:
            dx = dx + res_ref[...]
        dx_ref[...] = dx.astype(dx_ref.dtype)

        @pl.when(pl.program_id(0) == 0)
        def _():
            dg_ref[...] = jnp.zeros_like(dg_ref)

        dg_ref[...] += jnp.sum(dyv * nrm, axis=0, keepdims=True)

    in_specs = [_row_spec(d), _row_spec(d), _vec_spec(d)]
    args = [dy, xin, g]
    if with_res:
        in_specs.append(_row_spec(d))
        args.append(res)
    return pl.pallas_call(
        body, name=name, grid=(t // ROWS,),
        in_specs=in_specs,
        out_specs=(_row_spec(d), _vec_spec(d)),
        out_shape=(jax.ShapeDtypeStruct((t, d), out_dtype), jax.ShapeDtypeStruct((1, d), F32)),
        compiler_params=_params("arbitrary"),
    )(*args)


def _loss_head(y, target):
    t, d = y.shape

    def body(y_ref, t_ref, dy_ref, loss_ref):
        diff = y_ref[...] - t_ref[...]
        dy_ref[...] = diff * (1.0 / d)

        @pl.when(pl.program_id(0) == 0)
        def _():
            loss_ref[...] = jnp.zeros_like(loss_ref)

        loss_ref[...] += jnp.sum(diff * diff) * (0.5 / d)

    return pl.pallas_call(
        body, name="loss_head", grid=(t // ROWS,),
        in_specs=[_row_spec(d), _row_spec(d)],
        out_specs=(_row_spec(d), pl.BlockSpec((8, 128), lambda i: (0, 0))),
        out_shape=(jax.ShapeDtypeStruct((t, d), F32), jax.ShapeDtypeStruct((8, 128), F32)),
        compiler_params=_params("arbitrary"),
    )(y, target)


def _halo_specs(t, col, width):
    rb = ROWS // HALO
    last = t // HALO - 1
    return [pl.BlockSpec((ROWS, width), lambda i: (i, col)),
            pl.BlockSpec((HALO, width), lambda i: (jnp.maximum(i * rb - 1, 0), col)),
            pl.BlockSpec((HALO, width), lambda i: (jnp.minimum((i + 1) * rb, last), col))]


def _glu(a_ref, g_ref):
    return a_ref[...].astype(F32) * jax.nn.sigmoid(g_ref[...].astype(F32))


def _fill_ext(ext, cur, prev, nxt, i, nblk):
    ext[pl.ds(HALO, ROWS), :] = cur
    ext[pl.ds(0, HALO), :] = jnp.where(i > 0, prev, 0.0)
    ext[pl.ds(HALO + ROWS, HALO), :] = jnp.where(i < nblk - 1, nxt, 0.0)


LANES = 128
SUBLANES = 8
EXT_ROWS = ROWS + 2 * HALO
EXT_SCRATCH = pltpu.VMEM((EXT_ROWS, CONV_W), F32)
SHIFT_SCRATCH = pltpu.VMEM((SUBLANES, EXT_ROWS - SUBLANES, CONV_W), F32)


def _shift_copies(ext, shifted):
    for s in range(SUBLANES):
        shifted[s] = ext[pl.ds(s, EXT_ROWS - SUBLANES), :]


def _window(shifted, offset, rows, lanes, start=0):
    row0 = start + offset - offset % SUBLANES
    if not isinstance(row0, int):
        row0 = pl.multiple_of(row0, SUBLANES)
    return shifted[offset % SUBLANES, pl.ds(row0, rows), lanes]


def _conv_fwd(proj, w_dw, b_dw, ln_g, ln_b, comm=None):
    t = proj.shape[0]
    nblk = t // ROWS
    pad = HALO - CONV_K // 2

    def body(ac, ap, an, gc, gp, gn, w_ref, b_ref, lg_ref, lb_ref, yc_ref, uc_ref, uext, ushift):
        i = pl.program_id(0)
        _fill_ext(uext, _glu(ac, gc), _glu(ap, gp), _glu(an, gn), i, nblk)
        _shift_copies(uext, ushift)

        def chunk(cc, carry):
            lanes = pl.ds(pl.multiple_of(cc * LANES, LANES), LANES)
            acc = jnp.broadcast_to(b_ref[:, lanes], (ROWS, LANES))
            for j in range(CONV_K):
                acc = acc + _window(ushift, j + pad, ROWS, lanes) * w_ref[pl.ds(j, 1), lanes]
            uc_ref[:, lanes] = acc
            return carry

        lax.fori_loop(0, CONV_W // LANES, chunk, 0)
        acc = uc_ref[...]
        mu = jnp.mean(acc, axis=-1, keepdims=True)
        xc = acc - mu
        var = jnp.mean(xc * xc, axis=-1, keepdims=True)
        yln = xc * lax.rsqrt(var + LN_EPS) * lg_ref[...] + lb_ref[...]
        yc_ref[...] = (yln * jax.nn.sigmoid(yln)).astype(BF16)

    vec = pl.BlockSpec((1, CONV_W), lambda i: (0, 0))
    return _call(
        body, name="conv_fwd", grid=(nblk,),
        in_specs=_halo_specs(t, 0, CONV_W) + _halo_specs(t, 1, CONV_W)
        + [pl.BlockSpec((32, CONV_W), lambda i: (0, 0)), vec, vec, vec],
        out_specs=[pl.BlockSpec((ROWS, CONV_W), lambda i: (i, 0)),
                   pl.BlockSpec((ROWS, CONV_W), lambda i: (i, 0))],
        out_shape=[jax.ShapeDtypeStruct((t, CONV_W), BF16), jax.ShapeDtypeStruct((t, CONV_W), F32)],
        scratch_shapes=[EXT_SCRATCH, SHIFT_SCRATCH],
        sem=("parallel",), args=(proj, proj, proj, proj, proj, proj, w_dw, b_dw, ln_g, ln_b), comm=comm)


def _conv_bwd(proj, uc, dcat, w_dw, ln_g, ln_b):
    t = proj.shape[0]
    nblk = t // ROWS
    pad = HALO - CONV_K // 2

    half = ROWS // 2

    def body(ac, ap, an, gc, gp, gn, uc_c, uc_p, uc_n, dy_c, dy_p, dy_n, w_ref, lg_ref, lb_ref,
             dag_ref, dw_ref, dvec_ref, uext, dext, ushift, dshift, dw_part):
        i = pl.program_id(0)
        _fill_ext(uext, _glu(ac, gc), _glu(ap, gp), _glu(an, gn), i, nblk)
        _shift_copies(uext, ushift)

        def ln_bwd(u_ref, d_ref):
            u = u_ref[...]
            mu = jnp.mean(u, axis=-1, keepdims=True)
            xc = u - mu
            rstd = lax.rsqrt(jnp.mean(xc * xc, axis=-1, keepdims=True) + LN_EPS)
            xhat = xc * rstd
            yln = xhat * lg_ref[...] + lb_ref[...]
            sg = jax.nn.sigmoid(yln)
            dyln = d_ref[...] * (sg * (1.0 + yln * (1.0 - sg)))
            dxh = dyln * lg_ref[...]
            du = rstd * (dxh - jnp.mean(dxh, axis=-1, keepdims=True)
                         - xhat * jnp.mean(dxh * xhat, axis=-1, keepdims=True))
            return du, dyln, xhat

        duc, dyln, xhat = ln_bwd(uc_c, dy_c)
        _fill_ext(dext, duc, ln_bwd(uc_p, dy_p)[0], ln_bwd(uc_n, dy_n)[0], i, nblk)
        _shift_copies(dext, dshift)

        @pl.when(i == 0)
        def _():
            dw_part[...] = jnp.zeros_like(dw_part)
            dvec_ref[...] = jnp.zeros_like(dvec_ref)

        dvec_ref[pl.ds(0, 1), :] += jnp.sum(duc, axis=0, keepdims=True)
        dvec_ref[pl.ds(1, 1), :] += jnp.sum(dyln * xhat, axis=0, keepdims=True)
        dvec_ref[pl.ds(2, 1), :] += jnp.sum(dyln, axis=0, keepdims=True)

        def input_grad(cc, carry):
            lanes = pl.ds(pl.multiple_of(cc * LANES, LANES), LANES)
            gate_lanes = pl.ds(pl.multiple_of(CONV_W + cc * LANES, LANES), LANES)
            for r0 in (0, half):
                du = jnp.zeros((half, LANES), F32)
                for j in range(CONV_K):
                    du = du + _window(dshift, 2 * HALO - pad - j, half, lanes, r0) * w_ref[pl.ds(j, 1), lanes]
                rows = pl.ds(r0, half)
                a = ac[rows, lanes].astype(F32)
                sg = jax.nn.sigmoid(gc[rows, lanes].astype(F32))
                dag_ref[rows, lanes] = (du * sg).astype(BF16)
                dag_ref[rows, gate_lanes] = (du * a * sg * (1.0 - sg)).astype(BF16)
            return carry

        def tap_grad(cc, carry):
            lanes = pl.ds(pl.multiple_of(cc * LANES, LANES), LANES)
            for r0 in (0, half):
                duc_c = dext[pl.ds(HALO + r0, half), lanes]
                parts = [jnp.sum((duc_c * _window(ushift, j + pad, half, lanes, r0))
                                 .reshape(half // SUBLANES, SUBLANES, LANES), axis=0) for j in range(CONV_K)]
                parts.append(jnp.zeros((SUBLANES, LANES), F32))
                dw_part[:, lanes] += jnp.concatenate(parts, axis=0)
            return carry

        lax.fori_loop(0, CONV_W // LANES, input_grad, 0)
        lax.fori_loop(0, CONV_W // LANES, tap_grad, 0)

        @pl.when(i == nblk - 1)
        def _():
            dw_ref[...] = jnp.sum(dw_part[...].reshape(32, SUBLANES, CONV_W), axis=1)

    vec = pl.BlockSpec((1, CONV_W), lambda i: (0, 0))
    return pl.pallas_call(
        body, name="conv_bwd", grid=(nblk,),
        in_specs=_halo_specs(t, 0, CONV_W) + _halo_specs(t, 1, CONV_W) + _halo_specs(t, 0, CONV_W)
        + _halo_specs(t, 0, CONV_W) + [pl.BlockSpec((32, CONV_W), lambda i: (0, 0)), vec, vec],
        out_specs=(pl.BlockSpec((ROWS, 2 * CONV_W), lambda i: (i, 0)),
                   pl.BlockSpec((32, CONV_W), lambda i: (0, 0)),
                   pl.BlockSpec((8, CONV_W), lambda i: (0, 0))),
        out_shape=(jax.ShapeDtypeStruct((t, 2 * CONV_W), BF16),
                   jax.ShapeDtypeStruct((32, CONV_W), F32),
                   jax.ShapeDtypeStruct((8, CONV_W), F32)),
        scratch_shapes=[EXT_SCRATCH, EXT_SCRATCH, SHIFT_SCRATCH, SHIFT_SCRATCH,
                        pltpu.VMEM((32 * SUBLANES, CONV_W), F32)],
        compiler_params=_params("arbitrary"),
    )(proj, proj, proj, proj, proj, proj, uc, uc, uc, dcat, dcat, dcat, w_dw, ln_g, ln_b)


Q_BLK, K_BLK, V_BLK = 16, 24, 32


def _head_masks():
    lane = lax.broadcasted_iota(jnp.int32, (1, 2 * HEAD_D), 1)
    return [lane < HEAD_D, lane >= HEAD_D]


def _band(r, rows):
    rs = jnp.clip(r - WIN_ROWS // 2, 0, rows - WIN_ROWS)
    return rs, r - rs, pl.multiple_of(r * GRID_W, GRID_W), pl.multiple_of(rs * GRID_W, GRID_W)


ROWS_PER_STEP = 2
NT_DIMS = (((1,), (1,)), ((), ()))
TN_DIMS = (((0,), (0,)), ((), ()))
N_DR = 16
GE_SPEC = pl.BlockSpec((2, N_DR, GRID_W, GRID_W), lambda h: (h, 0, 0, 0))
BIAS_SCRATCH = pltpu.VMEM((WIN_ROWS, 2 * GRID_W, BAND), F32)


def _stack_heads(block, masks):
    z = jnp.zeros_like(block)
    return jnp.concatenate([jnp.where(masks[0], block, z), jnp.where(masks[1], block, z)], axis=0)


def _unstack_heads(both, masks):
    return jnp.where(masks[0], both[:GRID_W], both[GRID_W:])


def _softmax(s, bias):
    s = s * (HEAD_D ** -0.5) + bias
    e = jnp.exp(s - jnp.max(s, axis=-1, keepdims=True))
    return e * (1.0 / jnp.sum(e, axis=-1, keepdims=True))


def _fill_bias(ge_ref, bias):
    wq = lax.broadcasted_iota(jnp.int32, (GRID_W, BAND), 0)
    wk = jnp.bitwise_and(lax.broadcasted_iota(jnp.int32, (GRID_W, BAND), 1), GRID_W - 1)
    cs = jnp.clip(wq - WIN_COLS // 2, 0, GRID_W - WIN_COLS)
    inside = jnp.logical_and(wk >= cs, wk < cs + WIN_COLS)
    for hh in range(2):
        for off in range(WIN_ROWS):
            tile = jnp.concatenate([ge_ref[hh, WIN_ROWS - 1 - off + kr] for kr in range(WIN_ROWS)], axis=-1)
            bias[off, pl.ds(hh * GRID_W, GRID_W), :] = jnp.where(inside, tile, NEG_INF)


def _natten_fwd(proj, ge, comm=None):
    t = proj.shape[0]
    rows = t // GRID_W

    def body(q_ref, k_ref, v_ref, ge_ref, o_ref, bias):
        masks = _head_masks()
        _fill_bias(ge_ref, bias)

        def step(i, carry):
            bands = [_band(i * ROWS_PER_STEP + u, rows) for u in range(ROWS_PER_STEP)]
            vbs = [v_ref[pl.ds(k0, BAND), :] for _, _, _, k0 in bands]
            ss = [lax.dot_general(_stack_heads(q_ref[pl.ds(q0, GRID_W), :], masks), k_ref[pl.ds(k0, BAND), :],
                                  NT_DIMS, preferred_element_type=F32) for _, _, q0, k0 in bands]
            ps = [_softmax(s, bias[off]) for s, (_, off, _, _) in zip(ss, bands)]
            os = [jnp.dot(p.astype(BF16), vb, preferred_element_type=F32) for p, vb in zip(ps, vbs)]
            for o, (_, _, q0, _) in zip(os, bands):
                o_ref[pl.ds(q0, GRID_W), :] = _unstack_heads(o, masks).astype(BF16)
            return carry

        lax.fori_loop(0, rows // ROWS_PER_STEP, step, 0)

    col = lambda b: pl.BlockSpec((t, 2 * HEAD_D), lambda h: (0, b + h))
    return _call(
        body, name="natten_fwd", grid=(N_HEADS // 2,),
        in_specs=[col(Q_BLK), col(K_BLK), col(V_BLK), GE_SPEC],
        out_specs=[pl.BlockSpec((t, 2 * HEAD_D), lambda h: (0, h))],
        out_shape=[jax.ShapeDtypeStruct((t, NA_W), BF16)],
        scratch_shapes=[BIAS_SCRATCH],
        sem=("parallel",), args=(proj, proj, proj, ge), comm=comm)


def _natten_bwd(proj, ge, dcat, comm=None):
    t = proj.shape[0]
    rows = t // GRID_W
    scale = HEAD_D ** -0.5

    def body(q_ref, k_ref, v_ref, ge_ref, do_ref, dq_ref, dk_ref, dv_ref, dge_ref, bias, dbias, dk_acc, dv_acc):
        masks = _head_masks()
        _fill_bias(ge_ref, bias)
        dk_acc[...] = jnp.zeros_like(dk_acc)
        dv_acc[...] = jnp.zeros_like(dv_acc)
        dbias[...] = jnp.zeros_like(dbias)

        def step(i, carry):
            bands = [_band(i * ROWS_PER_STEP + u, rows) for u in range(ROWS_PER_STEP)]
            kbs = [k_ref[pl.ds(k0, BAND), :] for _, _, _, k0 in bands]
            q2s = [_stack_heads(q_ref[pl.ds(q0, GRID_W), :], masks) for _, _, q0, _ in bands]
            do2s = [_stack_heads(do_ref[pl.ds(q0, GRID_W), :].astype(BF16), masks) for _, _, q0, _ in bands]
            ss = [lax.dot_general(q2, kb, NT_DIMS, preferred_element_type=F32) for q2, kb in zip(q2s, kbs)]
            dps = [lax.dot_general(do2, v_ref[pl.ds(k0, BAND), :], NT_DIMS, preferred_element_type=F32)
                   for do2, (_, _, _, k0) in zip(do2s, bands)]
            ps = [_softmax(s, bias[off]) for s, (_, off, _, _) in zip(ss, bands)]
            dss =[p * (dp - jnp.sum(p * dp, axis=-1, keepdims=True)) for p, dp in zip(ps, dps)]
            dsbs = [ds.astype(BF16) for ds in dss]
            dqs = [jnp.dot(dsb, kb, preferred_element_type=F32) for dsb, kb in zip(dsbs, kbs)]
            dks = [lax.dot_general(dsb, q2, TN_DIMS, preferred_element_type=F32) for dsb, q2 in zip(dsbs, q2s)]
            dvs = [lax.dot_general(p.astype(BF16), do2, TN_DIMS, preferred_element_type=F32)
                   for p, do2 in zip(ps, do2s)]
            for ds, dq, dk, dv, (_, off, q0, k0) in zip(dss, dqs, dks, dvs, bands):
                dbias[off] += ds
                dq_ref[pl.ds(q0, GRID_W), :] = (_unstack_heads(dq, masks) * scale).astype(BF16)
                dk_acc[pl.ds(k0, BAND), :] += dk * scale
                dv_acc[pl.ds(k0, BAND), :] += dv
            return carry

        lax.fori_loop(0, rows // ROWS_PER_STEP, step, 0)
        dk_ref[...] = dk_acc[...].astype(BF16)
        dv_ref[...] = dv_acc[...].astype(BF16)
        for hh in range(2):
            for dr in range(N_DR):
                pieces = [dbias[off, pl.ds(hh * GRID_W, GRID_W), pl.ds((dr + off - WIN_ROWS + 1) * GRID_W, GRID_W)]
                          for off in range(WIN_ROWS) if 0 <= dr + off - WIN_ROWS + 1 < WIN_ROWS]
                dge_ref[hh, dr] = functools.reduce(jnp.add, pieces) if pieces else jnp.zeros((GRID_W, GRID_W), F32)

    col = lambda b: pl.BlockSpec((t, 2 * HEAD_D), lambda h: (0, b + h))
    o_spec = pl.BlockSpec((t, 2 * HEAD_D), lambda h: (0, h))
    o_shape = jax.ShapeDtypeStruct((t, NA_W), BF16)
    return _call(
        body, name="natten_bwd", grid=(N_HEADS // 2,),
        in_specs=[col(Q_BLK), col(K_BLK), col(V_BLK), GE_SPEC, col(CONV_W // (2 * HEAD_D))],
        out_specs=[o_spec, o_spec, o_spec, GE_SPEC],
        out_shape=[o_shape, o_shape, o_shape, jax.ShapeDtypeStruct((N_HEADS, N_DR, GRID_W, GRID_W), F32)],
        scratch_shapes=[BIAS_SCRATCH, BIAS_SCRATCH, pltpu.VMEM((t, 2 * HEAD_D), F32),
                        pltpu.VMEM((t, 2 * HEAD_D), F32)],
        sem=("parallel",), args=(proj, proj, proj, ge, dcat), comm=comm)


def _column_offsets():
    c = np.arange(GRID_W)
    return (np.clip(c[None, :] - c[:, None], -(WIN_COLS - 1), WIN_COLS - 1) + (WIN_COLS - 1)).reshape(-1)


def _rpb_expand(rpb_l):
    n_dc = 32
    et = np.zeros((n_dc, GRID_W * GRID_W), np.float32)
    et[_column_offsets(), np.arange(GRID_W * GRID_W)] = 1.0
    rp = jnp.pad(rpb_l, ((0, 0), (0, N_DR - rpb_l.shape[1]), (0, n_dc - rpb_l.shape[2]))).reshape(N_HEADS * N_DR, n_dc)

    def body(r_ref, e_ref, o_ref):
        o_ref[...] = jnp.dot(r_ref[...], e_ref[...], preferred_element_type=F32, precision=lax.Precision.HIGHEST)

    vm = pl.BlockSpec(memory_space=pltpu.VMEM)
    ge = pl.pallas_call(
        body, name="rpb_expand", in_specs=[vm, vm], out_specs=vm,
        out_shape=jax.ShapeDtypeStruct((N_HEADS * N_DR, GRID_W * GRID_W), F32),
        compiler_params=pltpu.CompilerParams(vmem_limit_bytes=VMEM_LIMIT),
    )(rp, jnp.asarray(et))
    return ge.reshape(N_HEADS, N_DR, GRID_W, GRID_W)


def _rpb_grad(dge):
    e = np.zeros((GRID_W * GRID_W, 128), np.float32)
    e[np.arange(GRID_W * GRID_W), _column_offsets()] = 1.0

    def body(z_ref, e_ref, o_ref):
        o_ref[...] = jnp.dot(z_ref[...], e_ref[...], preferred_element_type=F32, precision=lax.Precision.HIGHEST)

    vm = pl.BlockSpec(memory_space=pltpu.VMEM)
    out = pl.pallas_call(
        body, name="rpb_grad", in_specs=[vm, vm], out_specs=vm,
        out_shape=jax.ShapeDtypeStruct((N_HEADS * N_DR, 128), F32),
        compiler_params=pltpu.CompilerParams(vmem_limit_bytes=VMEM_LIMIT),
    )(dge.reshape(N_HEADS * N_DR, GRID_W * GRID_W), jnp.asarray(e))
    return out.reshape(N_HEADS, N_DR, 128)[:, :2 * WIN_ROWS - 1, :2 * WIN_COLS - 1]


def _cast_bf16(w2d, *, name):
    r, c = w2d.shape
    tr = _tile(r, (512, 256))

    def body(w_ref, o_ref):
        o_ref[...] = w_ref[...].astype(BF16)

    spec = pl.BlockSpec((tr, c), lambda i: (i, 0))
    return pl.pallas_call(
        body, name=name, grid=(r // tr,),
        in_specs=[spec], out_specs=spec, out_shape=jax.ShapeDtypeStruct((r, c), BF16),
        compiler_params=_params("parallel"),
    )(w2d)


def _core_index():
    return lax.axis_index("c").astype(jnp.int32).reshape(1)


def _pair_sum(dw, other, *, name):
    s, r, c = dw.shape
    h = r // 2
    tr = _tile(h, (256, 128))
    nb = h // tr

    def body(c_ref, a_ref, b_ref, o_ref):
        del c_ref
        o_ref[...] = (a_ref[...].astype(F32) + b_ref[...].astype(F32)).astype(BF16)

    return pl.pallas_call(
        body, name=name,
        grid_spec=pltpu.PrefetchScalarGridSpec(
            num_scalar_prefetch=1, grid=(s, nb),
            in_specs=[pl.BlockSpec((None, tr, c), lambda si, i, cr: (si, cr[0] * nb + i, 0)),
                      pl.BlockSpec((None, tr, c), lambda si, i, cr: (si, i, 0))],
            out_specs=pl.BlockSpec((None, tr, c), lambda si, i, cr: (si, i, 0))),
        out_shape=jax.ShapeDtypeStruct((s, h, c), BF16),
        compiler_params=_params("parallel", "parallel"),
    )(_core_index(), dw, other)


def _chip_sum(parts, grad, layer, *, name):
    s, h, c = parts.shape
    tr = _tile(h, (256, 128))
    nb = h // tr

    def body(c_ref, p_ref, g_in, o_ref):
        del c_ref, g_in
        acc = p_ref[0].astype(F32) + p_ref[1].astype(F32)
        acc = acc + p_ref[2].astype(F32)
        o_ref[...] = acc + p_ref[3].astype(F32)

    return pl.pallas_call(
        body, name=name,
        grid_spec=pltpu.PrefetchScalarGridSpec(
            num_scalar_prefetch=1, grid=(nb,),
            in_specs=[pl.BlockSpec((s, tr, c), lambda i, cr: (0, i, 0)), ANY],
            out_specs=pl.BlockSpec((None, tr, c), lambda i, cr: (layer, cr[0] * nb + i, 0))),
        out_shape=jax.ShapeDtypeStruct(grad.shape, F32),
        input_output_aliases={2: 0},
        compiler_params=_params("parallel"),
    )(_core_index(), parts, grad)


def _adamw(w, g, m, v, *, name):
    r, c = w.shape
    tr = _tile(r, (256, 128, 8))
    tr = tr if r % tr == 0 else r
    bc1 = 1.0 - ADAM_B1 ** ADAM_STEP
    bc2 = 1.0 - ADAM_B2 ** ADAM_STEP

    def body(w_ref, g_ref, m_ref, v_ref, go_ref, d_ref, mo_ref, vo_ref):
        gv = g_ref[...]
        mn = ADAM_B1 * m_ref[...] + (1.0 - ADAM_B1) * gv
        vn = ADAM_B2 * v_ref[...] + (1.0 - ADAM_B2) * (gv * gv)
        go_ref[...] = gv
        mo_ref[...] = mn
        vo_ref[...] = vn
        d_ref[...] = -ADAM_LR * ((mn / bc1) / (jnp.sqrt(vn / bc2) + ADAM_EPS) + ADAM_WD * w_ref[...])

    spec = pl.BlockSpec((tr, c), lambda i: (i, 0))
    shape = jax.ShapeDtypeStruct((r, c), F32)
    return pl.pallas_call(
        body, name=name, grid=(r // tr,),
        in_specs=[spec] * 4, out_specs=(spec,) * 4, out_shape=(shape,) * 4,
        compiler_params=_params("parallel"),
    )(w, g, m, v)


def _me():
    return lax.axis_index("x"), lax.axis_index("y"), lax.axis_index("c")


def _other_chips(x, y):
    return [(1 - x, y), (x, 1 - y), (1 - x, 1 - y)]


def _remote(src, dst, send, recv, k, to):
    return pltpu.make_async_remote_copy(src_ref=src, dst_ref=dst, send_sem=send.at[k], recv_sem=recv.at[k],
                                        device_id=to, device_id_type=MESH)


def _run_comm(comm, *, name):
    def body(*refs):
        ni, no = len(comm.ins), len(comm.out_shape)
        ins, outs, sems = refs[:ni], refs[ni:ni + no], refs[ni + no:]
        comm.start(ins, outs, sems)
        comm.finish(ins, outs, sems)

    return pl.pallas_call(
        body, name=name,
        in_specs=[ANY] * len(comm.ins), out_specs=[ANY] * len(comm.out_shape),
        out_shape=list(comm.out_shape), scratch_shapes=list(comm.sems),
        input_output_aliases={i: i for i in range(len(comm.ins))} if comm.aliased else {},
    )(*comm.ins)


def _half(ref, chip, hc):
    hr = ref.shape[1] // 2
    return ref.at[chip, pl.ds(hc * hr, hr), :]


def _gather_comm(layer, shards):
    nw = len(shards)

    def sent(ins, outs, sems):
        send, recv = sems
        x, y, c = _me()
        me_chip = 2 * x + y
        first = []
        for w in range(nw):
            hr = outs[w].shape[1] // 2
            src = ins[w].at[layer, pl.ds(c * hr, hr), :]
            for j, chip in enumerate(_other_chips(x, y)):
                first.append(_remote(src, _half(outs[w], me_chip, c), send, recv, 3 * w + j, (*chip, c)))
        return first

    def start(ins, outs, sems):
        for cp in sent(ins, outs, sems):
            cp.start()

    def finish(ins, outs, sems):
        send, recv = sems
        first = sent(ins, outs, sems)
        x, y, c = _me()
        for w in range(nw):
            for j, chip in enumerate(_other_chips(x, y)):
                landed = _half(outs[w], 2 * chip[0] + chip[1], c)
                _remote(landed, landed, send, recv, 3 * w + j, (*chip, c)).wait_recv()
        for cp in first:
            cp.wait_send()

    return _Comm(list(shards), [jax.ShapeDtypeStruct((N_CHIPS,) + s.shape[1:], BF16) for s in shards],
                 [pltpu.SemaphoreType.DMA((3 * nw,)), pltpu.SemaphoreType.DMA((3 * nw,))], start, finish)


def _place_own(gathered, shards, layer, chip):
    return [lax.dynamic_update_slice(g, s[layer][None], (chip, 0, 0)) for g, s in zip(gathered, shards)]


def _forward_comm(gathered):
    nw = len(gathered)

    def sent(outs, sems):
        send, recv = sems
        x, y, c = _me()
        cps = []
        for w in range(nw):
            for j, chip in enumerate(_other_chips(x, y)):
                landed = _half(outs[w], 2 * chip[0] + chip[1], c)
                cps.append(_remote(landed, landed, send, recv, 3 * w + j, (x, y, 1 - c)))
        return cps

    def start(ins, outs, sems):
        for cp in sent(outs, sems):
            cp.start()

    def finish(ins, outs, sems):
        send, recv = sems
        x, y, c = _me()
        for w in range(nw):
            for j, chip in enumerate(_other_chips(x, y)):
                theirs = _half(outs[w], 2 * chip[0] + chip[1], 1 - c)
                _remote(theirs, theirs, send, recv, 3 * w + j, (x, y, 1 - c)).wait_recv()
        for cp in sent(outs, sems):
            cp.wait_send()

    return _Comm(list(gathered), [jax.ShapeDtypeStruct(g.shape, BF16) for g in gathered],
                 [pltpu.SemaphoreType.DMA((3 * nw,)), pltpu.SemaphoreType.DMA((3 * nw,))], start, finish,
                 aliased=True)


def _pair_exchange(dws):
    nw = len(dws)

    def body(*refs):
        ins, outs = refs[:nw], refs[nw:2 * nw]
        send, recv = refs[2 * nw:]
        x, y, c = _me()
        cps = []
        for w in range(nw):
            hr = ins[w].shape[1] // 2
            src = ins[w].at[:, pl.ds((1 - c) * hr, hr), :]
            cps.append(_remote(src, outs[w], send, recv, w, (x, y, 1 - c)))
        for cp in cps:
            cp.start()
        for cp in cps:
            cp.wait()

    return pl.pallas_call(
        body, name="grad_pair_exchange",
        in_specs=[ANY] * nw, out_specs=[ANY] * nw,
        out_shape=[jax.ShapeDtypeStruct((d.shape[0], d.shape[1] // 2, d.shape[2]), BF16) for d in dws],
        scratch_shapes=[pltpu.SemaphoreType.DMA((nw,)), pltpu.SemaphoreType.DMA((nw,))],
    )(*dws)


def _scatter_comm(qs):
    nw = len(qs)

    def sent(ins, outs, sems):
        send, recv, local = sems
        x, y, c = _me()
        me_chip = 2 * x + y
        mine = [pltpu.make_async_copy(ins[w].at[me_chip], outs[w].at[me_chip], local.at[w]) for w in range(nw)]
        cps = []
        for w in range(nw):
            for j, chip in enumerate(_other_chips(x, y)):
                cps.append(_remote(ins[w].at[2 * chip[0] + chip[1]], outs[w].at[me_chip], send, recv,
                                   3 * w + j, (*chip, c)))
        return mine, cps

    def start(ins, outs, sems):
        mine, cps = sent(ins, outs, sems)
        for cp in mine + cps:
            cp.start()

    def finish(ins, outs, sems):
        send, recv, _ = sems
        mine, cps = sent(ins, outs, sems)
        x, y, c = _me()
        for w in range(nw):
            for j, chip in enumerate(_other_chips(x, y)):
                theirs = outs[w].at[2 * chip[0] + chip[1]]
                _remote(theirs, theirs, send, recv, 3 * w + j, (*chip, c)).wait_recv()
        for cp in cps:
            cp.wait_send()
        for cp in mine:
            cp.wait()

    return _Comm(list(qs), [jax.ShapeDtypeStruct(q.shape, BF16) for q in qs],
                 [pltpu.SemaphoreType.DMA((3 * nw,)), pltpu.SemaphoreType.DMA((3 * nw,)),
                  pltpu.SemaphoreType.DMA((nw,))], start, finish)


def _sibling_fill(layer, grads):
    nw = len(grads)

    def body(*refs):
        ins, outs = refs[:nw], refs[nw:2 * nw]
        send, recv = refs[2 * nw:]
        del ins
        x, y, c = _me()
        cps = []
        for w in range(nw):
            hr = outs[w].shape[1] // 2
            mine = outs[w].at[layer, pl.ds(c * hr, hr), :]
            cps.append(_remote(mine, mine, send, recv, w, (x, y, 1 - c)))
        for cp in cps:
            cp.start()
        for w in range(nw):
            hr = outs[w].shape[1] // 2
            theirs = outs[w].at[layer, pl.ds((1 - c) * hr, hr), :]
            _remote(theirs, theirs, send, recv, w, (x, y, 1 - c)).wait_recv()
        for cp in cps:
            cp.wait_send()

    return pl.pallas_call(
        body, name=f"grad_sibling_fill_{layer}",
        in_specs=[ANY] * nw, out_specs=[ANY] * nw,
        out_shape=[jax.ShapeDtypeStruct(g.shape, F32) for g in grads],
        input_output_aliases={w: w for w in range(nw)},
        scratch_shapes=[pltpu.SemaphoreType.DMA((nw,)), pltpu.SemaphoreType.DMA((nw,))],
    )(*grads)


def _all_devices(pack, *, reduce, name):
    r = pack.shape[0]

    def body(p_ref, o_ref, buf, send, recv):
        x, y, c = _me()
        me = 4 * x + 2 * y + c
        buf[me] = p_ref[...]
        flips = [(fx, fy, fc) for fx in (0, 1) for fy in (0, 1) for fc in (0, 1)][1:]
        peers = [(x ^ fx, y ^ fy, c ^ fc) for fx, fy, fc in flips]
        cps = [_remote(p_ref, buf.at[me], send, recv, k, peer) for k, peer in enumerate(peers)]
        for cp in cps:
            cp.start()
        for k, (px, py, pc) in enumerate(peers):
            theirs = buf.at[4 * px + 2 * py + pc]
            _remote(theirs, theirs, send, recv, k, (px, py, pc)).wait_recv()
        for cp in cps:
            cp.wait_send()
        if reduce:
            acc = buf[0]
            for d in range(1, N_DEV):
                acc = acc + buf[d]
            o_ref[...] = acc
        else:
            o_ref[...] = buf[...]

    vm = pl.BlockSpec(memory_space=pltpu.VMEM)
    return pl.pallas_call(
        body, name=name,
        in_specs=[vm], out_specs=vm,
        out_shape=jax.ShapeDtypeStruct((r, 128) if reduce else (N_DEV, r, 128), F32),
        scratch_shapes=[pltpu.VMEM((N_DEV, r, 128), F32), pltpu.SemaphoreType.DMA((N_DEV - 1,)),
                        pltpu.SemaphoreType.DMA((N_DEV - 1,))],
        compiler_params=pltpu.CompilerParams(vmem_limit_bytes=VMEM_LIMIT),
    )(pack)


def _to_rows(flat):
    v = flat.reshape(-1)
    rows = -(-v.shape[0] // 1024) * 8
    return jnp.pad(v, (0, rows * 128 - v.shape[0])).reshape(rows, 128)


def kernel(x, w_in, w_dw, b_dw, conv_ln_g, conv_ln_b, rpb, w_out, w_up, w_down, pre_mix_g, post_mix_g, pre_mlp_g, post_mlp_g, loss_target, m_w_in, m_w_dw, m_b_dw, m_conv_ln_g, m_conv_ln_b, m_rpb, m_w_out, m_w_up, m_w_down, m_pre_mix_g, m_post_mix_g, m_pre_mlp_g, m_post_mlp_g, v_w_in, v_w_dw, v_b_dw, v_conv_ln_g, v_conv_ln_b, v_rpb, v_w_out, v_w_up, v_w_down, v_pre_mix_g, v_post_mix_g, v_pre_mlp_g, v_post_mlp_g):
    depth = w_in.shape[0]
    t = x.shape[1]
    xl = x.reshape(t, D_MODEL)
    target = loss_target.reshape(t, D_MODEL)
    chip = 2 * lax.axis_index("x") + lax.axis_index("y")

    big = {"w_in": w_in, "w_out": w_out, "w_up": w_up, "w_down": w_down}
    big_names = list(big)
    shards = [_cast_bf16(big[n].reshape(-1, big[n].shape[-1]), name=f"cast_{n}").reshape(big[n].shape)
              for n in big_names]

    wdw_all = _all_devices(_to_rows(w_dw), reduce=False, name="gather_w_dw")[::2]
    wdw_all = wdw_all.reshape(N_CHIPS, -1)[:, :w_dw.size].reshape((N_CHIPS,) + w_dw.shape)
    wdw_full = jnp.moveaxis(wdw_all, 0, 2).reshape(depth, CONV_K, CONV_W)
    wdw_pad = jnp.pad(wdw_full, ((0, 0), (0, 32 - CONV_K), (0, 0)))

    vec = lambda a, l: a[l].reshape(1, -1)

    saved = []
    h = _rms_fwd(xl, vec(pre_mix_g, 0), name="rms_first")
    (g_in,) = _place_own(_run_comm(_forward_comm(_run_comm(_gather_comm(0, shards[:1]), name="gather_w_in_first")),
                                   name="forward_w_in_first"), shards[:1], 0, chip)
    g_out = None
    for l in range(depth):
        more = l + 1 < depth
        bias = _rpb_expand(rpb[l])
        conv_args = (wdw_pad[l], vec(b_dw, l), vec(conv_ln_g, l), vec(conv_ln_b, l))
        if l == 0:
            proj, part = _mm_nn(h, g_in, out_dtype=BF16, name="proj", comm=_gather_comm(0, shards[1:2]))
            yc, uc, g_out = _conv_fwd(proj, *conv_args, comm=_forward_comm([part]))
            (g_out,) = _place_own([g_out], shards[1:2], 0, chip)
        else:
            (proj,) = _mm_nn(h, g_in, out_dtype=BF16, name="proj")
            yc, uc = _conv_fwd(proj, *conv_args)
        g_out = g_out.reshape(1, D_MODEL, D_MODEL)
        ya, part = _natten_fwd(proj, bias, _gather_comm(l, shards[2:3]))
        cat = jnp.concatenate([yc, ya], axis=1)
        mix, g_up = _mm_nn(cat, g_out, out_dtype=F32, name="out_proj", comm=_forward_comm([part]))
        (g_up,) = _place_own([g_up], shards[2:3], l, chip)
        x1, h2 = _resid_rms(xl, mix, vec(post_mix_g, l), vec(pre_mlp_g, l), name="mix_resid")
        up, act, part = _mm_nn(h2, g_up, out_dtype=BF16, relu2=True, name="mlp_up",
                               comm=_gather_comm(l, shards[3:]))
        (g_down,) = _place_own(_run_comm(_forward_comm([part]), name="forward_w_down"), shards[3:], l, chip)
        g_down = g_down.reshape(1, D_FF, D_MODEL)
        f, *parts = _mm_nn(act, g_down, out_dtype=F32, name="mlp_down",
                           comm=_gather_comm(l + 1, shards[:2]) if more else None)
        saved.append(dict(x=xl, h=h, proj=proj, uc=uc, cat=cat, mix=mix, x1=x1, h2=h2, up=up, act=act, f=f,
                          bias=bias, w=(g_in, g_out, g_up, g_down)))
        if more:
            xl, h, g_in, g_out = _resid_rms(x1, f, vec(post_mlp_g, l), vec(pre_mix_g, l + 1), name="mlp_resid",
                                            comm=_forward_comm(parts))
            g_in, g_out = _place_own([g_in, g_out], shards[:2], l + 1, chip)
        else:
            (xl,) = _resid_rms(x1, f, vec(post_mlp_g, l), None, name="mlp_resid_last")

    dx, loss_blk = _loss_head(xl, target)
    loss = lax.psum(loss_blk[0, 0], ("x", "y", "c"))

    grads = {n: lax.empty(big[n].shape, F32) for n in big_names}
    small = [None] * depth

    def pair_reduce(n, dw):
        (other,) = _pair_exchange([dw])
        return _pair_sum(dw, other, name=f"pair_sum_{n}")

    def finish(n, layer, part):
        (grads[n],) = _sibling_fill(layer, [_chip_sum(part, grads[n], layer, name=f"chip_sum_{n}_{layer}")])

    q_in = None
    for l in reversed(range(depth)):
        sv = saved[l]
        g_in, g_out, g_up, g_down = sv["w"]
        df, dg_post_mlp = _rms_bwd(dx, sv["f"], vec(post_mlp_g, l), None, out_dtype=BF16, name="rms_bwd_mlp_post")
        dup, *part = _mm_nt(df, g_down, out_dtype=BF16, up=sv["up"], name="d_act",
                            comm=_scatter_comm([q_in]) if q_in is not None else None)
        if part:
            finish("w_in", l + 1, part[0])
        q_down = pair_reduce("w_down", _mm_tn(sv["act"], df, 1, name="dw_down").reshape(N_CHIPS, -1, D_MODEL))
        dh2, part = _mm_nt(dup, g_up, out_dtype=F32, name="d_h2", comm=_scatter_comm([q_down]))
        finish("w_down", l, part)
        q_up = pair_reduce("w_up", _mm_tn(sv["h2"], dup, N_CHIPS, name="dw_up"))
        dx1, dg_pre_mlp = _rms_bwd(dh2, sv["x1"], vec(pre_mlp_g, l), dx, out_dtype=F32, name="rms_bwd_mlp_pre")
        dmix, dg_post_mix = _rms_bwd(dx1, sv["mix"], vec(post_mix_g, l), None, out_dtype=BF16, name="rms_bwd_mix_post")
        (dcat,) = _mm_nt(dmix, g_out, out_dtype=F32, name="d_cat")
        q_out = pair_reduce("w_out", _mm_tn(sv["cat"], dmix, 1, name="dw_out").reshape(N_CHIPS, -1, D_MODEL))
        dq, dk, dv, dbias, part = _natten_bwd(sv["proj"], sv["bias"], dcat, _scatter_comm([q_up]))
        finish("w_up", l, part)
        dag, dwdw, dvec = _conv_bwd(sv["proj"], sv["uc"], dcat, wdw_pad[l], vec(conv_ln_g, l), vec(conv_ln_b, l))
        dproj = jnp.concatenate([dag, dq, dk, dv], axis=1)
        dh, part = _mm_nt(dproj, g_in, out_dtype=F32, name="d_h", comm=_scatter_comm([q_out]))
        finish("w_out", l, part)
        q_in = pair_reduce("w_in", _mm_tn(sv["h"], dproj, N_CHIPS, name="dw_in"))
        dx, dg_pre_mix = _rms_bwd(dh, sv["x"], vec(pre_mix_g, l), dx1, out_dtype=F32, name="rms_bwd_mix_pre")

        drpb = _rpb_grad(dbias)
        small[l] = jnp.concatenate([
            dvec[0], dvec[1], dvec[2], dg_pre_mix[0], dg_post_mix[0], dg_pre_mlp[0], dg_post_mlp[0],
            drpb.reshape(-1), dwdw[:CONV_K].reshape(-1)])

    finish("w_in", 0, _run_comm(_scatter_comm([q_in]), name="grad_scatter_w_in_last")[0])
    grads = [grads[n] for n in big_names]

    small_sum = _all_devices(_to_rows(jnp.stack(small)), reduce=True, name="reduce_small_grads")
    small_sum = small_sum.reshape(-1)[:depth * small[0].shape[0]].reshape(depth, -1)
    sizes = [CONV_W, CONV_W, CONV_W, D_MODEL, D_MODEL, D_MODEL, D_MODEL, rpb[0].size, CONV_K * CONV_W]
    offs = np.concatenate([[0], np.cumsum(sizes)])
    pieces = [small_sum[:, offs[i]:offs[i + 1]] for i in range(len(sizes))]
    g_small = {
        "b_dw": pieces[0], "conv_ln_g": pieces[1], "conv_ln_b": pieces[2],
        "pre_mix_g": pieces[3], "post_mix_g": pieces[4], "pre_mlp_g": pieces[5], "post_mlp_g": pieces[6],
        "rpb": pieces[7].reshape(rpb.shape),
        "w_dw": lax.dynamic_slice_in_dim(pieces[8].reshape(depth, CONV_K, CONV_W), chip * w_dw.shape[2],
                                         w_dw.shape[2], axis=2),
    }

    given = dict(w_in=(w_in, m_w_in, v_w_in), w_dw=(w_dw, m_w_dw, v_w_dw), b_dw=(b_dw, m_b_dw, v_b_dw),
                 conv_ln_g=(conv_ln_g, m_conv_ln_g, v_conv_ln_g), conv_ln_b=(conv_ln_b, m_conv_ln_b, v_conv_ln_b),
                 rpb=(rpb, m_rpb, v_rpb), w_out=(w_out, m_w_out, v_w_out), w_up=(w_up, m_w_up, v_w_up),
                 w_down=(w_down, m_w_down, v_w_down), pre_mix_g=(pre_mix_g, m_pre_mix_g, v_pre_mix_g),
                 post_mix_g=(post_mix_g, m_post_mix_g, v_post_mix_g), pre_mlp_g=(pre_mlp_g, m_pre_mlp_g, v_pre_mlp_g),
                 post_mlp_g=(post_mlp_g, m_post_mlp_g, v_post_mlp_g))
    results = {}
    for n, g in zip(big_names, grads):
        w, m, v = given[n]
        flat = lambda a: a.reshape(-1, a.shape[-1])
        outs = _adamw(flat(w), flat(g), flat(m), flat(v), name=f"adamw_{n}")
        results[n] = [o.reshape(w.shape) for o in outs]

    small_names = list(g_small)
    pack = lambda arrs: _to_rows(jnp.concatenate([a.reshape(-1) for a in arrs]))
    outs = _adamw(pack([given[n][0] for n in small_names]), pack([g_small[n] for n in small_names]),
                  pack([given[n][1] for n in small_names]), pack([given[n][2] for n in small_names]),
                  name="adamw_small")
    pos = 0
    for n in small_names:
        w = given[n][0]
        results[n] = [o.reshape(-1)[pos:pos + w.size].reshape(w.shape) for o in outs]
        pos += w.size

    order = ["w_in", "w_dw", "b_dw", "conv_ln_g", "conv_ln_b", "rpb", "w_out", "w_up", "w_down",
             "pre_mix_g", "post_mix_g", "pre_mlp_g", "post_mlp_g"]
    return (loss, dx.reshape(x.shape), *[results[n][0] for n in order], *[results[n][1] for n in order],
            *[results[n][2] for n in order], *[results[n][3] for n in order])
```

```python
import functools

import numpy as np
import jax
import jax.numpy as jnp
from jax import lax
from jax.experimental import pallas as pl
from jax.experimental.pallas import tpu as pltpu

F32 = jnp.float32
BF16 = jnp.bfloat16

D_MODEL = 2048
CONV_W = 1024
NA_W = 1024
N_HEADS = 16
HEAD_D = 64
GRID_W = 64
WIN_ROWS = 8
WIN_COLS = 16
CONV_K = 31
D_FF = 4 * D_MODEL
IN_COLS = 2 * CONV_W + 3 * NA_W
RMS_EPS = 1e-6
LN_EPS = 1e-5
NEG_INF = -1e30
N_CHIPS = 4
N_DEV = 8
HALO = 16
BAND = WIN_ROWS * GRID_W

ADAM_LR = 0.001
ADAM_B1 = 0.9
ADAM_B2 = 0.999
ADAM_EPS = 1e-08
ADAM_WD = 0.01
ADAM_STEP = 10

VMEM_LIMIT = 48 * 1024 * 1024
MESH = pl.DeviceIdType.MESH
ANY = pl.BlockSpec(memory_space=pl.ANY)


def _params(*sem):
    return pltpu.CompilerParams(dimension_semantics=sem, vmem_limit_bytes=VMEM_LIMIT)


class _Comm:
    def __init__(self, ins, out_shape, sems, start, finish, aliased=False):
        self.ins, self.out_shape, self.sems, self.start, self.finish = ins, out_shape, sems, start, finish
        self.aliased = aliased


def _call(body, *, name, grid, in_specs, out_specs, out_shape, scratch_shapes=(), sem, args, comm=None):
    in_specs, out_specs, out_shape = list(in_specs), list(out_specs), list(out_shape)
    scratch_shapes = list(scratch_shapes)
    if comm is None:
        return pl.pallas_call(body, name=name, grid=grid, in_specs=in_specs, out_specs=out_specs,
                              out_shape=out_shape, scratch_shapes=scratch_shapes,
                              compiler_params=_params(*sem))(*args)
    ni, no, ns = len(in_specs), len(out_specs), len(scratch_shapes)
    nci, nco = len(comm.ins), len(comm.out_shape)

    def full(*refs):
        ins, refs = refs[:ni], refs[ni:]
        cins, refs = refs[:nci], refs[nci:]
        outs, refs = refs[:no], refs[no:]
        couts, refs = refs[:nco], refs[nco:]
        scr, csems = refs[:ns], refs[ns:]
        ids = [pl.program_id(d) for d in range(len(grid))]
        first = functools.reduce(jnp.logical_and, [i == 0 for i in ids])
        last = functools.reduce(jnp.logical_and, [i == g - 1 for i, g in zip(ids, grid)])

        @pl.when(first)
        def _():
            comm.start(cins, couts, csems)

        body(*ins, *outs, *scr)

        @pl.when(last)
        def _():
            comm.finish(cins, couts, csems)

    return pl.pallas_call(
        full, name=name, grid=grid,
        in_specs=in_specs + [ANY] * nci, out_specs=out_specs + [ANY] * nco,
        out_shape=out_shape + list(comm.out_shape),
        scratch_shapes=scratch_shapes + list(comm.sems),
        input_output_aliases={ni + i: no + i for i in range(nci)} if comm.aliased else {},
        compiler_params=_params(*(["arbitrary"] * len(grid))),
    )(*args, *comm.ins)


K_TILES = (2048, 1280, 1024, 512)


def _tile(n, pref):
    for t in pref:
        if n % t == 0:
            return t
    return n


def _accumulate(acc, step, n_steps, finish, part):
    if n_steps == 1:
        finish(part())
        return

    @pl.when(step == 0)
    def _():
        acc[...] = part()

    @pl.when(jnp.logical_and(step > 0, step < n_steps - 1))
    def _():
        acc[...] += part()

    @pl.when(step == n_steps - 1)
    def _():
        finish(acc[...] + part())


def _mm_nn(a, w, *, out_dtype, relu2=False, name, comm=None):
    m, k = a.shape
    s, _, n = w.shape
    tm = _tile(m, (1024, 512, 256))
    tn = _tile(n, (1024, 1280, 512))
    tk = _tile(k, K_TILES)
    nps = n // tn
    nk = k // tk

    def body(a_ref, w_ref, *rest):
        outs, acc = rest[:-1], rest[-1]

        def finish(r):
            outs[0][...] = r.astype(outs[0].dtype)
            if relu2:
                p = jnp.maximum(r, 0.0)
                outs[1][...] = (p * p).astype(outs[1].dtype)

        _accumulate(acc, pl.program_id(2), nk, finish,
                    lambda: jnp.dot(a_ref[...], w_ref[...], preferred_element_type=F32))

    o_spec = pl.BlockSpec((tm, tn), lambda i, j, kk: (i, j))
    o_shape = jax.ShapeDtypeStruct((m, s * n), out_dtype)
    return _call(
        body, name=name,
        grid=(m // tm, s * nps, nk),
        in_specs=[pl.BlockSpec((tm, tk), lambda i, j, kk: (i, kk)),
                  pl.BlockSpec((None, tk, tn), lambda i, j, kk: (j // nps, kk, j % nps))],
        out_specs=[o_spec, o_spec] if relu2 else [o_spec],
        out_shape=[o_shape, o_shape] if relu2 else [o_shape],
        scratch_shapes=[pltpu.VMEM((tm, tn), F32)],
        sem=("parallel", "parallel", "arbitrary"), args=(a, w), comm=comm)


def _mm_nt(dy, w, *, out_dtype, up=None, name, comm=None):
    m = dy.shape[0]
    s, k, n = w.shape
    tm = _tile(m, (1024, 512, 256))
    tko = _tile(k, (1024, 512))
    tn = _tile(n, K_TILES)
    nps = n // tn
    nn = s * nps

    def body(dy_ref, w_ref, *rest):
        if up is None:
            o_ref, acc = rest
        else:
            up_ref, o_ref, acc = rest

        def finish(r):
            if up is not None:
                r = r * (2.0 * jnp.maximum(up_ref[...].astype(F32), 0.0))
            o_ref[...] = r.astype(o_ref.dtype)

        _accumulate(acc, pl.program_id(2), nn, finish,
                    lambda: lax.dot_general(dy_ref[...], w_ref[...], (((1,), (1,)), ((), ())),
                                            preferred_element_type=F32))

    in_specs = [pl.BlockSpec((tm, tn), lambda i, j, kk: (i, kk)),
                pl.BlockSpec((None, tko, tn), lambda i, j, kk: (kk // nps, j, kk % nps))]
    args = [dy, w]
    if up is not None:
        in_specs.append(pl.BlockSpec((tm, tko), lambda i, j, kk: (i, j)))
        args.append(up)
    return _call(
        body, name=name,
        grid=(m // tm, k // tko, nn),
        in_specs=in_specs,
        out_specs=[pl.BlockSpec((tm, tko), lambda i, j, kk: (i, j))],
        out_shape=[jax.ShapeDtypeStruct((m, k), out_dtype)],
        scratch_shapes=[pltpu.VMEM((tm, tko), F32)],
        sem=("parallel", "parallel", "arbitrary"), args=args, comm=comm)


def _mm_tn(a, dy, s, *, name, comm=None):
    m, k = a.shape
    n = dy.shape[1] // s
    tk = _tile(k, (1024, 512))
    tn = _tile(n, (1024, 1280, 512))
    tm = _tile(m, K_TILES)
    nps = n // tn
    nm = m // tm

    def body(a_ref, dy_ref, o_ref, acc):
        def finish(r):
            o_ref[...] = r.astype(o_ref.dtype)

        _accumulate(acc, pl.program_id(2), nm, finish,
                    lambda: lax.dot_general(a_ref[...], dy_ref[...], (((0,), (0,)), ((), ())),
                                            preferred_element_type=F32))

    return _call(
        body, name=name,
        grid=(k // tk, s * nps, nm),
        in_specs=[pl.BlockSpec((tm, tk), lambda i, j, mm: (mm, i)),
                  pl.BlockSpec((tm, tn), lambda i, j, mm: (mm, j))],
        out_specs=[pl.BlockSpec((None, tk, tn), lambda i, j, mm: (j // nps, i, j % nps))],
        out_shape=[jax.ShapeDtypeStruct((s, k, n), BF16)],
        scratch_shapes=[pltpu.VMEM((tk, tn), F32)],
        sem=("parallel", "parallel", "arbitrary"), args=(a, dy), comm=comm)


ROWS = 256


def _row_spec(d):
    return pl.BlockSpec((ROWS, d), lambda i: (i, 0))


def _vec_spec(d):
    return pl.BlockSpec((1, d), lambda i: (0, 0))


def _rstd(v):
    return lax.rsqrt(jnp.mean(v * v, axis=-1, keepdims=True) + RMS_EPS)


def _rms_fwd(x, g, *, name):
    t, d = x.shape

    def body(x_ref, g_ref, h_ref):
        xv = x_ref[...]
        h_ref[...] = ((xv * _rstd(xv)) * g_ref[...]).astype(BF16)

    return pl.pallas_call(
        body, name=name, grid=(t // ROWS,),
        in_specs=[_row_spec(d), _vec_spec(d)],
        out_specs=_row_spec(d),
        out_shape=jax.ShapeDtypeStruct((t, d), BF16),
        compiler_params=_params("parallel"),
    )(x, g)


def _resid_rms(res, y, g_post, g_next, *, name, comm=None):
    t, d = res.shape
    with_next = g_next is not None

    def body(res_ref, y_ref, gp_ref, *rest):
        yv = y_ref[...]
        xn = res_ref[...] + (yv * _rstd(yv)) * gp_ref[...]
        if with_next:
            gn_ref, xo_ref, h_ref = rest
            h_ref[...] = ((xn * _rstd(xn)) * gn_ref[...]).astype(BF16)
        else:
            (xo_ref,) = rest
        xo_ref[...] = xn

    in_specs = [_row_spec(d), _row_spec(d), _vec_spec(d)]
    args = [res, y, g_post]
    out_specs = [_row_spec(d)]
    out_shape = [jax.ShapeDtypeStruct((t, d), F32)]
    if with_next:
        in_specs.append(_vec_spec(d))
        args.append(g_next)
        out_specs.append(_row_spec(d))
        out_shape.append(jax.ShapeDtypeStruct((t, d), BF16))
    return _call(body, name=name, grid=(t // ROWS,), in_specs=in_specs, out_specs=out_specs, out_shape=out_shape,
                 sem=("parallel",), args=args, comm=comm)


def _rms_bwd(dy, xin, g, res, *, out_dtype, name, comm=None):
    t, d = xin.shape
    with_res = res is not None

    def body(dy_ref, x_ref, g_ref, *rest):
        if with_res:
            res_ref, dx_ref, dg_ref = rest
        else:
            dx_ref, dg_ref = rest
        xv = x_ref[...]
        r = _rstd(xv)
        nrm = xv * r
        dyv = dy_ref[...]
        dn = dyv * g_ref[...]
        dx = r * (dn - nrm * jnp.mean(dn * nrm, axis=-1, keepdims=True))
        if with_res:
            dx = dx + res_ref[...]
        dx_ref[...] = dx.astype(dx_ref.dtype)

        @pl.when(pl.program_id(0) == 0)
        def _():
            dg_ref[...] = jnp.zeros_like(dg_ref)

        dg_ref[...] += jnp.sum(dyv * nrm, axis=0, keepdims=True)

    in_specs = [_row_spec(d), _row_spec(d), _vec_spec(d)]
    args = [dy, xin, g]
    if with_res:
        in_specs.append(_row_spec(d))
        args.append(res)
    return _call(
        body, name=name, grid=(t // ROWS,),
        in_specs=in_specs,
        out_specs=[_row_spec(d), _vec_spec(d)],
        out_shape=[jax.ShapeDtypeStruct((t, d), out_dtype), jax.ShapeDtypeStruct((1, d), F32)],
        sem=("arbitrary",), args=args, comm=comm)


def _loss_head(y, target):
    t, d = y.shape

    def body(y_ref, t_ref, dy_ref, loss_ref):
        diff = y_ref[...] - t_ref[...]
        dy_ref[...] = diff * (1.0 / d)

        @pl.when(pl.program_id(0) == 0)
        def _():
            loss_ref[...] = jnp.zeros_like(loss_ref)

        loss_ref[...] += jnp.sum(diff * diff) * (0.5 / d)

    return pl.pallas_call(
        body, name="loss_head", grid=(t // ROWS,),
        in_specs=[_row_spec(d), _row_spec(d)],
        out_specs=(_row_spec(d), pl.BlockSpec((8, 128), lambda i: (0, 0))),
        out_shape=(jax.ShapeDtypeStruct((t, d), F32), jax.ShapeDtypeStruct((8, 128), F32)),
        compiler_params=_params("arbitrary"),
    )(y, target)


def _halo_specs(t, col, width):
    rb = ROWS // HALO
    last = t // HALO - 1
    return [pl.BlockSpec((ROWS, width), lambda i: (i, col)),
            pl.BlockSpec((HALO, width), lambda i: (jnp.maximum(i * rb - 1, 0), col)),
            pl.BlockSpec((HALO, width), lambda i: (jnp.minimum((i + 1) * rb, last), col))]


def _glu(a_ref, g_ref):
    return a_ref[...].astype(F32) * jax.nn.sigmoid(g_ref[...].astype(F32))


def _fill_ext(ext, cur, prev, nxt, i, nblk):
    ext[pl.ds(HALO, ROWS), :] = cur
    ext[pl.ds(0, HALO), :] = jnp.where(i > 0, prev, 0.0)
    ext[pl.ds(HALO + ROWS, HALO), :] = jnp.where(i < nblk - 1, nxt, 0.0)


LANES = 128
SUBLANES = 8
EXT_ROWS = ROWS + 2 * HALO
EXT_SCRATCH = pltpu.VMEM((EXT_ROWS, CONV_W), F32)
SHIFT_SCRATCH = pltpu.VMEM((SUBLANES, EXT_ROWS - SUBLANES, CONV_W), F32)


def _shift_copies(ext, shifted):
    for s in range(SUBLANES):
        shifted[s] = ext[pl.ds(s, EXT_ROWS - SUBLANES), :]


def _window(shifted, offset, rows, lanes, start=0):
    row0 = start + offset - offset % SUBLANES
    if not isinstance(row0, int):
        row0 = pl.multiple_of(row0, SUBLANES)
    return shifted[offset % SUBLANES, pl.ds(row0, rows), lanes]


def _conv_fwd(proj, w_dw, b_dw, ln_g, ln_b, comm=None):
    t = proj.shape[0]
    nblk = t // ROWS
    pad = HALO - CONV_K // 2

    def body(ac, ap, an, gc, gp, gn, w_ref, b_ref, lg_ref, lb_ref, yc_ref, uc_ref, uext, ushift):
        i = pl.program_id(0)
        _fill_ext(uext, _glu(ac, gc), _glu(ap, gp), _glu(an, gn), i, nblk)
        _shift_copies(uext, ushift)

        def chunk(cc, carry):
            lanes = pl.ds(pl.multiple_of(cc * LANES, LANES), LANES)
            acc = jnp.broadcast_to(b_ref[:, lanes], (ROWS, LANES))
            for j in range(CONV_K):
                acc = acc + _window(ushift, j + pad, ROWS, lanes) * w_ref[pl.ds(j, 1), lanes]
            uc_ref[:, lanes] = acc
            return carry

        lax.fori_loop(0, CONV_W // LANES, chunk, 0)
        acc = uc_ref[...]
        mu = jnp.mean(acc, axis=-1, keepdims=True)
        xc = acc - mu
        var = jnp.mean(xc * xc, axis=-1, keepdims=True)
        yln = xc * lax.rsqrt(var + LN_EPS) * lg_ref[...] + lb_ref[...]
        yc_ref[...] = (yln * jax.nn.sigmoid(yln)).astype(BF16)

    vec = pl.BlockSpec((1, CONV_W), lambda i: (0, 0))
    return _call(
        body, name="conv_fwd", grid=(nblk,),
        in_specs=_halo_specs(t, 0, CONV_W) + _halo_specs(t, 1, CONV_W)
        + [pl.BlockSpec((32, CONV_W), lambda i: (0, 0)), vec, vec, vec],
        out_specs=[pl.BlockSpec((ROWS, CONV_W), lambda i: (i, 0)),
                   pl.BlockSpec((ROWS, CONV_W), lambda i: (i, 0))],
        out_shape=[jax.ShapeDtypeStruct((t, CONV_W), BF16), jax.ShapeDtypeStruct((t, CONV_W), F32)],
        scratch_shapes=[EXT_SCRATCH, SHIFT_SCRATCH],
        sem=("parallel",), args=(proj, proj, proj, proj, proj, proj, w_dw, b_dw, ln_g, ln_b), comm=comm)


def _conv_bwd(proj, uc, dcat, w_dw, ln_g, ln_b, comm=None):
    t = proj.shape[0]
    nblk = t // ROWS
    pad = HALO - CONV_K // 2

    half = ROWS // 2

    def body(ac, ap, an, gc, gp, gn, uc_c, uc_p, uc_n, dy_c, dy_p, dy_n, w_ref, lg_ref, lb_ref,
             dag_ref, dw_ref, dvec_ref, uext, dext, ushift, dshift, dw_part):
        i = pl.program_id(0)
        _fill_ext(uext, _glu(ac, gc), _glu(ap, gp), _glu(an, gn), i, nblk)
        _shift_copies(uext, ushift)

        def ln_bwd(u_ref, d_ref):
            u = u_ref[...]
            mu = jnp.mean(u, axis=-1, keepdims=True)
            xc = u - mu
            rstd = lax.rsqrt(jnp.mean(xc * xc, axis=-1, keepdims=True) + LN_EPS)
            xhat = xc * rstd
            yln = xhat * lg_ref[...] + lb_ref[...]
            sg = jax.nn.sigmoid(yln)
            dyln = d_ref[...] * (sg * (1.0 + yln * (1.0 - sg)))
            dxh = dyln * lg_ref[...]
            du = rstd * (dxh - jnp.mean(dxh, axis=-1, keepdims=True)
                         - xhat * jnp.mean(dxh * xhat, axis=-1, keepdims=True))
            return du, dyln, xhat

        duc, dyln, xhat = ln_bwd(uc_c, dy_c)
        _fill_ext(dext, duc, ln_bwd(uc_p, dy_p)[0], ln_bwd(uc_n, dy_n)[0], i, nblk)
        _shift_copies(dext, dshift)

        @pl.when(i == 0)
        def _():
            dw_part[...] = jnp.zeros_like(dw_part)
            dvec_ref[...] = jnp.zeros_like(dvec_ref)

        dvec_ref[pl.ds(0, 1), :] += jnp.sum(duc, axis=0, keepdims=True)
        dvec_ref[pl.ds(1, 1), :] += jnp.sum(dyln * xhat, axis=0, keepdims=True)
        dvec_ref[pl.ds(2, 1), :] += jnp.sum(dyln, axis=0, keepdims=True)

        def input_grad(cc, carry):
            lanes = pl.ds(pl.multiple_of(cc * LANES, LANES), LANES)
            gate_lanes = pl.ds(pl.multiple_of(CONV_W + cc * LANES, LANES), LANES)
            for r0 in (0, half):
                du = jnp.zeros((half, LANES), F32)
                for j in range(CONV_K):
                    du = du + _window(dshift, 2 * HALO - pad - j, half, lanes, r0) * w_ref[pl.ds(j, 1), lanes]
                rows = pl.ds(r0, half)
                a = ac[rows, lanes].astype(F32)
                sg = jax.nn.sigmoid(gc[rows, lanes].astype(F32))
                dag_ref[rows, lanes] = (du * sg).astype(BF16)
                dag_ref[rows, gate_lanes] = (du * a * sg * (1.0 - sg)).astype(BF16)
            return carry

        def tap_grad(cc, carry):
            lanes = pl.ds(pl.multiple_of(cc * LANES, LANES), LANES)
            for r0 in (0, half):
                duc_c = dext[pl.ds(HALO + r0, half), lanes]
                parts = [jnp.sum((duc_c * _window(ushift, j + pad, half, lanes, r0))
                                 .reshape(half // SUBLANES, SUBLANES, LANES), axis=0) for j in range(CONV_K)]
                parts.append(jnp.zeros((SUBLANES, LANES), F32))
                dw_part[:, lanes] += jnp.concatenate(parts, axis=0)
            return carry

        lax.fori_loop(0, CONV_W // LANES, input_grad, 0)
        lax.fori_loop(0, CONV_W // LANES, tap_grad, 0)

        @pl.when(i == nblk - 1)
        def _():
            dw_ref[...] = jnp.sum(dw_part[...].reshape(32, SUBLANES, CONV_W), axis=1)

    vec = pl.BlockSpec((1, CONV_W), lambda i: (0, 0))
    return _call(
        body, name="conv_bwd", grid=(nblk,),
        in_specs=_halo_specs(t, 0, CONV_W) + _halo_specs(t, 1, CONV_W) + _halo_specs(t, 0, CONV_W)
        + _halo_specs(t, 0, CONV_W) + [pl.BlockSpec((32, CONV_W), lambda i: (0, 0)), vec, vec],
        out_specs=[pl.BlockSpec((ROWS, 2 * CONV_W), lambda i: (i, 0)),
                   pl.BlockSpec((32, CONV_W), lambda i: (0, 0)),
                   pl.BlockSpec((8, CONV_W), lambda i: (0, 0))],
        out_shape=[jax.ShapeDtypeStruct((t, 2 * CONV_W), BF16),
                   jax.ShapeDtypeStruct((32, CONV_W), F32),
                   jax.ShapeDtypeStruct((8, CONV_W), F32)],
        scratch_shapes=[EXT_SCRATCH, EXT_SCRATCH, SHIFT_SCRATCH, SHIFT_SCRATCH,
                        pltpu.VMEM((32 * SUBLANES, CONV_W), F32)],
        sem=("arbitrary",),
        args=(proj, proj, proj, proj, proj, proj, uc, uc, uc, dcat, dcat, dcat, w_dw, ln_g, ln_b), comm=comm)


Q_BLK, K_BLK, V_BLK = 16, 24, 32


def _head_masks():
    lane = lax.broadcasted_iota(jnp.int32, (1, 2 * HEAD_D), 1)
    return [lane < HEAD_D, lane >= HEAD_D]


def _band(r, rows):
    rs = jnp.clip(r - WIN_ROWS // 2, 0, rows - WIN_ROWS)
    return rs, r - rs, pl.multiple_of(r * GRID_W, GRID_W), pl.multiple_of(rs * GRID_W, GRID_W)


ROWS_PER_STEP = 2
NT_DIMS = (((1,), (1,)), ((), ()))
TN_DIMS = (((0,), (0,)), ((), ()))
N_DR = 16
GE_SPEC = pl.BlockSpec((2, N_DR, GRID_W, GRID_W), lambda h: (h, 0, 0, 0))
BIAS_SCRATCH = pltpu.VMEM((WIN_ROWS, 2 * GRID_W, BAND), F32)


def _stack_heads(block, masks):
    z = jnp.zeros_like(block)
    return jnp.concatenate([jnp.where(masks[0], block, z), jnp.where(masks[1], block, z)], axis=0)


def _unstack_heads(both, masks):
    return jnp.where(masks[0], both[:GRID_W], both[GRID_W:])


def _softmax(s, bias):
    s = s * (HEAD_D ** -0.5) + bias
    e = jnp.exp(s - jnp.max(s, axis=-1, keepdims=True))
    return e * (1.0 / jnp.sum(e, axis=-1, keepdims=True))


def _fill_bias(ge_ref, bias):
    wq = lax.broadcasted_iota(jnp.int32, (GRID_W, BAND), 0)
    wk = jnp.bitwise_and(lax.broadcasted_iota(jnp.int32, (GRID_W, BAND), 1), GRID_W - 1)
    cs = jnp.clip(wq - WIN_COLS // 2, 0, GRID_W - WIN_COLS)
    inside = jnp.logical_and(wk >= cs, wk < cs + WIN_COLS)
    for hh in range(2):
        for off in range(WIN_ROWS):
            tile = jnp.concatenate([ge_ref[hh, WIN_ROWS - 1 - off + kr] for kr in range(WIN_ROWS)], axis=-1)
            bias[off, pl.ds(hh * GRID_W, GRID_W), :] = jnp.where(inside, tile, NEG_INF)


def _natten_fwd(proj, ge, comm=None):
    t = proj.shape[0]
    rows = t // GRID_W

    def body(q_ref, k_ref, v_ref, ge_ref, o_ref, bias):
        masks = _head_masks()
        _fill_bias(ge_ref, bias)

        def step(i, carry):
            bands = [_band(i * ROWS_PER_STEP + u, rows) for u in range(ROWS_PER_STEP)]
            vbs = [v_ref[pl.ds(k0, BAND), :] for _, _, _, k0 in bands]
            ss = [lax.dot_general(_stack_heads(q_ref[pl.ds(q0, GRID_W), :], masks), k_ref[pl.ds(k0, BAND), :],
                                  NT_DIMS, preferred_element_type=F32) for _, _, q0, k0 in bands]
            ps = [_softmax(s, bias[off]) for s, (_, off, _, _) in zip(ss, bands)]
            os = [jnp.dot(p.astype(BF16), vb, preferred_element_type=F32) for p, vb in zip(ps, vbs)]
            for o, (_, _, q0, _) in zip(os, bands):
                o_ref[pl.ds(q0, GRID_W), :] = _unstack_heads(o, masks).astype(BF16)
            return carry

        lax.fori_loop(0, rows // ROWS_PER_STEP, step, 0)

    col = lambda b: pl.BlockSpec((t, 2 * HEAD_D), lambda h: (0, b + h))
    return _call(
        body, name="natten_fwd", grid=(N_HEADS // 2,),
        in_specs=[col(Q_BLK), col(K_BLK), col(V_BLK), GE_SPEC],
        out_specs=[pl.BlockSpec((t, 2 * HEAD_D), lambda h: (0, h))],
        out_shape=[jax.ShapeDtypeStruct((t, NA_W), BF16)],
        scratch_shapes=[BIAS_SCRATCH],
        sem=("parallel",), args=(proj, proj, proj, ge), comm=comm)


def _natten_bwd(proj, ge, dcat, comm=None):
    t = proj.shape[0]
    rows = t // GRID_W
    scale = HEAD_D ** -0.5

    def body(q_ref, k_ref, v_ref, ge_ref, do_ref, dq_ref, dk_ref, dv_ref, dge_ref, bias, dbias, dk_acc, dv_acc):
        masks = _head_masks()
        _fill_bias(ge_ref, bias)
        dk_acc[...] = jnp.zeros_like(dk_acc)
        dv_acc[...] = jnp.zeros_like(dv_acc)
        dbias[...] = jnp.zeros_like(dbias)

        def step(i, carry):
            bands = [_band(i * ROWS_PER_STEP + u, rows) for u in range(ROWS_PER_STEP)]
            kbs = [k_ref[pl.ds(k0, BAND), :] for _, _, _, k0 in bands]
            q2s = [_stack_heads(q_ref[pl.ds(q0, GRID_W), :], masks) for _, _, q0, _ in bands]
            do2s = [_stack_heads(do_ref[pl.ds(q0, GRID_W), :].astype(BF16), masks) for _, _, q0, _ in bands]
            ss = [lax.dot_general(q2, kb, NT_DIMS, preferred_element_type=F32) for q2, kb in zip(q2s, kbs)]
            dps = [lax.dot_general(do2, v_ref[pl.ds(k0, BAND), :], NT_DIMS, preferred_element_type=F32)
                   for do2, (_, _, _, k0) in zip(do2s, bands)]
            ps = [_softmax(s, bias[off]) for s, (_, off, _, _) in zip(ss, bands)]
            dss =[p * (dp - jnp.sum(p * dp, axis=-1, keepdims=True)) for p, dp in zip(ps, dps)]
            dsbs = [ds.astype(BF16) for ds in dss]
            dqs = [jnp.dot(dsb, kb, preferred_element_type=F32) for dsb, kb in zip(dsbs, kbs)]
            dks = [lax.dot_general(dsb, q2, TN_DIMS, preferred_element_type=F32) for dsb, q2 in zip(dsbs, q2s)]
            dvs = [lax.dot_general(p.astype(BF16), do2, TN_DIMS, preferred_element_type=F32)
                   for p, do2 in zip(ps, do2s)]
            for ds, dq, dk, dv, (_, off, q0, k0) in zip(dss, dqs, dks, dvs, bands):
                dbias[off] += ds
                dq_ref[pl.ds(q0, GRID_W), :] = (_unstack_heads(dq, masks) * scale).astype(BF16)
                dk_acc[pl.ds(k0, BAND), :] += dk * scale
                dv_acc[pl.ds(k0, BAND), :] += dv
            return carry

        lax.fori_loop(0, rows // ROWS_PER_STEP, step, 0)
        dk_ref[...] = dk_acc[...].astype(BF16)
        dv_ref[...] = dv_acc[...].astype(BF16)
        for hh in range(2):
            for dr in range(N_DR):
                pieces = [dbias[off, pl.ds(hh * GRID_W, GRID_W), pl.ds((dr + off - WIN_ROWS + 1) * GRID_W, GRID_W)]
                          for off in range(WIN_ROWS) if 0 <= dr + off - WIN_ROWS + 1 < WIN_ROWS]
                dge_ref[hh, dr] = functools.reduce(jnp.add, pieces) if pieces else jnp.zeros((GRID_W, GRID_W), F32)

    col = lambda b: pl.BlockSpec((t, 2 * HEAD_D), lambda h: (0, b + h))
    o_spec = pl.BlockSpec((t, 2 * HEAD_D), lambda h: (0, h))
    o_shape = jax.ShapeDtypeStruct((t, NA_W), BF16)
    return _call(
        body, name="natten_bwd", grid=(N_HEADS // 2,),
        in_specs=[col(Q_BLK), col(K_BLK), col(V_BLK), GE_SPEC, col(CONV_W // (2 * HEAD_D))],
        out_specs=[o_spec, o_spec, o_spec, GE_SPEC],
        out_shape=[o_shape, o_shape, o_shape, jax.ShapeDtypeStruct((N_HEADS, N_DR, GRID_W, GRID_W), F32)],
        scratch_shapes=[BIAS_SCRATCH, BIAS_SCRATCH, pltpu.VMEM((t, 2 * HEAD_D), F32),
                        pltpu.VMEM((t, 2 * HEAD_D), F32)],
        sem=("parallel",), args=(proj, proj, proj, ge, dcat), comm=comm)


def _column_offsets():
    c = np.arange(GRID_W)
    return (np.clip(c[None, :] - c[:, None], -(WIN_COLS - 1), WIN_COLS - 1) + (WIN_COLS - 1)).reshape(-1)


def _rpb_expand(rpb_l):
    n_dc = 32
    et = np.zeros((n_dc, GRID_W * GRID_W), np.float32)
    et[_column_offsets(), np.arange(GRID_W * GRID_W)] = 1.0
    rp = jnp.pad(rpb_l, ((0, 0), (0, N_DR - rpb_l.shape[1]), (0, n_dc - rpb_l.shape[2]))).reshape(N_HEADS * N_DR, n_dc)

    def body(r_ref, e_ref, o_ref):
        o_ref[...] = jnp.dot(r_ref[...], e_ref[...], preferred_element_type=F32, precision=lax.Precision.HIGHEST)

    vm = pl.BlockSpec(memory_space=pltpu.VMEM)
    ge = pl.pallas_call(
        body, name="rpb_expand", in_specs=[vm, vm], out_specs=vm,
        out_shape=jax.ShapeDtypeStruct((N_HEADS * N_DR, GRID_W * GRID_W), F32),
        compiler_params=pltpu.CompilerParams(vmem_limit_bytes=VMEM_LIMIT),
    )(rp, jnp.asarray(et))
    return ge.reshape(N_HEADS, N_DR, GRID_W, GRID_W)


def _rpb_grad(dge):
    e = np.zeros((GRID_W * GRID_W, 128), np.float32)
    e[np.arange(GRID_W * GRID_W), _column_offsets()] = 1.0

    def body(z_ref, e_ref, o_ref):
        o_ref[...] = jnp.dot(z_ref[...], e_ref[...], preferred_element_type=F32, precision=lax.Precision.HIGHEST)

    vm = pl.BlockSpec(memory_space=pltpu.VMEM)
    out = pl.pallas_call(
        body, name="rpb_grad", in_specs=[vm, vm], out_specs=vm,
        out_shape=jax.ShapeDtypeStruct((N_HEADS * N_DR, 128), F32),
        compiler_params=pltpu.CompilerParams(vmem_limit_bytes=VMEM_LIMIT),
    )(dge.reshape(N_HEADS * N_DR, GRID_W * GRID_W), jnp.asarray(e))
    return out.reshape(N_HEADS, N_DR, 128)[:, :2 * WIN_ROWS - 1, :2 * WIN_COLS - 1]


def _cast_bf16(w2d, *, name):
    r, c = w2d.shape
    tr = _tile(r, (512, 256))

    def body(w_ref, o_ref):
        o_ref[...] = w_ref[...].astype(BF16)

    spec = pl.BlockSpec((tr, c), lambda i: (i, 0))
    return pl.pallas_call(
        body, name=name, grid=(r // tr,),
        in_specs=[spec], out_specs=spec, out_shape=jax.ShapeDtypeStruct((r, c), BF16),
        compiler_params=_params("parallel"),
    )(w2d)


def _core_index():
    return lax.axis_index("c").astype(jnp.int32).reshape(1)


def _pair_sum(dw, other, *, name):
    s, r, c = dw.shape
    h = r // 2
    tr = _tile(h, (256, 128))
    nb = h // tr

    def body(c_ref, a_ref, b_ref, o_ref):
        del c_ref
        o_ref[...] = (a_ref[...].astype(F32) + b_ref[...].astype(F32)).astype(BF16)

    return pl.pallas_call(
        body, name=name,
        grid_spec=pltpu.PrefetchScalarGridSpec(
            num_scalar_prefetch=1, grid=(s, nb),
            in_specs=[pl.BlockSpec((None, tr, c), lambda si, i, cr: (si, cr[0] * nb + i, 0)),
                      pl.BlockSpec((None, tr, c), lambda si, i, cr: (si, i, 0))],
            out_specs=pl.BlockSpec((None, tr, c), lambda si, i, cr: (si, i, 0))),
        out_shape=jax.ShapeDtypeStruct((s, h, c), BF16),
        compiler_params=_params("parallel", "parallel"),
    )(_core_index(), dw, other)


def _chip_sum(parts, grad, layer, *, name):
    s, h, c = parts.shape
    tr = _tile(h, (256, 128))
    nb = h // tr

    def body(c_ref, p_ref, g_in, o_ref):
        del c_ref, g_in
        acc = p_ref[0].astype(F32) + p_ref[1].astype(F32)
        acc = acc + p_ref[2].astype(F32)
        o_ref[...] = acc + p_ref[3].astype(F32)

    return pl.pallas_call(
        body, name=name,
        grid_spec=pltpu.PrefetchScalarGridSpec(
            num_scalar_prefetch=1, grid=(nb,),
            in_specs=[pl.BlockSpec((s, tr, c), lambda i, cr: (0, i, 0)), ANY],
            out_specs=pl.BlockSpec((None, tr, c), lambda i, cr: (layer, cr[0] * nb + i, 0))),
        out_shape=jax.ShapeDtypeStruct(grad.shape, F32),
        input_output_aliases={2: 0},
        compiler_params=_params("parallel"),
    )(_core_index(), parts, grad)


def _adamw(w, g, m, v, *, name, comm=None):
    r, c = w.shape
    tr = _tile(r, (256, 128, 8))
    tr = tr if r % tr == 0 else r
    bc1 = 1.0 - ADAM_B1 ** ADAM_STEP
    bc2 = 1.0 - ADAM_B2 ** ADAM_STEP

    def body(w_ref, g_ref, m_ref, v_ref, go_ref, d_ref, mo_ref, vo_ref):
        gv = g_ref[...]
        mn = ADAM_B1 * m_ref[...] + (1.0 - ADAM_B1) * gv
        vn = ADAM_B2 * v_ref[...] + (1.0 - ADAM_B2) * (gv * gv)
        go_ref[...] = gv
        mo_ref[...] = mn
        vo_ref[...] = vn
        d_ref[...] = -ADAM_LR * ((mn / bc1) / (jnp.sqrt(vn / bc2) + ADAM_EPS) + ADAM_WD * w_ref[...])

    spec = pl.BlockSpec((tr, c), lambda i: (i, 0))
    shape = jax.ShapeDtypeStruct((r, c), F32)
    return _call(body, name=name, grid=(r // tr,), in_specs=[spec] * 4, out_specs=[spec] * 4, out_shape=[shape] * 4,
                 sem=("parallel",), args=(w, g, m, v), comm=comm)


def _me():
    return lax.axis_index("x"), lax.axis_index("y"), lax.axis_index("c")


def _other_chips(x, y):
    return [(1 - x, y), (x, 1 - y), (1 - x, 1 - y)]


def _remote(src, dst, send, recv, k, to):
    return pltpu.make_async_remote_copy(src_ref=src, dst_ref=dst, send_sem=send.at[k], recv_sem=recv.at[k],
                                        device_id=to, device_id_type=MESH)


def _run_comm(comm, *, name):
    def body(*refs):
        ni, no = len(comm.ins), len(comm.out_shape)
        ins, outs, sems = refs[:ni], refs[ni:ni + no], refs[ni + no:]
        comm.start(ins, outs, sems)
        comm.finish(ins, outs, sems)

    return pl.pallas_call(
        body, name=name,
        in_specs=[ANY] * len(comm.ins), out_specs=[ANY] * len(comm.out_shape),
        out_shape=list(comm.out_shape), scratch_shapes=list(comm.sems),
        input_output_aliases={i: i for i in range(len(comm.ins))} if comm.aliased else {},
    )(*comm.ins)


def _half(ref, chip, hc):
    hr = ref.shape[1] // 2
    return ref.at[chip, pl.ds(hc * hr, hr), :]


def _gather_comm(layer, shards):
    nw = len(shards)

    def sent(ins, outs, sems):
        send, recv = sems
        x, y, c = _me()
        me_chip = 2 * x + y
        first = []
        for w in range(nw):
            hr = outs[w].shape[1] // 2
            src = ins[w].at[layer, pl.ds(c * hr, hr), :]
            for j, chip in enumerate(_other_chips(x, y)):
                first.append(_remote(src, _half(outs[w], me_chip, c), send, recv, 3 * w + j, (*chip, c)))
        return first

    def start(ins, outs, sems):
        for cp in sent(ins, outs, sems):
            cp.start()

    def finish(ins, outs, sems):
        send, recv = sems
        first = sent(ins, outs, sems)
        x, y, c = _me()
        for w in range(nw):
            for j, chip in enumerate(_other_chips(x, y)):
                landed = _half(outs[w], 2 * chip[0] + chip[1], c)
                _remote(landed, landed, send, recv, 3 * w + j, (*chip, c)).wait_recv()
        for cp in first:
            cp.wait_send()

    return _Comm(list(shards), [jax.ShapeDtypeStruct((N_CHIPS,) + s.shape[1:], BF16) for s in shards],
                 [pltpu.SemaphoreType.DMA((3 * nw,)), pltpu.SemaphoreType.DMA((3 * nw,))], start, finish)


def _place_own(gathered, shards, layer, chip):
    return [lax.dynamic_update_slice(g, s[layer][None], (chip, 0, 0)) for g, s in zip(gathered, shards)]


def _forward_comm(gathered):
    nw = len(gathered)

    def sent(outs, sems):
        send, recv = sems
        x, y, c = _me()
        cps = []
        for w in range(nw):
            for j, chip in enumerate(_other_chips(x, y)):
                landed = _half(outs[w], 2 * chip[0] + chip[1], c)
                cps.append(_remote(landed, landed, send, recv, 3 * w + j, (x, y, 1 - c)))
        return cps

    def start(ins, outs, sems):
        for cp in sent(outs, sems):
            cp.start()

    def finish(ins, outs, sems):
        send, recv = sems
        x, y, c = _me()
        for w in range(nw):
            for j, chip in enumerate(_other_chips(x, y)):
                theirs = _half(outs[w], 2 * chip[0] + chip[1], 1 - c)
                _remote(theirs, theirs, send, recv, 3 * w + j, (x, y, 1 - c)).wait_recv()
        for cp in sent(outs, sems):
            cp.wait_send()

    return _Comm(list(gathered), [jax.ShapeDtypeStruct(g.shape, BF16) for g in gathered],
                 [pltpu.SemaphoreType.DMA((3 * nw,)), pltpu.SemaphoreType.DMA((3 * nw,))], start, finish,
                 aliased=True)


def _exchange_comm(dws):
    nw = len(dws)

    def copies(ins, outs, sems):
        send, recv = sems
        x, y, c = _me()
        cps = []
        for w in range(nw):
            hr = ins[w].shape[1] // 2
            src = ins[w].at[:, pl.ds((1 - c) * hr, hr), :]
            cps.append(_remote(src, outs[w], send, recv, w, (x, y, 1 - c)))
        return cps

    def start(ins, outs, sems):
        for cp in copies(ins, outs, sems):
            cp.start()

    def finish(ins, outs, sems):
        for cp in copies(ins, outs, sems):
            cp.wait()

    return _Comm(list(dws), [jax.ShapeDtypeStruct((d.shape[0], d.shape[1] // 2, d.shape[2]), BF16) for d in dws],
                 [pltpu.SemaphoreType.DMA((nw,)), pltpu.SemaphoreType.DMA((nw,))], start, finish)


def _scatter_comm(qs):
    nw = len(qs)

    def sent(ins, outs, sems):
        send, recv, local = sems
        x, y, c = _me()
        me_chip = 2 * x + y
        mine = [pltpu.make_async_copy(ins[w].at[me_chip], outs[w].at[me_chip], local.at[w]) for w in range(nw)]
        cps = []
        for w in range(nw):
            for j, chip in enumerate(_other_chips(x, y)):
                cps.append(_remote(ins[w].at[2 * chip[0] + chip[1]], outs[w].at[me_chip], send, recv,
                                   3 * w + j, (*chip, c)))
        return mine, cps

    def start(ins, outs, sems):
        mine, cps = sent(ins, outs, sems)
        for cp in mine + cps:
            cp.start()

    def finish(ins, outs, sems):
        send, recv, _ = sems
        mine, cps = sent(ins, outs, sems)
        x, y, c = _me()
        for w in range(nw):
            for j, chip in enumerate(_other_chips(x, y)):
                theirs = outs[w].at[2 * chip[0] + chip[1]]
                _remote(theirs, theirs, send, recv, 3 * w + j, (*chip, c)).wait_recv()
        for cp in cps:
            cp.wait_send()
        for cp in mine:
            cp.wait()

    return _Comm(list(qs), [jax.ShapeDtypeStruct(q.shape, BF16) for q in qs],
                 [pltpu.SemaphoreType.DMA((3 * nw,)), pltpu.SemaphoreType.DMA((3 * nw,)),
                  pltpu.SemaphoreType.DMA((nw,))], start, finish)


def _fill_comm(layer, grads):
    nw = len(grads)

    def sent(outs, sems):
        send, recv = sems
        x, y, c = _me()
        cps = []
        for w in range(nw):
            hr = outs[w].shape[1] // 2
            mine = outs[w].at[layer, pl.ds(c * hr, hr), :]
            cps.append(_remote(mine, mine, send, recv, w, (x, y, 1 - c)))
        return cps

    def start(ins, outs, sems):
        for cp in sent(outs, sems):
            cp.start()

    def finish(ins, outs, sems):
        send, recv = sems
        x, y, c = _me()
        for w in range(nw):
            hr = outs[w].shape[1] // 2
            theirs = outs[w].at[layer, pl.ds((1 - c) * hr, hr), :]
            _remote(theirs, theirs, send, recv, w, (x, y, 1 - c)).wait_recv()
        for cp in sent(outs, sems):
            cp.wait_send()

    return _Comm(list(grads), [jax.ShapeDtypeStruct(g.shape, F32) for g in grads],
                 [pltpu.SemaphoreType.DMA((nw,)), pltpu.SemaphoreType.DMA((nw,))], start, finish, aliased=True)


def _all_devices(pack, *, reduce, name):
    r = pack.shape[0]

    def body(p_ref, o_ref, buf, send, recv):
        x, y, c = _me()
        me = 4 * x + 2 * y + c
        buf[me] = p_ref[...]
        flips = [(fx, fy, fc) for fx in (0, 1) for fy in (0, 1) for fc in (0, 1)][1:]
        peers = [(x ^ fx, y ^ fy, c ^ fc) for fx, fy, fc in flips]
        cps = [_remote(p_ref, buf.at[me], send, recv, k, peer) for k, peer in enumerate(peers)]
        for cp in cps:
            cp.start()
        for k, (px, py, pc) in enumerate(peers):
            theirs = buf.at[4 * px + 2 * py + pc]
            _remote(theirs, theirs, send, recv, k, (px, py, pc)).wait_recv()
        for cp in cps:
            cp.wait_send()
        if reduce:
            acc = buf[0]
            for d in range(1, N_DEV):
                acc = acc + buf[d]
            o_ref[...] = acc
        else:
            o_ref[...] = buf[...]

    vm = pl.BlockSpec(memory_space=pltpu.VMEM)
    return pl.pallas_call(
        body, name=name,
        in_specs=[vm], out_specs=vm,
        out_shape=jax.ShapeDtypeStruct((r, 128) if reduce else (N_DEV, r, 128), F32),
        scratch_shapes=[pltpu.VMEM((N_DEV, r, 128), F32), pltpu.SemaphoreType.DMA((N_DEV - 1,)),
                        pltpu.SemaphoreType.DMA((N_DEV - 1,))],
        compiler_params=pltpu.CompilerParams(vmem_limit_bytes=VMEM_LIMIT),
    )(pack)


def _to_rows(flat):
    v = flat.reshape(-1)
    rows = -(-v.shape[0] // 1024) * 8
    return jnp.pad(v, (0, rows * 128 - v.shape[0])).reshape(rows, 128)


def kernel(x, w_in, w_dw, b_dw, conv_ln_g, conv_ln_b, rpb, w_out, w_up, w_down, pre_mix_g, post_mix_g, pre_mlp_g, post_mlp_g, loss_target, m_w_in, m_w_dw, m_b_dw, m_conv_ln_g, m_conv_ln_b, m_rpb, m_w_out, m_w_up, m_w_down, m_pre_mix_g, m_post_mix_g, m_pre_mlp_g, m_post_mlp_g, v_w_in, v_w_dw, v_b_dw, v_conv_ln_g, v_conv_ln_b, v_rpb, v_w_out, v_w_up, v_w_down, v_pre_mix_g, v_post_mix_g, v_pre_mlp_g, v_post_mlp_g):
    depth = w_in.shape[0]
    t = x.shape[1]
    xl = x.reshape(t, D_MODEL)
    target = loss_target.reshape(t, D_MODEL)
    chip = 2 * lax.axis_index("x") + lax.axis_index("y")

    big = {"w_in": w_in, "w_out": w_out, "w_up": w_up, "w_down": w_down}
    big_names = list(big)
    shards = [_cast_bf16(big[n].reshape(-1, big[n].shape[-1]), name=f"cast_{n}").reshape(big[n].shape)
              for n in big_names]

    wdw_all = _all_devices(_to_rows(w_dw), reduce=False, name="gather_w_dw")[::2]
    wdw_all = wdw_all.reshape(N_CHIPS, -1)[:, :w_dw.size].reshape((N_CHIPS,) + w_dw.shape)
    wdw_full = jnp.moveaxis(wdw_all, 0, 2).reshape(depth, CONV_K, CONV_W)
    wdw_pad = jnp.pad(wdw_full, ((0, 0), (0, 32 - CONV_K), (0, 0)))

    vec = lambda a, l: a[l].reshape(1, -1)

    saved = []
    h = _rms_fwd(xl, vec(pre_mix_g, 0), name="rms_first")
    (g_in,) = _place_own(_run_comm(_forward_comm(_run_comm(_gather_comm(0, shards[:1]), name="gather_w_in_first")),
                                   name="forward_w_in_first"), shards[:1], 0, chip)
    g_out = None
    for l in range(depth):
        more = l + 1 < depth
        bias = _rpb_expand(rpb[l])
        conv_args = (wdw_pad[l], vec(b_dw, l), vec(conv_ln_g, l), vec(conv_ln_b, l))
        if l == 0:
            proj, part = _mm_nn(h, g_in, out_dtype=BF16, name="proj", comm=_gather_comm(0, shards[1:2]))
            yc, uc, g_out = _conv_fwd(proj, *conv_args, comm=_forward_comm([part]))
            (g_out,) = _place_own([g_out], shards[1:2], 0, chip)
        else:
            (proj,) = _mm_nn(h, g_in, out_dtype=BF16, name="proj")
            yc, uc = _conv_fwd(proj, *conv_args)
        g_out = g_out.reshape(1, D_MODEL, D_MODEL)
        ya, part = _natten_fwd(proj, bias, _gather_comm(l, shards[2:3]))
        cat = jnp.concatenate([yc, ya], axis=1)
        mix, g_up = _mm_nn(cat, g_out, out_dtype=F32, name="out_proj", comm=_forward_comm([part]))
        (g_up,) = _place_own([g_up], shards[2:3], l, chip)
        x1, h2 = _resid_rms(xl, mix, vec(post_mix_g, l), vec(pre_mlp_g, l), name="mix_resid")
        up, act, part = _mm_nn(h2, g_up, out_dtype=BF16, relu2=True, name="mlp_up",
                               comm=_gather_comm(l, shards[3:]))
        (g_down,) = _place_own(_run_comm(_forward_comm([part]), name="forward_w_down"), shards[3:], l, chip)
        g_down = g_down.reshape(1, D_FF, D_MODEL)
        f, *parts = _mm_nn(act, g_down, out_dtype=F32, name="mlp_down",
                           comm=_gather_comm(l + 1, shards[:2]) if more else None)
        saved.append(dict(x=xl, h=h, proj=proj, uc=uc, cat=cat, mix=mix, x1=x1, h2=h2, up=up, act=act, f=f,
                          bias=bias, w=(g_in, g_out, g_up, g_down)))
        if more:
            xl, h, g_in, g_out = _resid_rms(x1, f, vec(post_mlp_g, l), vec(pre_mix_g, l + 1), name="mlp_resid",
                                            comm=_forward_comm(parts))
            g_in, g_out = _place_own([g_in, g_out], shards[:2], l + 1, chip)
        else:
            (xl,) = _resid_rms(x1, f, vec(post_mlp_g, l), None, name="mlp_resid_last")

    dx, loss_blk = _loss_head(xl, target)
    loss = lax.psum(loss_blk[0, 0], ("x", "y", "c"))

    grads = {n: lax.empty(big[n].shape, F32) for n in big_names}
    small = [None] * depth

    def chip_sum(n, layer, part):
        grads[n] = _chip_sum(part, grads[n], layer, name=f"chip_sum_{n}_{layer}")
        return _fill_comm(layer, [grads[n]])

    q_in = None
    for l in reversed(range(depth)):
        sv = saved[l]
        g_in, g_out, g_up, g_down = sv["w"]
        df, dg_post_mlp = _rms_bwd(dx, sv["f"], vec(post_mlp_g, l), None, out_dtype=BF16, name="rms_bwd_mlp_post")
        dup, *part = _mm_nt(df, g_down, out_dtype=BF16, up=sv["up"], name="d_act",
                            comm=_scatter_comm([q_in]) if q_in is not None else None)
        dw_down, *done = _mm_tn(sv["act"], df, 1, name="dw_down",
                                comm=chip_sum("w_in", l + 1, part[0]) if part else None)
        if done:
            grads["w_in"] = done[0]
        dw_down = dw_down.reshape(N_CHIPS, -1, D_MODEL)
        dh2, other = _mm_nt(dup, g_up, out_dtype=F32, name="d_h2", comm=_exchange_comm([dw_down]))
        q_down = _pair_sum(dw_down, other, name="pair_sum_w_down")
        dw_up, part = _mm_tn(sv["h2"], dup, N_CHIPS, name="dw_up", comm=_scatter_comm([q_down]))
        dx1, dg_pre_mlp, grads["w_down"] = _rms_bwd(dh2, sv["x1"], vec(pre_mlp_g, l), dx, out_dtype=F32,
                                                    name="rms_bwd_mlp_pre", comm=chip_sum("w_down", l, part))
        dmix, dg_post_mix, other = _rms_bwd(dx1, sv["mix"], vec(post_mix_g, l), None, out_dtype=BF16,
                                            name="rms_bwd_mix_post", comm=_exchange_comm([dw_up]))
        q_up = _pair_sum(dw_up, other, name="pair_sum_w_up")
        dw_out = _mm_tn(sv["cat"], dmix, 1, name="dw_out")[0].reshape(N_CHIPS, -1, D_MODEL)
        dcat, other = _mm_nt(dmix, g_out, out_dtype=F32, name="d_cat", comm=_exchange_comm([dw_out]))
        q_out = _pair_sum(dw_out, other, name="pair_sum_w_out")
        dq, dk, dv, dbias, part = _natten_bwd(sv["proj"], sv["bias"], dcat, _scatter_comm([q_up]))
        dag, dwdw, dvec, grads["w_up"] = _conv_bwd(sv["proj"], sv["uc"], dcat, wdw_pad[l], vec(conv_ln_g, l),
                                                   vec(conv_ln_b, l), comm=chip_sum("w_up", l, part))
        dproj = jnp.concatenate([dag, dq, dk, dv], axis=1)
        dh, part = _mm_nt(dproj, g_in, out_dtype=F32, name="d_h", comm=_scatter_comm([q_out]))
        dw_in, grads["w_out"] = _mm_tn(sv["h"], dproj, N_CHIPS, name="dw_in", comm=chip_sum("w_out", l, part))
        dx, dg_pre_mix, other = _rms_bwd(dh, sv["x"], vec(pre_mix_g, l), dx1, out_dtype=F32,
                                         name="rms_bwd_mix_pre", comm=_exchange_comm([dw_in]))
        q_in = _pair_sum(dw_in, other, name="pair_sum_w_in")

        drpb = _rpb_grad(dbias)
        small[l] = jnp.concatenate([
            dvec[0], dvec[1], dvec[2], dg_pre_mix[0], dg_post_mix[0], dg_pre_mlp[0], dg_post_mlp[0],
            drpb.reshape(-1), dwdw[:CONV_K].reshape(-1)])

    small_sum = _all_devices(_to_rows(jnp.stack(small)), reduce=True, name="reduce_small_grads")
    small_sum = small_sum.reshape(-1)[:depth * small[0].shape[0]].reshape(depth, -1)
    sizes = [CONV_W, CONV_W, CONV_W, D_MODEL, D_MODEL, D_MODEL, D_MODEL, rpb[0].size, CONV_K * CONV_W]
    offs = np.concatenate([[0], np.cumsum(sizes)])
    pieces = [small_sum[:, offs[i]:offs[i + 1]] for i in range(len(sizes))]
    g_small = {
        "b_dw": pieces[0], "conv_ln_g": pieces[1], "conv_ln_b": pieces[2],
        "pre_mix_g": pieces[3], "post_mix_g": pieces[4], "pre_mlp_g": pieces[5], "post_mlp_g": pieces[6],
        "rpb": pieces[7].reshape(rpb.shape),
        "w_dw": lax.dynamic_slice_in_dim(pieces[8].reshape(depth, CONV_K, CONV_W), chip * w_dw.shape[2],
                                         w_dw.shape[2], axis=2),
    }

    given = dict(w_in=(w_in, m_w_in, v_w_in), w_dw=(w_dw, m_w_dw, v_w_dw), b_dw=(b_dw, m_b_dw, v_b_dw),
                 conv_ln_g=(conv_ln_g, m_conv_ln_g, v_conv_ln_g), conv_ln_b=(conv_ln_b, m_conv_ln_b, v_conv_ln_b),
                 rpb=(rpb, m_rpb, v_rpb), w_out=(w_out, m_w_out, v_w_out), w_up=(w_up, m_w_up, v_w_up),
                 w_down=(w_down, m_w_down, v_w_down), pre_mix_g=(pre_mix_g, m_pre_mix_g, v_pre_mix_g),
                 post_mix_g=(post_mix_g, m_post_mix_g, v_post_mix_g), pre_mlp_g=(pre_mlp_g, m_pre_mlp_g, v_pre_mlp_g),
                 post_mlp_g=(post_mlp_g, m_post_mlp_g, v_post_mlp_g))
    results = {}

    def big_adamw(n, comm=None):
        w, m, v = given[n]
        flat = lambda a: a.reshape(-1, a.shape[-1])
        outs = _adamw(flat(w), flat(grads[n]), flat(m), flat(v), name=f"adamw_{n}", comm=comm)
        results[n] = [o.reshape(w.shape) for o in outs[:4]]
        return outs[4:]

    (part,) = big_adamw("w_down", _scatter_comm([q_in]))
    (grads["w_in"],) = big_adamw("w_up", chip_sum("w_in", 0, part))
    big_adamw("w_out")
    big_adamw("w_in")

    small_names = list(g_small)
    pack = lambda arrs: _to_rows(jnp.concatenate([a.reshape(-1) for a in arrs]))
    outs = _adamw(pack([given[n][0] for n in small_names]), pack([g_small[n] for n in small_names]),
                  pack([given[n][1] for n in small_names]), pack([given[n][2] for n in small_names]),
                  name="adamw_small")
    pos = 0
    for n in small_names:
        w = given[n][0]
        results[n] = [o.reshape(-1)[pos:pos + w.size].reshape(w.shape) for o in outs]
        pos += w.size

    order = ["w_in", "w_dw", "b_dw", "conv_ln_g", "conv_ln_b", "rpb", "w_out", "w_up", "w_down",
             "pre_mix_g", "post_mix_g", "pre_mlp_g", "post_mlp_g"]
    return (loss, dx.reshape(x.shape), *[results[n][0] for n in order], *[results[n][1] for n in order],
            *[results[n][2] for n in order], *[results[n][3] for n in order])
```

```python
import functools

import numpy as np
import jax
import jax.numpy as jnp
from jax import lax
from jax.experimental import pallas as pl
from jax.experimental.pallas import tpu as pltpu

F32 = jnp.float32
BF16 = jnp.bfloat16

D_MODEL = 2048
CONV_W = 1024
NA_W = 1024
N_HEADS = 16
HEAD_D = 64
GRID_W = 64
WIN_ROWS = 8
WIN_COLS = 16
CONV_K = 31
D_FF = 4 * D_MODEL
IN_COLS = 2 * CONV_W + 3 * NA_W
RMS_EPS = 1e-6
LN_EPS = 1e-5
NEG_INF = -1e30
N_CHIPS = 4
N_DEV = 8
HALO = 16
BAND = WIN_ROWS * GRID_W

ADAM_LR = 0.001
ADAM_B1 = 0.9
ADAM_B2 = 0.999
ADAM_EPS = 1e-08
ADAM_WD = 0.01
ADAM_STEP = 10

VMEM_LIMIT = 48 * 1024 * 1024
MESH = pl.DeviceIdType.MESH
ANY = pl.BlockSpec(memory_space=pl.ANY)


def _params(*sem):
    return pltpu.CompilerParams(dimension_semantics=sem, vmem_limit_bytes=VMEM_LIMIT)


class _Comm:
    def __init__(self, ins, out_shape, sems, start, finish, aliased=False):
        self.ins, self.out_shape, self.sems, self.start, self.finish = ins, out_shape, sems, start, finish
        self.aliased = aliased


def _call(body, *, name, grid, in_specs, out_specs, out_shape, scratch_shapes=(), sem, args, comm=None):
    in_specs, out_specs, out_shape = list(in_specs), list(out_specs), list(out_shape)
    scratch_shapes = list(scratch_shapes)
    if comm is None:
        return pl.pallas_call(body, name=name, grid=grid, in_specs=in_specs, out_specs=out_specs,
                              out_shape=out_shape, scratch_shapes=scratch_shapes,
                              compiler_params=_params(*sem))(*args)
    ni, no, ns = len(in_specs), len(out_specs), len(scratch_shapes)
    nci, nco = len(comm.ins), len(comm.out_shape)

    def full(*refs):
        ins, refs = refs[:ni], refs[ni:]
        cins, refs = refs[:nci], refs[nci:]
        outs, refs = refs[:no], refs[no:]
        couts, refs = refs[:nco], refs[nco:]
        scr, csems = refs[:ns], refs[ns:]
        ids = [pl.program_id(d) for d in range(len(grid))]
        first = functools.reduce(jnp.logical_and, [i == 0 for i in ids])
        last = functools.reduce(jnp.logical_and, [i == g - 1 for i, g in zip(ids, grid)])

        @pl.when(first)
        def _():
            comm.start(cins, couts, csems)

        body(*ins, *outs, *scr)

        @pl.when(last)
        def _():
            comm.finish(cins, couts, csems)

    return pl.pallas_call(
        full, name=name, grid=grid,
        in_specs=in_specs + [ANY] * nci, out_specs=out_specs + [ANY] * nco,
        out_shape=out_shape + list(comm.out_shape),
        scratch_shapes=scratch_shapes + list(comm.sems),
        input_output_aliases={ni + i: no + i for i in range(nci)} if comm.aliased else {},
        compiler_params=_params(*(["arbitrary"] * len(grid))),
    )(*args, *comm.ins)


K_TILES = (2048, 1280, 1024, 512)


def _tile(n, pref):
    for t in pref:
        if n % t == 0:
            return t
    return n


def _accumulate(acc, step, n_steps, finish, part):
    if n_steps == 1:
        finish(part())
        return

    @pl.when(step == 0)
    def _():
        acc[...] = part()

    @pl.when(jnp.logical_and(step > 0, step < n_steps - 1))
    def _():
        acc[...] += part()

    @pl.when(step == n_steps - 1)
    def _():
        finish(acc[...] + part())


def _mm_nn(a, w, *, out_dtype, relu2=False, name, comm=None):
    m, k = a.shape
    s, _, n = w.shape
    tm = _tile(m, (1024, 512, 256))
    tn = _tile(n, (1024, 1280, 512))
    tk = _tile(k, K_TILES)
    nps = n // tn
    nk = k // tk

    def body(a_ref, w_ref, *rest):
        outs, acc = rest[:-1], rest[-1]

        def finish(r):
            outs[0][...] = r.astype(outs[0].dtype)
            if relu2:
                p = jnp.maximum(r, 0.0)
                outs[1][...] = (p * p).astype(outs[1].dtype)

        _accumulate(acc, pl.program_id(2), nk, finish,
                    lambda: jnp.dot(a_ref[...], w_ref[...], preferred_element_type=F32))

    o_spec = pl.BlockSpec((tm, tn), lambda i, j, kk: (i, j))
    o_shape = jax.ShapeDtypeStruct((m, s * n), out_dtype)
    return _call(
        body, name=name,
        grid=(m // tm, s * nps, nk),
        in_specs=[pl.BlockSpec((tm, tk), lambda i, j, kk: (i, kk)),
                  pl.BlockSpec((None, tk, tn), lambda i, j, kk: (j // nps, kk, j % nps))],
        out_specs=[o_spec, o_spec] if relu2 else [o_spec],
        out_shape=[o_shape, o_shape] if relu2 else [o_shape],
        scratch_shapes=[pltpu.VMEM((tm, tn), F32)],
        sem=("parallel", "parallel", "arbitrary"), args=(a, w), comm=comm)


def _mm_nt(dy, w, *, out_dtype, up=None, name, comm=None):
    m = dy.shape[0]
    s, k, n = w.shape
    tm = _tile(m, (1024, 512, 256))
    tko = _tile(k, (1024, 512))
    tn = _tile(n, K_TILES)
    nps = n // tn
    nn = s * nps

    def body(dy_ref, w_ref, *rest):
        if up is None:
            o_ref, acc = rest
        else:
            up_ref, o_ref, acc = rest

        def finish(r):
            if up is not None:
                r = r * (2.0 * jnp.maximum(up_ref[...].astype(F32), 0.0))
            o_ref[...] = r.astype(o_ref.dtype)

        _accumulate(acc, pl.program_id(2), nn, finish,
                    lambda: lax.dot_general(dy_ref[...], w_ref[...], (((1,), (1,)), ((), ())),
                                            preferred_element_type=F32))

    in_specs = [pl.BlockSpec((tm, tn), lambda i, j, kk: (i, kk)),
                pl.BlockSpec((None, tko, tn), lambda i, j, kk: (kk // nps, j, kk % nps))]
    args = [dy, w]
    if up is not None:
        in_specs.append(pl.BlockSpec((tm, tko), lambda i, j, kk: (i, j)))
        args.append(up)
    return _call(
        body, name=name,
        grid=(m // tm, k // tko, nn),
        in_specs=in_specs,
        out_specs=[pl.BlockSpec((tm, tko), lambda i, j, kk: (i, j))],
        out_shape=[jax.ShapeDtypeStruct((m, k), out_dtype)],
        scratch_shapes=[pltpu.VMEM((tm, tko), F32)],
        sem=("parallel", "parallel", "arbitrary"), args=args, comm=comm)


def _mm_tn(a, dy, s, *, name, comm=None):
    m, k = a.shape
    n = dy.shape[1] // s
    tk = _tile(k, (1024, 512))
    tn = _tile(n, (1024, 1280, 512))
    tm = _tile(m, K_TILES)
    nps = n // tn
    nm = m // tm

    def body(a_ref, dy_ref, o_ref, acc):
        def finish(r):
            o_ref[...] = r.astype(o_ref.dtype)

        _accumulate(acc, pl.program_id(2), nm, finish,
                    lambda: lax.dot_general(a_ref[...], dy_ref[...], (((0,), (0,)), ((), ())),
                                            preferred_element_type=F32))

    return _call(
        body, name=name,
        grid=(k // tk, s * nps, nm),
        in_specs=[pl.BlockSpec((tm, tk), lambda i, j, mm: (mm, i)),
                  pl.BlockSpec((tm, tn), lambda i, j, mm: (mm, j))],
        out_specs=[pl.BlockSpec((None, tk, tn), lambda i, j, mm: (j // nps, i, j % nps))],
        out_shape=[jax.ShapeDtypeStruct((s, k, n), BF16)],
        scratch_shapes=[pltpu.VMEM((tk, tn), F32)],
        sem=("parallel", "parallel", "arbitrary"), args=(a, dy), comm=comm)


ROWS = 256


def _row_spec(d):
    return pl.BlockSpec((ROWS, d), lambda i: (i, 0))


def _vec_spec(d):
    return pl.BlockSpec((1, d), lambda i: (0, 0))


def _rstd(v):
    return lax.rsqrt(jnp.mean(v * v, axis=-1, keepdims=True) + RMS_EPS)


def _rms_fwd(x, g, *, name):
    t, d = x.shape

    def body(x_ref, g_ref, h_ref):
        xv = x_ref[...]
        h_ref[...] = ((xv * _rstd(xv)) * g_ref[...]).astype(BF16)

    return pl.pallas_call(
        body, name=name, grid=(t // ROWS,),
        in_specs=[_row_spec(d), _vec_spec(d)],
        out_specs=_row_spec(d),
        out_shape=jax.ShapeDtypeStruct((t, d), BF16),
        compiler_params=_params("parallel"),
    )(x, g)


def _resid_rms(res, y, g_post, g_next, *, name, comm=None):
    t, d = res.shape
    with_next = g_next is not None

    def body(res_ref, y_ref, gp_ref, *rest):
        yv = y_ref[...]
        xn = res_ref[...] + (yv * _rstd(yv)) * gp_ref[...]
        if with_next:
            gn_ref, xo_ref, h_ref = rest
            h_ref[...] = ((xn * _rstd(xn)) * gn_ref[...]).astype(BF16)
        else:
            (xo_ref,) = rest
        xo_ref[...] = xn

    in_specs = [_row_spec(d), _row_spec(d), _vec_spec(d)]
    args = [res, y, g_post]
    out_specs = [_row_spec(d)]
    out_shape = [jax.ShapeDtypeStruct((t, d), F32)]
    if with_next:
        in_specs.append(_vec_spec(d))
        args.append(g_next)
        out_specs.append(_row_spec(d))
        out_shape.append(jax.ShapeDtypeStruct((t, d), BF16))
    return _call(body, name=name, grid=(t // ROWS,), in_specs=in_specs, out_specs=out_specs, out_shape=out_shape,
                 sem=("parallel",), args=args, comm=comm)


def _rms_bwd(dy, xin, g, res, *, out_dtype, name, comm=None):
    t, d = xin.shape
    with_res = res is not None

    def body(dy_ref, x_ref, g_ref, *rest):
        if with_res:
            res_ref, dx_ref, dg_ref = rest
        else:
            dx_ref, dg_ref = rest
        xv = x_ref[...]
        r = _rstd(xv)
        nrm = xv * r
        dyv = dy_ref[...]
        dn = dyv * g_ref[...]
        dx = r * (dn - nrm * jnp.mean(dn * nrm, axis=-1, keepdims=True))
        if with_res:
            dx = dx + res_ref[...]
        dx_ref[...] = dx.astype(dx_ref.dtype)

        @pl.when(pl.program_id(0) == 0)
        def _():
            dg_ref[...] = jnp.zeros_like(dg_ref)

        dg_ref[...] += jnp.sum(dyv * nrm, axis=0, keepdims=True)

    in_specs = [_row_spec(d), _row_spec(d), _vec_spec(d)]
    args = [dy, xin, g]
    if with_res:
        in_specs.append(_row_spec(d))
        args.append(res)
    return _call(
        body, name=name, grid=(t // ROWS,),
        in_specs=in_specs,
        out_specs=[_row_spec(d), _vec_spec(d)],
        out_shape=[jax.ShapeDtypeStruct((t, d), out_dtype), jax.ShapeDtypeStruct((1, d), F32)],
        sem=("arbitrary",), args=args, comm=comm)


def _loss_head(y, target):
    t, d = y.shape

    def body(y_ref, t_ref, dy_ref, loss_ref):
        diff = y_ref[...] - t_ref[...]
        dy_ref[...] = diff * (1.0 / d)

        @pl.when(pl.program_id(0) == 0)
        def _():
            loss_ref[...] = jnp.zeros_like(loss_ref)

        loss_ref[...] += jnp.sum(diff * diff) * (0.5 / d)

    return pl.pallas_call(
        body, name="loss_head", grid=(t // ROWS,),
        in_specs=[_row_spec(d), _row_spec(d)],
        out_specs=(_row_spec(d), pl.BlockSpec((8, 128), lambda i: (0, 0))),
        out_shape=(jax.ShapeDtypeStruct((t, d), F32), jax.ShapeDtypeStruct((8, 128), F32)),
        compiler_params=_params("arbitrary"),
    )(y, target)


def _halo_specs(t, col, width):
    rb = ROWS // HALO
    last = t // HALO - 1
    return [pl.BlockSpec((ROWS, width), lambda i: (i, col)),
            pl.BlockSpec((HALO, width), lambda i: (jnp.maximum(i * rb - 1, 0), col)),
            pl.BlockSpec((HALO, width), lambda i: (jnp.minimum((i + 1) * rb, last), col))]


def _glu(a_ref, g_ref):
    return a_ref[...].astype(F32) * jax.nn.sigmoid(g_ref[...].astype(F32))


def _fill_ext(ext, cur, prev, nxt, i, nblk):
    ext[pl.ds(HALO, ROWS), :] = cur
    ext[pl.ds(0, HALO), :] = jnp.where(i > 0, prev, 0.0)
    ext[pl.ds(HALO + ROWS, HALO), :] = jnp.where(i < nblk - 1, nxt, 0.0)


LANES = 128
SUBLANES = 8
EXT_ROWS = ROWS + 2 * HALO
EXT_SCRATCH = pltpu.VMEM((EXT_ROWS, CONV_W), F32)
SHIFT_SCRATCH = pltpu.VMEM((SUBLANES, EXT_ROWS - SUBLANES, CONV_W), F32)


def _shift_copies(ext, shifted):
    for s in range(SUBLANES):
        shifted[s] = ext[pl.ds(s, EXT_ROWS - SUBLANES), :]


def _window(shifted, offset, rows, lanes, start=0):
    row0 = start + offset - offset % SUBLANES
    if not isinstance(row0, int):
        row0 = pl.multiple_of(row0, SUBLANES)
    return shifted[offset % SUBLANES, pl.ds(row0, rows), lanes]


def _conv_fwd(proj, w_dw, b_dw, ln_g, ln_b, comm=None):
    t = proj.shape[0]
    nblk = t // ROWS
    pad = HALO - CONV_K // 2

    def body(ac, ap, an, gc, gp, gn, w_ref, b_ref, lg_ref, lb_ref, yc_ref, uc_ref, uext, ushift):
        i = pl.program_id(0)
        _fill_ext(uext, _glu(ac, gc), _glu(ap, gp), _glu(an, gn), i, nblk)
        _shift_copies(uext, ushift)

        def chunk(cc, carry):
            lanes = pl.ds(pl.multiple_of(cc * LANES, LANES), LANES)
            acc = jnp.broadcast_to(b_ref[:, lanes], (ROWS, LANES))
            for j in range(CONV_K):
                acc = acc + _window(ushift, j + pad, ROWS, lanes) * w_ref[pl.ds(j, 1), lanes]
            uc_ref[:, lanes] = acc
            return carry

        lax.fori_loop(0, CONV_W // LANES, chunk, 0)
        acc = uc_ref[...]
        mu = jnp.mean(acc, axis=-1, keepdims=True)
        xc = acc - mu
        var = jnp.mean(xc * xc, axis=-1, keepdims=True)
        yln = xc * lax.rsqrt(var + LN_EPS) * lg_ref[...] + lb_ref[...]
        yc_ref[...] = (yln * jax.nn.sigmoid(yln)).astype(BF16)

    vec = pl.BlockSpec((1, CONV_W), lambda i: (0, 0))
    return _call(
        body, name="conv_fwd", grid=(nblk,),
        in_specs=_halo_specs(t, 0, CONV_W) + _halo_specs(t, 1, CONV_W)
        + [pl.BlockSpec((32, CONV_W), lambda i: (0, 0)), vec, vec, vec],
        out_specs=[pl.BlockSpec((ROWS, CONV_W), lambda i: (i, 0)),
                   pl.BlockSpec((ROWS, CONV_W), lambda i: (i, 0))],
        out_shape=[jax.ShapeDtypeStruct((t, CONV_W), BF16), jax.ShapeDtypeStruct((t, CONV_W), F32)],
        scratch_shapes=[EXT_SCRATCH, SHIFT_SCRATCH],
        sem=("parallel",), args=(proj, proj, proj, proj, proj, proj, w_dw, b_dw, ln_g, ln_b), comm=comm)


def _conv_bwd(proj, uc, dcat, w_dw, ln_g, ln_b, comm=None):
    t = proj.shape[0]
    nblk = t // ROWS
    pad = HALO - CONV_K // 2

    half = ROWS // 2

    def body(ac, ap, an, gc, gp, gn, uc_c, uc_p, uc_n, dy_c, dy_p, dy_n, w_ref, lg_ref, lb_ref,
             dag_ref, dw_ref, dvec_ref, uext, dext, ushift, dshift, dw_part):
        i = pl.program_id(0)
        _fill_ext(uext, _glu(ac, gc), _glu(ap, gp), _glu(an, gn), i, nblk)
        _shift_copies(uext, ushift)

        def ln_bwd(u_ref, d_ref):
            u = u_ref[...]
            mu = jnp.mean(u, axis=-1, keepdims=True)
            xc = u - mu
            rstd = lax.rsqrt(jnp.mean(xc * xc, axis=-1, keepdims=True) + LN_EPS)
            xhat = xc * rstd
            yln = xhat * lg_ref[...] + lb_ref[...]
            sg = jax.nn.sigmoid(yln)
            dyln = d_ref[...] * (sg * (1.0 + yln * (1.0 - sg)))
            dxh = dyln * lg_ref[...]
            du = rstd * (dxh - jnp.mean(dxh, axis=-1, keepdims=True)
                         - xhat * jnp.mean(dxh * xhat, axis=-1, keepdims=True))
            return du, dyln, xhat

        duc, dyln, xhat = ln_bwd(uc_c, dy_c)
        _fill_ext(dext, duc, ln_bwd(uc_p, dy_p)[0], ln_bwd(uc_n, dy_n)[0], i, nblk)
        _shift_copies(dext, dshift)

        @pl.when(i == 0)
        def _():
            dw_part[...] = jnp.zeros_like(dw_part)
            dvec_ref[...] = jnp.zeros_like(dvec_ref)

        dvec_ref[pl.ds(0, 1), :] += jnp.sum(duc, axis=0, keepdims=True)
        dvec_ref[pl.ds(1, 1), :] += jnp.sum(dyln * xhat, axis=0, keepdims=True)
        dvec_ref[pl.ds(2, 1), :] += jnp.sum(dyln, axis=0, keepdims=True)

        def input_grad(cc, carry):
            lanes = pl.ds(pl.multiple_of(cc * LANES, LANES), LANES)
            gate_lanes = pl.ds(pl.multiple_of(CONV_W + cc * LANES, LANES), LANES)
            for r0 in (0, half):
                du = jnp.zeros((half, LANES), F32)
                for j in range(CONV_K):
                    du = du + _window(dshift, 2 * HALO - pad - j, half, lanes, r0) * w_ref[pl.ds(j, 1), lanes]
                rows = pl.ds(r0, half)
                a = ac[rows, lanes].astype(F32)
                sg = jax.nn.sigmoid(gc[rows, lanes].astype(F32))
                dag_ref[rows, lanes] = (du * sg).astype(BF16)
                dag_ref[rows, gate_lanes] = (du * a * sg * (1.0 - sg)).astype(BF16)
            return carry

        def tap_grad(cc, carry):
            lanes = pl.ds(pl.multiple_of(cc * LANES, LANES), LANES)
            for r0 in (0, half):
                duc_c = dext[pl.ds(HALO + r0, half), lanes]
                parts = [jnp.sum((duc_c * _window(ushift, j + pad, half, lanes, r0))
                                 .reshape(half // SUBLANES, SUBLANES, LANES), axis=0) for j in range(CONV_K)]
                parts.append(jnp.zeros((SUBLANES, LANES), F32))
                dw_part[:, lanes] += jnp.concatenate(parts, axis=0)
            return carry

        lax.fori_loop(0, CONV_W // LANES, input_grad, 0)
        lax.fori_loop(0, CONV_W // LANES, tap_grad, 0)

        @pl.when(i == nblk - 1)
        def _():
            dw_ref[...] = jnp.sum(dw_part[...].reshape(32, SUBLANES, CONV_W), axis=1)

    vec = pl.BlockSpec((1, CONV_W), lambda i: (0, 0))
    return _call(
        body, name="conv_bwd", grid=(nblk,),
        in_specs=_halo_specs(t, 0, CONV_W) + _halo_specs(t, 1, CONV_W) + _halo_specs(t, 0, CONV_W)
        + _halo_specs(t, 0, CONV_W) + [pl.BlockSpec((32, CONV_W), lambda i: (0, 0)), vec, vec],
        out_specs=[pl.BlockSpec((ROWS, 2 * CONV_W), lambda i: (i, 0)),
                   pl.BlockSpec((32, CONV_W), lambda i: (0, 0)),
                   pl.BlockSpec((8, CONV_W), lambda i: (0, 0))],
        out_shape=[jax.ShapeDtypeStruct((t, 2 * CONV_W), BF16),
                   jax.ShapeDtypeStruct((32, CONV_W), F32),
                   jax.ShapeDtypeStruct((8, CONV_W), F32)],
        scratch_shapes=[EXT_SCRATCH, EXT_SCRATCH, SHIFT_SCRATCH, SHIFT_SCRATCH,
                        pltpu.VMEM((32 * SUBLANES, CONV_W), F32)],
        sem=("arbitrary",),
        args=(proj, proj, proj, proj, proj, proj, uc, uc, uc, dcat, dcat, dcat, w_dw, ln_g, ln_b), comm=comm)


Q_BLK, K_BLK, V_BLK = 16, 24, 32


def _head_masks():
    lane = lax.broadcasted_iota(jnp.int32, (1, 2 * HEAD_D), 1)
    return [lane < HEAD_D, lane >= HEAD_D]


def _band(r, rows):
    rs = jnp.clip(r - WIN_ROWS // 2, 0, rows - WIN_ROWS)
    return rs, r - rs, pl.multiple_of(r * GRID_W, GRID_W), pl.multiple_of(rs * GRID_W, GRID_W)


ROWS_PER_STEP = 4
NT_DIMS = (((1,), (1,)), ((), ()))
TN_DIMS = (((0,), (0,)), ((), ()))
N_DR = 16
GE_SPEC = pl.BlockSpec((2, N_DR, GRID_W, GRID_W), lambda h: (h, 0, 0, 0))
BIAS_SCRATCH = pltpu.VMEM((WIN_ROWS, 2 * GRID_W, BAND), F32)


def _stack_heads(block, masks):
    z = jnp.zeros_like(block)
    return jnp.concatenate([jnp.where(masks[0], block, z), jnp.where(masks[1], block, z)], axis=0)


def _unstack_heads(both, masks):
    return jnp.where(masks[0], both[:GRID_W], both[GRID_W:])


def _softmax(s, bias):
    s = s * (HEAD_D ** -0.5) + bias
    e = jnp.exp(s - jnp.max(s, axis=-1, keepdims=True))
    return e * (1.0 / jnp.sum(e, axis=-1, keepdims=True))


def _fill_bias(ge_ref, bias):
    wq = lax.broadcasted_iota(jnp.int32, (GRID_W, BAND), 0)
    wk = jnp.bitwise_and(lax.broadcasted_iota(jnp.int32, (GRID_W, BAND), 1), GRID_W - 1)
    cs = jnp.clip(wq - WIN_COLS // 2, 0, GRID_W - WIN_COLS)
    inside = jnp.logical_and(wk >= cs, wk < cs + WIN_COLS)
    for hh in range(2):
        for off in range(WIN_ROWS):
            tile = jnp.concatenate([ge_ref[hh, WIN_ROWS - 1 - off + kr] for kr in range(WIN_ROWS)], axis=-1)
            bias[off, pl.ds(hh * GRID_W, GRID_W), :] = jnp.where(inside, tile, NEG_INF)


def _natten_fwd(proj, ge, comm=None):
    t = proj.shape[0]
    rows = t // GRID_W

    def body(q_ref, k_ref, v_ref, ge_ref, o_ref, bias):
        masks = _head_masks()
        _fill_bias(ge_ref, bias)

        def step(i, carry):
            bands = [_band(i * ROWS_PER_STEP + u, rows) for u in range(ROWS_PER_STEP)]
            vbs = [v_ref[pl.ds(k0, BAND), :] for _, _, _, k0 in bands]
            ss = [lax.dot_general(_stack_heads(q_ref[pl.ds(q0, GRID_W), :], masks), k_ref[pl.ds(k0, BAND), :],
                                  NT_DIMS, preferred_element_type=F32) for _, _, q0, k0 in bands]
            ps = [_softmax(s, bias[off]) for s, (_, off, _, _) in zip(ss, bands)]
            os = [jnp.dot(p.astype(BF16), vb, preferred_element_type=F32) for p, vb in zip(ps, vbs)]
            for o, (_, _, q0, _) in zip(os, bands):
                o_ref[pl.ds(q0, GRID_W), :] = _unstack_heads(o, masks).astype(BF16)
            return carry

        lax.fori_loop(0, rows // ROWS_PER_STEP, step, 0)

    col = lambda b: pl.BlockSpec((t, 2 * HEAD_D), lambda h: (0, b + h))
    return _call(
        body, name="natten_fwd", grid=(N_HEADS // 2,),
        in_specs=[col(Q_BLK), col(K_BLK), col(V_BLK), GE_SPEC],
        out_specs=[pl.BlockSpec((t, 2 * HEAD_D), lambda h: (0, h))],
        out_shape=[jax.ShapeDtypeStruct((t, NA_W), BF16)],
        scratch_shapes=[BIAS_SCRATCH],
        sem=("parallel",), args=(proj, proj, proj, ge), comm=comm)


def _natten_bwd(proj, ge, dcat, comm=None):
    t = proj.shape[0]
    rows = t // GRID_W
    scale = HEAD_D ** -0.5

    def body(q_ref, k_ref, v_ref, ge_ref, do_ref, dq_ref, dk_ref, dv_ref, dge_ref, bias, dbias, dk_acc, dv_acc):
        masks = _head_masks()
        _fill_bias(ge_ref, bias)
        dk_acc[...] = jnp.zeros_like(dk_acc)
        dv_acc[...] = jnp.zeros_like(dv_acc)
        dbias[...] = jnp.zeros_like(dbias)

        def step(i, carry):
            bands = [_band(i * ROWS_PER_STEP + u, rows) for u in range(ROWS_PER_STEP)]
            kbs = [k_ref[pl.ds(k0, BAND), :] for _, _, _, k0 in bands]
            q2s = [_stack_heads(q_ref[pl.ds(q0, GRID_W), :], masks) for _, _, q0, _ in bands]
            do2s = [_stack_heads(do_ref[pl.ds(q0, GRID_W), :].astype(BF16), masks) for _, _, q0, _ in bands]
            ss = [lax.dot_general(q2, kb, NT_DIMS, preferred_element_type=F32) for q2, kb in zip(q2s, kbs)]
            dps = [lax.dot_general(do2, v_ref[pl.ds(k0, BAND), :], NT_DIMS, preferred_element_type=F32)
                   for do2, (_, _, _, k0) in zip(do2s, bands)]
            ps = [_softmax(s, bias[off]) for s, (_, off, _, _) in zip(ss, bands)]
            dss =[p * (dp - jnp.sum(p * dp, axis=-1, keepdims=True)) for p, dp in zip(ps, dps)]
            dsbs = [ds.astype(BF16) for ds in dss]
            dqs = [jnp.dot(dsb, kb, preferred_element_type=F32) for dsb, kb in zip(dsbs, kbs)]
            dks = [lax.dot_general(dsb, q2, TN_DIMS, preferred_element_type=F32) for dsb, q2 in zip(dsbs, q2s)]
            dvs = [lax.dot_general(p.astype(BF16), do2, TN_DIMS, preferred_element_type=F32)
                   for p, do2 in zip(ps, do2s)]
            for ds, dq, dk, dv, (_, off, q0, k0) in zip(dss, dqs, dks, dvs, bands):
                dbias[off] += ds
                dq_ref[pl.ds(q0, GRID_W), :] = (_unstack_heads(dq, masks) * scale).astype(BF16)
                dk_acc[pl.ds(k0, BAND), :] += dk * scale
                dv_acc[pl.ds(k0, BAND), :] += dv
            return carry

        lax.fori_loop(0, rows // ROWS_PER_STEP, step, 0)
        dk_ref[...] = dk_acc[...].astype(BF16)
        dv_ref[...] = dv_acc[...].astype(BF16)
        for hh in range(2):
            for dr in range(N_DR):
                pieces = [dbias[off, pl.ds(hh * GRID_W, GRID_W), pl.ds((dr + off - WIN_ROWS + 1) * GRID_W, GRID_W)]
                          for off in range(WIN_ROWS) if 0 <= dr + off - WIN_ROWS + 1 < WIN_ROWS]
                dge_ref[hh, dr] = functools.reduce(jnp.add, pieces) if pieces else jnp.zeros((GRID_W, GRID_W), F32)

    col = lambda b: pl.BlockSpec((t, 2 * HEAD_D), lambda h: (0, b + h))
    o_spec = pl.BlockSpec((t, 2 * HEAD_D), lambda h: (0, h))
    o_shape = jax.ShapeDtypeStruct((t, NA_W), BF16)
    return _call(
        body, name="natten_bwd", grid=(N_HEADS // 2,),
        in_specs=[col(Q_BLK), col(K_BLK), col(V_BLK), GE_SPEC, col(CONV_W // (2 * HEAD_D))],
        out_specs=[o_spec, o_spec, o_spec, GE_SPEC],
        out_shape=[o_shape, o_shape, o_shape, jax.ShapeDtypeStruct((N_HEADS, N_DR, GRID_W, GRID_W), F32)],
        scratch_shapes=[BIAS_SCRATCH, BIAS_SCRATCH, pltpu.VMEM((t, 2 * HEAD_D), F32),
                        pltpu.VMEM((t, 2 * HEAD_D), F32)],
        sem=("parallel",), args=(proj, proj, proj, ge, dcat), comm=comm)


def _column_offsets():
    c = np.arange(GRID_W)
    return (np.clip(c[None, :] - c[:, None], -(WIN_COLS - 1), WIN_COLS - 1) + (WIN_COLS - 1)).reshape(-1)


def _rpb_expand(rpb_l):
    n_dc = 32
    et = np.zeros((n_dc, GRID_W * GRID_W), np.float32)
    et[_column_offsets(), np.arange(GRID_W * GRID_W)] = 1.0
    rp = jnp.pad(rpb_l, ((0, 0), (0, N_DR - rpb_l.shape[1]), (0, n_dc - rpb_l.shape[2]))).reshape(N_HEADS * N_DR, n_dc)

    def body(r_ref, e_ref, o_ref):
        o_ref[...] = jnp.dot(r_ref[...], e_ref[...], preferred_element_type=F32, precision=lax.Precision.HIGHEST)

    vm = pl.BlockSpec(memory_space=pltpu.VMEM)
    ge = pl.pallas_call(
        body, name="rpb_expand", in_specs=[vm, vm], out_specs=vm,
        out_shape=jax.ShapeDtypeStruct((N_HEADS * N_DR, GRID_W * GRID_W), F32),
        compiler_params=pltpu.CompilerParams(vmem_limit_bytes=VMEM_LIMIT),
    )(rp, jnp.asarray(et))
    return ge.reshape(N_HEADS, N_DR, GRID_W, GRID_W)


def _rpb_grad(dge):
    e = np.zeros((GRID_W * GRID_W, 128), np.float32)
    e[np.arange(GRID_W * GRID_W), _column_offsets()] = 1.0

    def body(z_ref, e_ref, o_ref):
        o_ref[...] = jnp.dot(z_ref[...], e_ref[...], preferred_element_type=F32, precision=lax.Precision.HIGHEST)

    vm = pl.BlockSpec(memory_space=pltpu.VMEM)
    out = pl.pallas_call(
        body, name="rpb_grad", in_specs=[vm, vm], out_specs=vm,
        out_shape=jax.ShapeDtypeStruct((N_HEADS * N_DR, 128), F32),
        compiler_params=pltpu.CompilerParams(vmem_limit_bytes=VMEM_LIMIT),
    )(dge.reshape(N_HEADS * N_DR, GRID_W * GRID_W), jnp.asarray(e))
    return out.reshape(N_HEADS, N_DR, 128)[:, :2 * WIN_ROWS - 1, :2 * WIN_COLS - 1]


def _cast_bf16(w2d, *, name):
    r, c = w2d.shape
    tr = _tile(r, (512, 256))

    def body(w_ref, o_ref):
        o_ref[...] = w_ref[...].astype(BF16)

    spec = pl.BlockSpec((tr, c), lambda i: (i, 0))
    return pl.pallas_call(
        body, name=name, grid=(r // tr,),
        in_specs=[spec], out_specs=spec, out_shape=jax.ShapeDtypeStruct((r, c), BF16),
        compiler_params=_params("parallel"),
    )(w2d)


def _core_index():
    return lax.axis_index("c").astype(jnp.int32).reshape(1)


def _pair_sum(dw, other, *, name):
    s, r, c = dw.shape
    h = r // 2
    tr = _tile(h, (256, 128))
    nb = h // tr

    def body(c_ref, a_ref, b_ref, o_ref):
        del c_ref
        o_ref[...] = (a_ref[...].astype(F32) + b_ref[...].astype(F32)).astype(BF16)

    return pl.pallas_call(
        body, name=name,
        grid_spec=pltpu.PrefetchScalarGridSpec(
            num_scalar_prefetch=1, grid=(s, nb),
            in_specs=[pl.BlockSpec((None, tr, c), lambda si, i, cr: (si, cr[0] * nb + i, 0)),
                      pl.BlockSpec((None, tr, c), lambda si, i, cr: (si, i, 0))],
            out_specs=pl.BlockSpec((None, tr, c), lambda si, i, cr: (si, i, 0))),
        out_shape=jax.ShapeDtypeStruct((s, h, c), BF16),
        compiler_params=_params("parallel", "parallel"),
    )(_core_index(), dw, other)


def _chip_sum(parts, grad, layer, *, name):
    s, h, c = parts.shape
    tr = _tile(h, (256, 128))
    nb = h // tr

    def body(c_ref, p_ref, g_in, o_ref):
        del c_ref, g_in
        acc = p_ref[0].astype(F32) + p_ref[1].astype(F32)
        acc = acc + p_ref[2].astype(F32)
        o_ref[...] = acc + p_ref[3].astype(F32)

    return pl.pallas_call(
        body, name=name,
        grid_spec=pltpu.PrefetchScalarGridSpec(
            num_scalar_prefetch=1, grid=(nb,),
            in_specs=[pl.BlockSpec((s, tr, c), lambda i, cr: (0, i, 0)), ANY],
            out_specs=pl.BlockSpec((None, tr, c), lambda i, cr: (layer, cr[0] * nb + i, 0))),
        out_shape=jax.ShapeDtypeStruct(grad.shape, F32),
        input_output_aliases={2: 0},
        compiler_params=_params("parallel"),
    )(_core_index(), parts, grad)


def _adamw(w, g, m, v, *, name):
    r, c = w.shape
    tr = _tile(r, (256, 128, 8))
    tr = tr if r % tr == 0 else r
    bc1 = 1.0 - ADAM_B1 ** ADAM_STEP
    bc2 = 1.0 - ADAM_B2 ** ADAM_STEP

    def body(w_ref, g_ref, m_ref, v_ref, go_ref, d_ref, mo_ref, vo_ref):
        gv = g_ref[...]
        mn = ADAM_B1 * m_ref[...] + (1.0 - ADAM_B1) * gv
        vn = ADAM_B2 * v_ref[...] + (1.0 - ADAM_B2) * (gv * gv)
        go_ref[...] = gv
        mo_ref[...] = mn
        vo_ref[...] = vn
        d_ref[...] = -ADAM_LR * ((mn / bc1) / (jnp.sqrt(vn / bc2) + ADAM_EPS) + ADAM_WD * w_ref[...])

    spec = pl.BlockSpec((tr, c), lambda i: (i, 0))
    shape = jax.ShapeDtypeStruct((r, c), F32)
    return _call(body, name=name, grid=(r // tr,), in_specs=[spec] * 4, out_specs=[spec] * 4, out_shape=[shape] * 4,
                 sem=("parallel",), args=(w, g, m, v))


def _me():
    return lax.axis_index("x"), lax.axis_index("y"), lax.axis_index("c")


def _other_chips(x, y):
    return [(1 - x, y), (x, 1 - y), (1 - x, 1 - y)]


def _remote(src, dst, send, recv, k, to):
    return pltpu.make_async_remote_copy(src_ref=src, dst_ref=dst, send_sem=send.at[k], recv_sem=recv.at[k],
                                        device_id=to, device_id_type=MESH)


def _run_comm(comm, *, name):
    def body(*refs):
        ni, no = len(comm.ins), len(comm.out_shape)
        ins, outs, sems = refs[:ni], refs[ni:ni + no], refs[ni + no:]
        comm.start(ins, outs, sems)
        comm.finish(ins, outs, sems)

    return pl.pallas_call(
        body, name=name,
        in_specs=[ANY] * len(comm.ins), out_specs=[ANY] * len(comm.out_shape),
        out_shape=list(comm.out_shape), scratch_shapes=list(comm.sems),
        input_output_aliases={i: i for i in range(len(comm.ins))} if comm.aliased else {},
    )(*comm.ins)


def _half(ref, chip, hc):
    hr = ref.shape[1] // 2
    return ref.at[chip, pl.ds(hc * hr, hr), :]


def _gather_comm(layer, shards):
    nw = len(shards)

    def sent(ins, outs, sems):
        send, recv = sems
        x, y, c = _me()
        me_chip = 2 * x + y
        first = []
        for w in range(nw):
            hr = outs[w].shape[1] // 2
            src = ins[w].at[layer, pl.ds(c * hr, hr), :]
            for j, chip in enumerate(_other_chips(x, y)):
                first.append(_remote(src, _half(outs[w], me_chip, c), send, recv, 3 * w + j, (*chip, c)))
        return first

    def start(ins, outs, sems):
        for cp in sent(ins, outs, sems):
            cp.start()

    def finish(ins, outs, sems):
        send, recv = sems
        first = sent(ins, outs, sems)
        x, y, c = _me()
        for w in range(nw):
            for j, chip in enumerate(_other_chips(x, y)):
                landed = _half(outs[w], 2 * chip[0] + chip[1], c)
                _remote(landed, landed, send, recv, 3 * w + j, (*chip, c)).wait_recv()
        for cp in first:
            cp.wait_send()

    return _Comm(list(shards), [jax.ShapeDtypeStruct((N_CHIPS,) + s.shape[1:], BF16) for s in shards],
                 [pltpu.SemaphoreType.DMA((3 * nw,)), pltpu.SemaphoreType.DMA((3 * nw,))], start, finish)


def _place_own(gathered, shards, layer, chip):
    return [lax.dynamic_update_slice(g, s[layer][None], (chip, 0, 0)) for g, s in zip(gathered, shards)]


def _forward_comm(gathered):
    nw = len(gathered)

    def sent(outs, sems):
        send, recv = sems
        x, y, c = _me()
        cps = []
        for w in range(nw):
            for j, chip in enumerate(_other_chips(x, y)):
                landed = _half(outs[w], 2 * chip[0] + chip[1], c)
                cps.append(_remote(landed, landed, send, recv, 3 * w + j, (x, y, 1 - c)))
        return cps

    def start(ins, outs, sems):
        for cp in sent(outs, sems):
            cp.start()

    def finish(ins, outs, sems):
        send, recv = sems
        x, y, c = _me()
        for w in range(nw):
            for j, chip in enumerate(_other_chips(x, y)):
                theirs = _half(outs[w], 2 * chip[0] + chip[1], 1 - c)
                _remote(theirs, theirs, send, recv, 3 * w + j, (x, y, 1 - c)).wait_recv()
        for cp in sent(outs, sems):
            cp.wait_send()

    return _Comm(list(gathered), [jax.ShapeDtypeStruct(g.shape, BF16) for g in gathered],
                 [pltpu.SemaphoreType.DMA((3 * nw,)), pltpu.SemaphoreType.DMA((3 * nw,))], start, finish,
                 aliased=True)


def _exchange_comm(dws):
    nw = len(dws)

    def copies(ins, outs, sems):
        send, recv = sems
        x, y, c = _me()
        cps = []
        for w in range(nw):
            hr = ins[w].shape[1] // 2
            src = ins[w].at[:, pl.ds((1 - c) * hr, hr), :]
            cps.append(_remote(src, outs[w], send, recv, w, (x, y, 1 - c)))
        return cps

    def start(ins, outs, sems):
        for cp in copies(ins, outs, sems):
            cp.start()

    def finish(ins, outs, sems):
        for cp in copies(ins, outs, sems):
            cp.wait()

    return _Comm(list(dws), [jax.ShapeDtypeStruct((d.shape[0], d.shape[1] // 2, d.shape[2]), BF16) for d in dws],
                 [pltpu.SemaphoreType.DMA((nw,)), pltpu.SemaphoreType.DMA((nw,))], start, finish)


def _scatter_comm(qs):
    nw = len(qs)

    def sent(ins, outs, sems):
        send, recv, local = sems
        x, y, c = _me()
        me_chip = 2 * x + y
        mine = [pltpu.make_async_copy(ins[w].at[me_chip], outs[w].at[me_chip], local.at[w]) for w in range(nw)]
        cps = []
        for w in range(nw):
            for j, chip in enumerate(_other_chips(x, y)):
                cps.append(_remote(ins[w].at[2 * chip[0] + chip[1]], outs[w].at[me_chip], send, recv,
                                   3 * w + j, (*chip, c)))
        return mine, cps

    def start(ins, outs, sems):
        mine, cps = sent(ins, outs, sems)
        for cp in mine + cps:
            cp.start()

    def finish(ins, outs, sems):
        send, recv, _ = sems
        mine, cps = sent(ins, outs, sems)
        x, y, c = _me()
        for w in range(nw):
            for j, chip in enumerate(_other_chips(x, y)):
                theirs = outs[w].at[2 * chip[0] + chip[1]]
                _remote(theirs, theirs, send, recv, 3 * w + j, (*chip, c)).wait_recv()
        for cp in cps:
            cp.wait_send()
        for cp in mine:
            cp.wait()

    return _Comm(list(qs), [jax.ShapeDtypeStruct(q.shape, BF16) for q in qs],
                 [pltpu.SemaphoreType.DMA((3 * nw,)), pltpu.SemaphoreType.DMA((3 * nw,)),
                  pltpu.SemaphoreType.DMA((nw,))], start, finish)


def _fill_comm(layer, grads):
    nw = len(grads)

    def sent(outs, sems):
        send, recv = sems
        x, y, c = _me()
        cps = []
        for w in range(nw):
            hr = outs[w].shape[1] // 2
            mine = outs[w].at[layer, pl.ds(c * hr, hr), :]
            cps.append(_remote(mine, mine, send, recv, w, (x, y, 1 - c)))
        return cps

    def start(ins, outs, sems):
        for cp in sent(outs, sems):
            cp.start()

    def finish(ins, outs, sems):
        send, recv = sems
        x, y, c = _me()
        for w in range(nw):
            hr = outs[w].shape[1] // 2
            theirs = outs[w].at[layer, pl.ds((1 - c) * hr, hr), :]
            _remote(theirs, theirs, send, recv, w, (x, y, 1 - c)).wait_recv()
        for cp in sent(outs, sems):
            cp.wait_send()

    return _Comm(list(grads), [jax.ShapeDtypeStruct(g.shape, F32) for g in grads],
                 [pltpu.SemaphoreType.DMA((nw,)), pltpu.SemaphoreType.DMA((nw,))], start, finish, aliased=True)


def _all_devices(pack, *, reduce, name):
    r = pack.shape[0]

    def body(p_ref, o_ref, buf, send, recv):
        x, y, c = _me()
        me = 4 * x + 2 * y + c
        buf[me] = p_ref[...]
        flips = [(fx, fy, fc) for fx in (0, 1) for fy in (0, 1) for fc in (0, 1)][1:]
        peers = [(x ^ fx, y ^ fy, c ^ fc) for fx, fy, fc in flips]
        cps = [_remote(p_ref, buf.at[me], send, recv, k, peer) for k, peer in enumerate(peers)]
        for cp in cps:
            cp.start()
        for k, (px, py, pc) in enumerate(peers):
            theirs = buf.at[4 * px + 2 * py + pc]
            _remote(theirs, theirs, send, recv, k, (px, py, pc)).wait_recv()
        for cp in cps:
            cp.wait_send()
        if reduce:
            acc = buf[0]
            for d in range(1, N_DEV):
                acc = acc + buf[d]
            o_ref[...] = acc
        else:
            o_ref[...] = buf[...]

    vm = pl.BlockSpec(memory_space=pltpu.VMEM)
    return pl.pallas_call(
        body, name=name,
        in_specs=[vm], out_specs=vm,
        out_shape=jax.ShapeDtypeStruct((r, 128) if reduce else (N_DEV, r, 128), F32),
        scratch_shapes=[pltpu.VMEM((N_DEV, r, 128), F32), pltpu.SemaphoreType.DMA((N_DEV - 1,)),
                        pltpu.SemaphoreType.DMA((N_DEV - 1,))],
        compiler_params=pltpu.CompilerParams(vmem_limit_bytes=VMEM_LIMIT),
    )(pack)


def _to_rows(flat):
    v = flat.reshape(-1)
    rows = -(-v.shape[0] // 1024) * 8
    return jnp.pad(v, (0, rows * 128 - v.shape[0])).reshape(rows, 128)


def kernel(x, w_in, w_dw, b_dw, conv_ln_g, conv_ln_b, rpb, w_out, w_up, w_down, pre_mix_g, post_mix_g, pre_mlp_g, post_mlp_g, loss_target, m_w_in, m_w_dw, m_b_dw, m_conv_ln_g, m_conv_ln_b, m_rpb, m_w_out, m_w_up, m_w_down, m_pre_mix_g, m_post_mix_g, m_pre_mlp_g, m_post_mlp_g, v_w_in, v_w_dw, v_b_dw, v_conv_ln_g, v_conv_ln_b, v_rpb, v_w_out, v_w_up, v_w_down, v_pre_mix_g, v_post_mix_g, v_pre_mlp_g, v_post_mlp_g):
    depth = w_in.shape[0]
    t = x.shape[1]
    xl = x.reshape(t, D_MODEL)
    target = loss_target.reshape(t, D_MODEL)
    chip = 2 * lax.axis_index("x") + lax.axis_index("y")

    big = {"w_in": w_in, "w_out": w_out, "w_up": w_up, "w_down": w_down}
    big_names = list(big)
    shards = [_cast_bf16(big[n].reshape(-1, big[n].shape[-1]), name=f"cast_{n}").reshape(big[n].shape)
              for n in big_names]

    wdw_all = _all_devices(_to_rows(w_dw), reduce=False, name="gather_w_dw")[::2]
    wdw_all = wdw_all.reshape(N_CHIPS, -1)[:, :w_dw.size].reshape((N_CHIPS,) + w_dw.shape)
    wdw_full = jnp.moveaxis(wdw_all, 0, 2).reshape(depth, CONV_K, CONV_W)
    wdw_pad = jnp.pad(wdw_full, ((0, 0), (0, 32 - CONV_K), (0, 0)))

    vec = lambda a, l: a[l].reshape(1, -1)

    saved = []
    h = _rms_fwd(xl, vec(pre_mix_g, 0), name="rms_first")
    (g_in,) = _place_own(_run_comm(_forward_comm(_run_comm(_gather_comm(0, shards[:1]), name="gather_w_in_first")),
                                   name="forward_w_in_first"), shards[:1], 0, chip)
    g_out = None
    for l in range(depth):
        more = l + 1 < depth
        bias = _rpb_expand(rpb[l])
        conv_args = (wdw_pad[l], vec(b_dw, l), vec(conv_ln_g, l), vec(conv_ln_b, l))
        if l == 0:
            proj, part = _mm_nn(h, g_in, out_dtype=BF16, name="proj", comm=_gather_comm(0, shards[1:2]))
            yc, uc, g_out = _conv_fwd(proj, *conv_args, comm=_forward_comm([part]))
            (g_out,) = _place_own([g_out], shards[1:2], 0, chip)
        else:
            (proj,) = _mm_nn(h, g_in, out_dtype=BF16, name="proj")
            yc, uc = _conv_fwd(proj, *conv_args)
        g_out = g_out.reshape(1, D_MODEL, D_MODEL)
        ya, part = _natten_fwd(proj, bias, _gather_comm(l, shards[2:3]))
        cat = jnp.concatenate([yc, ya], axis=1)
        mix, g_up = _mm_nn(cat, g_out, out_dtype=F32, name="out_proj", comm=_forward_comm([part]))
        (g_up,) = _place_own([g_up], shards[2:3], l, chip)
        x1, h2 = _resid_rms(xl, mix, vec(post_mix_g, l), vec(pre_mlp_g, l), name="mix_resid")
        up, act, part = _mm_nn(h2, g_up, out_dtype=BF16, relu2=True, name="mlp_up",
                               comm=_gather_comm(l, shards[3:]))
        (g_down,) = _place_own(_run_comm(_forward_comm([part]), name="forward_w_down"), shards[3:], l, chip)
        g_down = g_down.reshape(1, D_FF, D_MODEL)
        f, *parts = _mm_nn(act, g_down, out_dtype=F32, name="mlp_down",
                           comm=_gather_comm(l + 1, shards[:2]) if more else None)
        saved.append(dict(x=xl, h=h, proj=proj, uc=uc, cat=cat, mix=mix, x1=x1, h2=h2, up=up, act=act, f=f,
                          bias=bias, w=(g_in, g_out, g_up, g_down)))
        if more:
            xl, h, g_in, g_out = _resid_rms(x1, f, vec(post_mlp_g, l), vec(pre_mix_g, l + 1), name="mlp_resid",
                                            comm=_forward_comm(parts))
            g_in, g_out = _place_own([g_in, g_out], shards[:2], l + 1, chip)
        else:
            (xl,) = _resid_rms(x1, f, vec(post_mlp_g, l), None, name="mlp_resid_last")

    dx, loss_blk = _loss_head(xl, target)
    loss = lax.psum(loss_blk[0, 0], ("x", "y", "c"))

    grads = {n: lax.empty(big[n].shape, F32) for n in big_names}
    small = [None] * depth

    def chip_sum(n, layer, part):
        grads[n] = _chip_sum(part, grads[n], layer, name=f"chip_sum_{n}_{layer}")
        return _fill_comm(layer, [grads[n]])

    q_in = None
    for l in reversed(range(depth)):
        sv = saved[l]
        g_in, g_out, g_up, g_down = sv["w"]
        df, dg_post_mlp = _rms_bwd(dx, sv["f"], vec(post_mlp_g, l), None, out_dtype=BF16, name="rms_bwd_mlp_post")
        dup, *part = _mm_nt(df, g_down, out_dtype=BF16, up=sv["up"], name="d_act",
                            comm=_scatter_comm([q_in]) if q_in is not None else None)
        dw_down, *done = _mm_tn(sv["act"], df, 1, name="dw_down",
                                comm=chip_sum("w_in", l + 1, part[0]) if part else None)
        if done:
            grads["w_in"] = done[0]
        dw_down = dw_down.reshape(N_CHIPS, -1, D_MODEL)
        dh2, other = _mm_nt(dup, g_up, out_dtype=F32, name="d_h2", comm=_exchange_comm([dw_down]))
        q_down = _pair_sum(dw_down, other, name="pair_sum_w_down")
        dw_up, part = _mm_tn(sv["h2"], dup, N_CHIPS, name="dw_up", comm=_scatter_comm([q_down]))
        dx1, dg_pre_mlp, grads["w_down"] = _rms_bwd(dh2, sv["x1"], vec(pre_mlp_g, l), dx, out_dtype=F32,
                                                    name="rms_bwd_mlp_pre", comm=chip_sum("w_down", l, part))
        dmix, dg_post_mix, other = _rms_bwd(dx1, sv["mix"], vec(post_mix_g, l), None, out_dtype=BF16,
                                            name="rms_bwd_mix_post", comm=_exchange_comm([dw_up]))
        q_up = _pair_sum(dw_up, other, name="pair_sum_w_up")
        dw_out = _mm_tn(sv["cat"], dmix, 1, name="dw_out")[0].reshape(N_CHIPS, -1, D_MODEL)
        dcat, other = _mm_nt(dmix, g_out, out_dtype=F32, name="d_cat", comm=_exchange_comm([dw_out]))
        q_out = _pair_sum(dw_out, other, name="pair_sum_w_out")
        dq, dk, dv, dbias, part = _natten_bwd(sv["proj"], sv["bias"], dcat, _scatter_comm([q_up]))
        dag, dwdw, dvec, grads["w_up"] = _conv_bwd(sv["proj"], sv["uc"], dcat, wdw_pad[l], vec(conv_ln_g, l),
                                                   vec(conv_ln_b, l), comm=chip_sum("w_up", l, part))
        dproj = jnp.concatenate([dag, dq, dk, dv], axis=1)
        dh, part = _mm_nt(dproj, g_in, out_dtype=F32, name="d_h", comm=_scatter_comm([q_out]))
        dw_in, grads["w_out"] = _mm_tn(sv["h"], dproj, N_CHIPS, name="dw_in", comm=chip_sum("w_out", l, part))
        dx, dg_pre_mix, other = _rms_bwd(dh, sv["x"], vec(pre_mix_g, l), dx1, out_dtype=F32,
                                         name="rms_bwd_mix_pre", comm=_exchange_comm([dw_in]))
        q_in = _pair_sum(dw_in, other, name="pair_sum_w_in")

        drpb = _rpb_grad(dbias)
        small[l] = jnp.concatenate([
            dvec[0], dvec[1], dvec[2], dg_pre_mix[0], dg_post_mix[0], dg_pre_mlp[0], dg_post_mlp[0],
            drpb.reshape(-1), dwdw[:CONV_K].reshape(-1)])

    small_sum = _all_devices(_to_rows(jnp.stack(small)), reduce=True, name="reduce_small_grads")
    small_sum = small_sum.reshape(-1)[:depth * small[0].shape[0]].reshape(depth, -1)
    sizes = [CONV_W, CONV_W, CONV_W, D_MODEL, D_MODEL, D_MODEL, D_MODEL, rpb[0].size, CONV_K * CONV_W]
    offs = np.concatenate([[0], np.cumsum(sizes)])
    pieces = [small_sum[:, offs[i]:offs[i + 1]] for i in range(len(sizes))]
    g_small = {
        "b_dw": pieces[0], "conv_ln_g": pieces[1], "conv_ln_b": pieces[2],
        "pre_mix_g": pieces[3], "post_mix_g": pieces[4], "pre_mlp_g": pieces[5], "post_mlp_g": pieces[6],
        "rpb": pieces[7].reshape(rpb.shape),
        "w_dw": lax.dynamic_slice_in_dim(pieces[8].reshape(depth, CONV_K, CONV_W), chip * w_dw.shape[2],
                                         w_dw.shape[2], axis=2),
    }

    given = dict(w_in=(w_in, m_w_in, v_w_in), w_dw=(w_dw, m_w_dw, v_w_dw), b_dw=(b_dw, m_b_dw, v_b_dw),
                 conv_ln_g=(conv_ln_g, m_conv_ln_g, v_conv_ln_g), conv_ln_b=(conv_ln_b, m_conv_ln_b, v_conv_ln_b),
                 rpb=(rpb, m_rpb, v_rpb), w_out=(w_out, m_w_out, v_w_out), w_up=(w_up, m_w_up, v_w_up),
                 w_down=(w_down, m_w_down, v_w_down), pre_mix_g=(pre_mix_g, m_pre_mix_g, v_pre_mix_g),
                 post_mix_g=(post_mix_g, m_post_mix_g, v_post_mix_g), pre_mlp_g=(pre_mlp_g, m_pre_mlp_g, v_pre_mlp_g),
                 post_mlp_g=(post_mlp_g, m_post_mlp_g, v_post_mlp_g))
    results = {}

    (part,) = _run_comm(_scatter_comm([q_in]), name="grad_scatter_w_in_last")
    (grads["w_in"],) = _run_comm(chip_sum("w_in", 0, part), name="grad_fill_w_in_last")
    for n in big_names:
        w, m, v = given[n]
        flat = lambda a: a.reshape(-1, a.shape[-1])
        outs = _adamw(flat(w), flat(grads[n]), flat(m), flat(v), name=f"adamw_{n}")
        results[n] = [o.reshape(w.shape) for o in outs]

    small_names = list(g_small)
    pack = lambda arrs: _to_rows(jnp.concatenate([a.reshape(-1) for a in arrs]))
    outs = _adamw(pack([given[n][0] for n in small_names]), pack([g_small[n] for n in small_names]),
                  pack([given[n][1] for n in small_names]), pack([given[n][2] for n in small_names]),
                  name="adamw_small")
    pos = 0
    for n in small_names:
        w = given[n][0]
        results[n] = [o.reshape(-1)[pos:pos + w.size].reshape(w.shape) for o in outs]
        pos += w.size

    order = ["w_in", "w_dw", "b_dw", "conv_ln_g", "conv_ln_b", "rpb", "w_out", "w_up", "w_down",
             "pre_mix_g", "post_mix_g", "pre_mlp_g", "post_mlp_g"]
    return (loss, dx.reshape(x.shape), *[results[n][0] for n in order], *[results[n][1] for n in order],
            *[results[n][2] for n in order], *[results[n][3] for n in order])
```

```python
import functools

import numpy as np
import jax
import jax.numpy as jnp
from jax import lax
from jax.experimental import pallas as pl
from jax.experimental.pallas import tpu as pltpu

F32 = jnp.float32
BF16 = jnp.bfloat16

D_MODEL = 2048
CONV_W = 1024
NA_W = 1024
N_HEADS = 16
HEAD_D = 64
GRID_W = 64
WIN_ROWS = 8
WIN_COLS = 16
CONV_K = 31
D_FF = 4 * D_MODEL
IN_COLS = 2 * CONV_W + 3 * NA_W
RMS_EPS = 1e-6
LN_EPS = 1e-5
NEG_INF = -1e30
N_CHIPS = 4
N_DEV = 8
HALO = 16
BAND = WIN_ROWS * GRID_W

ADAM_LR = 0.001
ADAM_B1 = 0.9
ADAM_B2 = 0.999
ADAM_EPS = 1e-08
ADAM_WD = 0.01
ADAM_STEP = 10

VMEM_LIMIT = 48 * 1024 * 1024
MESH = pl.DeviceIdType.MESH
ANY = pl.BlockSpec(memory_space=pl.ANY)


def _params(*sem):
    return pltpu.CompilerParams(dimension_semantics=sem, vmem_limit_bytes=VMEM_LIMIT)


class _Comm:
    def __init__(self, ins, out_shape, sems, start, finish, aliased=False):
        self.ins, self.out_shape, self.sems, self.start, self.finish = ins, out_shape, sems, start, finish
        self.aliased = aliased


def _call(body, *, name, grid, in_specs, out_specs, out_shape, scratch_shapes=(), sem, args, comm=None):
    in_specs, out_specs, out_shape = list(in_specs), list(out_specs), list(out_shape)
    scratch_shapes = list(scratch_shapes)
    if comm is None:
        return pl.pallas_call(body, name=name, grid=grid, in_specs=in_specs, out_specs=out_specs,
                              out_shape=out_shape, scratch_shapes=scratch_shapes,
                              compiler_params=_params(*sem))(*args)
    ni, no, ns = len(in_specs), len(out_specs), len(scratch_shapes)
    nci, nco = len(comm.ins), len(comm.out_shape)

    def full(*refs):
        ins, refs = refs[:ni], refs[ni:]
        cins, refs = refs[:nci], refs[nci:]
        outs, refs = refs[:no], refs[no:]
        couts, refs = refs[:nco], refs[nco:]
        scr, csems = refs[:ns], refs[ns:]
        ids = [pl.program_id(d) for d in range(len(grid))]
        first = functools.reduce(jnp.logical_and, [i == 0 for i in ids])
        last = functools.reduce(jnp.logical_and, [i == g - 1 for i, g in zip(ids, grid)])

        @pl.when(first)
        def _():
            comm.start(cins, couts, csems)

        body(*ins, *outs, *scr)

        @pl.when(last)
        def _():
            comm.finish(cins, couts, csems)

    return pl.pallas_call(
        full, name=name, grid=grid,
        in_specs=in_specs + [ANY] * nci, out_specs=out_specs + [ANY] * nco,
        out_shape=out_shape + list(comm.out_shape),
        scratch_shapes=scratch_shapes + list(comm.sems),
        input_output_aliases={ni + i: no + i for i in range(nci)} if comm.aliased else {},
        compiler_params=_params(*(["arbitrary"] * len(grid))),
    )(*args, *comm.ins)


K_TILES = (2048, 1280, 1024, 512)


def _tile(n, pref):
    for t in pref:
        if n % t == 0:
            return t
    return n


def _accumulate(acc, step, n_steps, finish, part):
    if n_steps == 1:
        finish(part())
        return

    @pl.when(step == 0)
    def _():
        acc[...] = part()

    @pl.when(jnp.logical_and(step > 0, step < n_steps - 1))
    def _():
        acc[...] += part()

    @pl.when(step == n_steps - 1)
    def _():
        finish(acc[...] + part())


def _mm_nn(a, w, *, out_dtype, relu2=False, name, comm=None):
    m, k = a.shape
    s, _, n = w.shape
    tm = _tile(m, (1024, 512, 256))
    tn = _tile(n, (1024, 1280, 512))
    tk = _tile(k, K_TILES)
    nps = n // tn
    nk = k // tk

    def body(a_ref, w_ref, *rest):
        outs, acc = rest[:-1], rest[-1]

        def finish(r):
            outs[0][...] = r.astype(outs[0].dtype)
            if relu2:
                p = jnp.maximum(r, 0.0)
                outs[1][...] = (p * p).astype(outs[1].dtype)

        _accumulate(acc, pl.program_id(2), nk, finish,
                    lambda: jnp.dot(a_ref[...], w_ref[...], preferred_element_type=F32))

    o_spec = pl.BlockSpec((tm, tn), lambda i, j, kk: (i, j))
    o_shape = jax.ShapeDtypeStruct((m, s * n), out_dtype)
    return _call(
        body, name=name,
        grid=(m // tm, s * nps, nk),
        in_specs=[pl.BlockSpec((tm, tk), lambda i, j, kk: (i, kk)),
                  pl.BlockSpec((None, tk, tn), lambda i, j, kk: (j // nps, kk, j % nps))],
        out_specs=[o_spec, o_spec] if relu2 else [o_spec],
        out_shape=[o_shape, o_shape] if relu2 else [o_shape],
        scratch_shapes=[pltpu.VMEM((tm, tn), F32)],
        sem=("parallel", "parallel", "arbitrary"), args=(a, w), comm=comm)


def _mm_nt(dy, w, *, out_dtype, up=None, name, comm=None):
    m = dy.shape[0]
    s, k, n = w.shape
    tm = _tile(m, (1024, 512, 256))
    tko = _tile(k, (1024, 512))
    tn = _tile(n, K_TILES)
    nps = n // tn
    nn = s * nps

    def body(dy_ref, w_ref, *rest):
        if up is None:
            o_ref, acc = rest
        else:
            up_ref, o_ref, acc = rest

        def finish(r):
            if up is not None:
                r = r * (2.0 * jnp.maximum(up_ref[...].astype(F32), 0.0))
            o_ref[...] = r.astype(o_ref.dtype)

        _accumulate(acc, pl.program_id(2), nn, finish,
                    lambda: lax.dot_general(dy_ref[...], w_ref[...], (((1,), (1,)), ((), ())),
                                            preferred_element_type=F32))

    in_specs = [pl.BlockSpec((tm, tn), lambda i, j, kk: (i, kk)),
                pl.BlockSpec((None, tko, tn), lambda i, j, kk: (kk // nps, j, kk % nps))]
    args = [dy, w]
    if up is not None:
        in_specs.append(pl.BlockSpec((tm, tko), lambda i, j, kk: (i, j)))
        args.append(up)
    return _call(
        body, name=name,
        grid=(m // tm, k // tko, nn),
        in_specs=in_specs,
        out_specs=[pl.BlockSpec((tm, tko), lambda i, j, kk: (i, j))],
        out_shape=[jax.ShapeDtypeStruct((m, k), out_dtype)],
        scratch_shapes=[pltpu.VMEM((tm, tko), F32)],
        sem=("parallel", "parallel", "arbitrary"), args=args, comm=comm)


def _mm_tn(a, dy, s, *, name, comm=None):
    m, k = a.shape
    n = dy.shape[1] // s
    tk = _tile(k, (1024, 512))
    tn = _tile(n, (1024, 1280, 512))
    tm = _tile(m, K_TILES)
    nps = n // tn
    nm = m // tm

    def body(a_ref, dy_ref, o_ref, acc):
        def finish(r):
            o_ref[...] = r.astype(o_ref.dtype)

        _accumulate(acc, pl.program_id(2), nm, finish,
                    lambda: lax.dot_general(a_ref[...], dy_ref[...], (((0,), (0,)), ((), ())),
                                            preferred_element_type=F32))

    return _call(
        body, name=name,
        grid=(k // tk, s * nps, nm),
        in_specs=[pl.BlockSpec((tm, tk), lambda i, j, mm: (mm, i)),
                  pl.BlockSpec((tm, tn), lambda i, j, mm: (mm, j))],
        out_specs=[pl.BlockSpec((None, tk, tn), lambda i, j, mm: (j // nps, i, j % nps))],
        out_shape=[jax.ShapeDtypeStruct((s, k, n), BF16)],
        scratch_shapes=[pltpu.VMEM((tk, tn), F32)],
        sem=("parallel", "parallel", "arbitrary"), args=(a, dy), comm=comm)


ROWS = 256


def _row_spec(d):
    return pl.BlockSpec((ROWS, d), lambda i: (i, 0))


def _vec_spec(d):
    return pl.BlockSpec((1, d), lambda i: (0, 0))


def _rstd(v):
    return lax.rsqrt(jnp.mean(v * v, axis=-1, keepdims=True) + RMS_EPS)


def _rms_fwd(x, g, *, name):
    t, d = x.shape

    def body(x_ref, g_ref, h_ref):
        xv = x_ref[...]
        h_ref[...] = ((xv * _rstd(xv)) * g_ref[...]).astype(BF16)

    return pl.pallas_call(
        body, name=name, grid=(t // ROWS,),
        in_specs=[_row_spec(d), _vec_spec(d)],
        out_specs=_row_spec(d),
        out_shape=jax.ShapeDtypeStruct((t, d), BF16),
        compiler_params=_params("parallel"),
    )(x, g)


def _resid_rms(res, y, g_post, g_next, *, name, comm=None):
    t, d = res.shape
    with_next = g_next is not None

    def body(res_ref, y_ref, gp_ref, *rest):
        yv = y_ref[...]
        xn = res_ref[...] + (yv * _rstd(yv)) * gp_ref[...]
        if with_next:
            gn_ref, xo_ref, h_ref = rest
            h_ref[...] = ((xn * _rstd(xn)) * gn_ref[...]).astype(BF16)
        else:
            (xo_ref,) = rest
        xo_ref[...] = xn

    in_specs = [_row_spec(d), _row_spec(d), _vec_spec(d)]
    args = [res, y, g_post]
    out_specs = [_row_spec(d)]
    out_shape = [jax.ShapeDtypeStruct((t, d), F32)]
    if with_next:
        in_specs.append(_vec_spec(d))
        args.append(g_next)
        out_specs.append(_row_spec(d))
        out_shape.append(jax.ShapeDtypeStruct((t, d), BF16))
    return _call(body, name=name, grid=(t // ROWS,), in_specs=in_specs, out_specs=out_specs, out_shape=out_shape,
                 sem=("parallel",), args=args, comm=comm)


def _rms_bwd(dy, xin, g, res, *, out_dtype, name, comm=None):
    t, d = xin.shape
    with_res = res is not None

    def body(dy_ref, x_ref, g_ref, *rest):
        if with_res:
            res_ref, dx_ref, dg_ref = rest
        else:
            dx_ref, dg_ref = rest
        xv = x_ref[...]
        r = _rstd(xv)
        nrm = xv * r
        dyv = dy_ref[...]
        dn = dyv * g_ref[...]
        dx = r * (dn - nrm * jnp.mean(dn * nrm, axis=-1, keepdims=True))
        if with_res:
            dx = dx + res_ref[...]
        dx_ref[...] = dx.astype(dx_ref.dtype)

        @pl.when(pl.program_id(0) == 0)
        def _():
            dg_ref[...] = jnp.zeros_like(dg_ref)

        dg_ref[...] += jnp.sum(dyv * nrm, axis=0, keepdims=True)

    in_specs = [_row_spec(d), _row_spec(d), _vec_spec(d)]
    args = [dy, xin, g]
    if with_res:
        in_specs.append(_row_spec(d))
        args.append(res)
    return _call(
        body, name=name, grid=(t // ROWS,),
        in_specs=in_specs,
        out_specs=[_row_spec(d), _vec_spec(d)],
        out_shape=[jax.ShapeDtypeStruct((t, d), out_dtype), jax.ShapeDtypeStruct((1, d), F32)],
        sem=("arbitrary",), args=args, comm=comm)


def _loss_head(y, target):
    t, d = y.shape

    def body(y_ref, t_ref, dy_ref, loss_ref):
        diff = y_ref[...] - t_ref[...]
        dy_ref[...] = diff * (1.0 / d)

        @pl.when(pl.program_id(0) == 0)
        def _():
            loss_ref[...] = jnp.zeros_like(loss_ref)

        loss_ref[...] += jnp.sum(diff * diff) * (0.5 / d)

    return pl.pallas_call(
        body, name="loss_head", grid=(t // ROWS,),
        in_specs=[_row_spec(d), _row_spec(d)],
        out_specs=(_row_spec(d), pl.BlockSpec((8, 128), lambda i: (0, 0))),
        out_shape=(jax.ShapeDtypeStruct((t, d), F32), jax.ShapeDtypeStruct((8, 128), F32)),
        compiler_params=_params("arbitrary"),
    )(y, target)


def _halo_specs(t, col, width):
    rb = ROWS // HALO
    last = t // HALO - 1
    return [pl.BlockSpec((ROWS, width), lambda i: (i, col)),
            pl.BlockSpec((HALO, width), lambda i: (jnp.maximum(i * rb - 1, 0), col)),
            pl.BlockSpec((HALO, width), lambda i: (jnp.minimum((i + 1) * rb, last), col))]


def _glu(a_ref, g_ref):
    return a_ref[...].astype(F32) * jax.nn.sigmoid(g_ref[...].astype(F32))


def _fill_ext(ext, cur, prev, nxt, i, nblk):
    ext[pl.ds(HALO, ROWS), :] = cur
    ext[pl.ds(0, HALO), :] = jnp.where(i > 0, prev, 0.0)
    ext[pl.ds(HALO + ROWS, HALO), :] = jnp.where(i < nblk - 1, nxt, 0.0)


LANES = 128
SUBLANES = 8
EXT_ROWS = ROWS + 2 * HALO
EXT_SCRATCH = pltpu.VMEM((EXT_ROWS, CONV_W), F32)
SHIFT_SCRATCH = pltpu.VMEM((SUBLANES, EXT_ROWS - SUBLANES, CONV_W), F32)


def _shift_copies(ext, shifted):
    for s in range(SUBLANES):
        shifted[s] = ext[pl.ds(s, EXT_ROWS - SUBLANES), :]


def _window(shifted, offset, rows, lanes, start=0):
    row0 = start + offset - offset % SUBLANES
    if not isinstance(row0, int):
        row0 = pl.multiple_of(row0, SUBLANES)
    return shifted[offset % SUBLANES, pl.ds(row0, rows), lanes]


def _conv_fwd(proj, w_dw, b_dw, ln_g, ln_b, comm=None):
    t = proj.shape[0]
    nblk = t // ROWS
    pad = HALO - CONV_K // 2

    def body(ac, ap, an, gc, gp, gn, w_ref, b_ref, lg_ref, lb_ref, yc_ref, uc_ref, uext, ushift):
        i = pl.program_id(0)
        _fill_ext(uext, _glu(ac, gc), _glu(ap, gp), _glu(an, gn), i, nblk)
        _shift_copies(uext, ushift)

        def chunk(cc, carry):
            lanes = pl.ds(pl.multiple_of(cc * LANES, LANES), LANES)
            acc = jnp.broadcast_to(b_ref[:, lanes], (ROWS, LANES))
            for j in range(CONV_K):
                acc = acc + _window(ushift, j + pad, ROWS, lanes) * w_ref[pl.ds(j, 1), lanes]
            uc_ref[:, lanes] = acc
            return carry

        lax.fori_loop(0, CONV_W // LANES, chunk, 0)
        acc = uc_ref[...]
        mu = jnp.mean(acc, axis=-1, keepdims=True)
        xc = acc - mu
        var = jnp.mean(xc * xc, axis=-1, keepdims=True)
        yln = xc * lax.rsqrt(var + LN_EPS) * lg_ref[...] + lb_ref[...]
        yc_ref[...] = (yln * jax.nn.sigmoid(yln)).astype(BF16)

    vec = pl.BlockSpec((1, CONV_W), lambda i: (0, 0))
    return _call(
        body, name="conv_fwd", grid=(nblk,),
        in_specs=_halo_specs(t, 0, CONV_W) + _halo_specs(t, 1, CONV_W)
        + [pl.BlockSpec((32, CONV_W), lambda i: (0, 0)), vec, vec, vec],
        out_specs=[pl.BlockSpec((ROWS, CONV_W), lambda i: (i, 0)),
                   pl.BlockSpec((ROWS, CONV_W), lambda i: (i, 0))],
        out_shape=[jax.ShapeDtypeStruct((t, CONV_W), BF16), jax.ShapeDtypeStruct((t, CONV_W), F32)],
        scratch_shapes=[EXT_SCRATCH, SHIFT_SCRATCH],
        sem=("parallel",), args=(proj, proj, proj, proj, proj, proj, w_dw, b_dw, ln_g, ln_b), comm=comm)


def _conv_bwd(proj, uc, dcat, w_dw, ln_g, ln_b, comm=None):
    t = proj.shape[0]
    nblk = t // ROWS
    pad = HALO - CONV_K // 2

    half = ROWS // 2

    def body(ac, ap, an, gc, gp, gn, uc_c, uc_p, uc_n, dy_c, dy_p, dy_n, w_ref, lg_ref, lb_ref,
             dag_ref, dw_ref, dvec_ref, uext, dext, ushift, dshift, dw_part):
        i = pl.program_id(0)
        _fill_ext(uext, _glu(ac, gc), _glu(ap, gp), _glu(an, gn), i, nblk)
        _shift_copies(uext, ushift)

        def ln_bwd(u_ref, d_ref):
            u = u_ref[...]
            mu = jnp.mean(u, axis=-1, keepdims=True)
            xc = u - mu
            rstd = lax.rsqrt(jnp.mean(xc * xc, axis=-1, keepdims=True) + LN_EPS)
            xhat = xc * rstd
            yln = xhat * lg_ref[...] + lb_ref[...]
            sg = jax.nn.sigmoid(yln)
            dyln = d_ref[...] * (sg * (1.0 + yln * (1.0 - sg)))
            dxh = dyln * lg_ref[...]
            du = rstd * (dxh - jnp.mean(dxh, axis=-1, keepdims=True)
                         - xhat * jnp.mean(dxh * xhat, axis=-1, keepdims=True))
            return du, dyln, xhat

        duc, dyln, xhat = ln_bwd(uc_c, dy_c)
        _fill_ext(dext, duc, ln_bwd(uc_p, dy_p)[0], ln_bwd(uc_n, dy_n)[0], i, nblk)
        _shift_copies(dext, dshift)

        @pl.when(i == 0)
        def _():
            dw_part[...] = jnp.zeros_like(dw_part)
            dvec_ref[...] = jnp.zeros_like(dvec_ref)

        dvec_ref[pl.ds(0, 1), :] += jnp.sum(duc, axis=0, keepdims=True)
        dvec_ref[pl.ds(1, 1), :] += jnp.sum(dyln * xhat, axis=0, keepdims=True)
        dvec_ref[pl.ds(2, 1), :] += jnp.sum(dyln, axis=0, keepdims=True)

        def input_grad(cc, carry):
            lanes = pl.ds(pl.multiple_of(cc * LANES, LANES), LANES)
            gate_lanes = pl.ds(pl.multiple_of(CONV_W + cc * LANES, LANES), LANES)
            for r0 in (0, half):
                du = jnp.zeros((half, LANES), F32)
                for j in range(CONV_K):
                    du = du + _window(dshift, 2 * HALO - pad - j, half, lanes, r0) * w_ref[pl.ds(j, 1), lanes]
                rows = pl.ds(r0, half)
                a = ac[rows, lanes].astype(F32)
                sg = jax.nn.sigmoid(gc[rows, lanes].astype(F32))
                dag_ref[rows, lanes] = (du * sg).astype(BF16)
                dag_ref[rows, gate_lanes] = (du * a * sg * (1.0 - sg)).astype(BF16)
            return carry

        def tap_grad(cc, carry):
            lanes = pl.ds(pl.multiple_of(cc * LANES, LANES), LANES)
            for r0 in (0, half):
                duc_c = dext[pl.ds(HALO + r0, half), lanes]
                parts = [jnp.sum((duc_c * _window(ushift, j + pad, half, lanes, r0))
                                 .reshape(half // SUBLANES, SUBLANES, LANES), axis=0) for j in range(CONV_K)]
                parts.append(jnp.zeros((SUBLANES, LANES), F32))
                dw_part[:, lanes] += jnp.concatenate(parts, axis=0)
            return carry

        lax.fori_loop(0, CONV_W // LANES, input_grad, 0)
        lax.fori_loop(0, CONV_W // LANES, tap_grad, 0)

        @pl.when(i == nblk - 1)
        def _():
            dw_ref[...] = jnp.sum(dw_part[...].reshape(32, SUBLANES, CONV_W), axis=1)

    vec = pl.BlockSpec((1, CONV_W), lambda i: (0, 0))
    return _call(
        body, name="conv_bwd", grid=(nblk,),
        in_specs=_halo_specs(t, 0, CONV_W) + _halo_specs(t, 1, CONV_W) + _halo_specs(t, 0, CONV_W)
        + _halo_specs(t, 0, CONV_W) + [pl.BlockSpec((32, CONV_W), lambda i: (0, 0)), vec, vec],
        out_specs=[pl.BlockSpec((ROWS, 2 * CONV_W), lambda i: (i, 0)),
                   pl.BlockSpec((32, CONV_W), lambda i: (0, 0)),
                   pl.BlockSpec((8, CONV_W), lambda i: (0, 0))],
        out_shape=[jax.ShapeDtypeStruct((t, 2 * CONV_W), BF16),
                   jax.ShapeDtypeStruct((32, CONV_W), F32),
                   jax.ShapeDtypeStruct((8, CONV_W), F32)],
        scratch_shapes=[EXT_SCRATCH, EXT_SCRATCH, SHIFT_SCRATCH, SHIFT_SCRATCH,
                        pltpu.VMEM((32 * SUBLANES, CONV_W), F32)],
        sem=("arbitrary",),
        args=(proj, proj, proj, proj, proj, proj, uc, uc, uc, dcat, dcat, dcat, w_dw, ln_g, ln_b), comm=comm)


Q_BLK, K_BLK, V_BLK = 16, 24, 32


def _head_masks():
    lane = lax.broadcasted_iota(jnp.int32, (1, 2 * HEAD_D), 1)
    return [lane < HEAD_D, lane >= HEAD_D]


def _band(r, rows):
    rs = jnp.clip(r - WIN_ROWS // 2, 0, rows - WIN_ROWS)
    return rs, r - rs, pl.multiple_of(r * GRID_W, GRID_W), pl.multiple_of(rs * GRID_W, GRID_W)


ROWS_PER_STEP = 4
NT_DIMS = (((1,), (1,)), ((), ()))
TN_DIMS = (((0,), (0,)), ((), ()))
N_DR = 16
GE_SPEC = pl.BlockSpec((2, N_DR, GRID_W, GRID_W), lambda h: (h, 0, 0, 0))
BIAS_SCRATCH = pltpu.VMEM((WIN_ROWS, 2 * GRID_W, BAND), F32)


def _stack_heads(block, masks):
    z = jnp.zeros_like(block)
    return jnp.concatenate([jnp.where(masks[0], block, z), jnp.where(masks[1], block, z)], axis=0)


def _unstack_heads(both, masks):
    return jnp.where(masks[0], both[:GRID_W], both[GRID_W:])


def _softmax(s, bias):
    s = s * (HEAD_D ** -0.5) + bias
    e = jnp.exp(s - jnp.max(s, axis=-1, keepdims=True))
    return e * (1.0 / jnp.sum(e, axis=-1, keepdims=True))


def _fill_bias(ge_ref, bias):
    wq = lax.broadcasted_iota(jnp.int32, (GRID_W, BAND), 0)
    wk = jnp.bitwise_and(lax.broadcasted_iota(jnp.int32, (GRID_W, BAND), 1), GRID_W - 1)
    cs = jnp.clip(wq - WIN_COLS // 2, 0, GRID_W - WIN_COLS)
    inside = jnp.logical_and(wk >= cs, wk < cs + WIN_COLS)
    for hh in range(2):
        for off in range(WIN_ROWS):
            tile = jnp.concatenate([ge_ref[hh, WIN_ROWS - 1 - off + kr] for kr in range(WIN_ROWS)], axis=-1)
            bias[off, pl.ds(hh * GRID_W, GRID_W), :] = jnp.where(inside, tile, NEG_INF)


def _natten_fwd(proj, ge, comm=None):
    t = proj.shape[0]
    rows = t // GRID_W

    def body(q_ref, k_ref, v_ref, ge_ref, o_ref, bias):
        masks = _head_masks()
        _fill_bias(ge_ref, bias)

        def step(i, carry):
            bands = [_band(i * ROWS_PER_STEP + u, rows) for u in range(ROWS_PER_STEP)]
            vbs = [v_ref[pl.ds(k0, BAND), :] for _, _, _, k0 in bands]
            ss = [lax.dot_general(_stack_heads(q_ref[pl.ds(q0, GRID_W), :], masks), k_ref[pl.ds(k0, BAND), :],
                                  NT_DIMS, preferred_element_type=F32) for _, _, q0, k0 in bands]
            ps = [_softmax(s, bias[off]) for s, (_, off, _, _) in zip(ss, bands)]
            os = [jnp.dot(p.astype(BF16), vb, preferred_element_type=F32) for p, vb in zip(ps, vbs)]
            for o, (_, _, q0, _) in zip(os, bands):
                o_ref[pl.ds(q0, GRID_W), :] = _unstack_heads(o, masks).astype(BF16)
            return carry

        lax.fori_loop(0, rows // ROWS_PER_STEP, step, 0)

    col = lambda b: pl.BlockSpec((t, 2 * HEAD_D), lambda h: (0, b + h))
    return _call(
        body, name="natten_fwd", grid=(N_HEADS // 2,),
        in_specs=[col(Q_BLK), col(K_BLK), col(V_BLK), GE_SPEC],
        out_specs=[pl.BlockSpec((t, 2 * HEAD_D), lambda h: (0, h))],
        out_shape=[jax.ShapeDtypeStruct((t, NA_W), BF16)],
        scratch_shapes=[BIAS_SCRATCH],
        sem=("parallel",), args=(proj, proj, proj, ge), comm=comm)


def _natten_bwd(proj, ge, dcat, comm=None):
    t = proj.shape[0]
    rows = t // GRID_W
    scale = HEAD_D ** -0.5

    def body(q_ref, k_ref, v_ref, ge_ref, do_ref, dq_ref, dk_ref, dv_ref, dge_ref, bias, dbias, dk_acc, dv_acc):
        masks = _head_masks()
        _fill_bias(ge_ref, bias)
        dk_acc[...] = jnp.zeros_like(dk_acc)
        dv_acc[...] = jnp.zeros_like(dv_acc)
        dbias[...] = jnp.zeros_like(dbias)

        def step(i, carry):
            bands = [_band(i * ROWS_PER_STEP + u, rows) for u in range(ROWS_PER_STEP)]
            kbs = [k_ref[pl.ds(k0, BAND), :] for _, _, _, k0 in bands]
            q2s = [_stack_heads(q_ref[pl.ds(q0, GRID_W), :], masks) for _, _, q0, _ in bands]
            do2s = [_stack_heads(do_ref[pl.ds(q0, GRID_W), :].astype(BF16), masks) for _, _, q0, _ in bands]
            ss = [lax.dot_general(q2, kb, NT_DIMS, preferred_element_type=F32) for q2, kb in zip(q2s, kbs)]
            dps = [lax.dot_general(do2, v_ref[pl.ds(k0, BAND), :], NT_DIMS, preferred_element_type=F32)
                   for do2, (_, _, _, k0) in zip(do2s, bands)]
            ps = [_softmax(s, bias[off]) for s, (_, off, _, _) in zip(ss, bands)]
            dss =[p * (dp - jnp.sum(p * dp, axis=-1, keepdims=True)) for p, dp in zip(ps, dps)]
            dsbs = [ds.astype(BF16) for ds in dss]
            dqs = [jnp.dot(dsb, kb, preferred_element_type=F32) for dsb, kb in zip(dsbs, kbs)]
            dks = [lax.dot_general(dsb, q2, TN_DIMS, preferred_element_type=F32) for dsb, q2 in zip(dsbs, q2s)]
            dvs = [lax.dot_general(p.astype(BF16), do2, TN_DIMS, preferred_element_type=F32)
                   for p, do2 in zip(ps, do2s)]
            for ds, dq, dk, dv, (_, off, q0, k0) in zip(dss, dqs, dks, dvs, bands):
                dbias[off] += ds
                dq_ref[pl.ds(q0, GRID_W), :] = (_unstack_heads(dq, masks) * scale).astype(BF16)
                dk_acc[pl.ds(k0, BAND), :] += dk * scale
                dv_acc[pl.ds(k0, BAND), :] += dv
            return carry

        lax.fori_loop(0, rows // ROWS_PER_STEP, step, 0)
        dk_ref[...] = dk_acc[...].astype(BF16)
        dv_ref[...] = dv_acc[...].astype(BF16)
        for hh in range(2):
            for dr in range(N_DR):
                pieces = [dbias[off, pl.ds(hh * GRID_W, GRID_W), pl.ds((dr + off - WIN_ROWS + 1) * GRID_W, GRID_W)]
                          for off in range(WIN_ROWS) if 0 <= dr + off - WIN_ROWS + 1 < WIN_ROWS]
                dge_ref[hh, dr] = functools.reduce(jnp.add, pieces) if pieces else jnp.zeros((GRID_W, GRID_W), F32)

    col = lambda b: pl.BlockSpec((t, 2 * HEAD_D), lambda h: (0, b + h))
    o_spec = pl.BlockSpec((t, 2 * HEAD_D), lambda h: (0, h))
    o_shape = jax.ShapeDtypeStruct((t, NA_W), BF16)
    return _call(
        body, name="natten_bwd", grid=(N_HEADS // 2,),
        in_specs=[col(Q_BLK), col(K_BLK), col(V_BLK), GE_SPEC, col(CONV_W // (2 * HEAD_D))],
        out_specs=[o_spec, o_spec, o_spec, GE_SPEC],
        out_shape=[o_shape, o_shape, o_shape, jax.ShapeDtypeStruct((N_HEADS, N_DR, GRID_W, GRID_W), F32)],
        scratch_shapes=[BIAS_SCRATCH, BIAS_SCRATCH, pltpu.VMEM((t, 2 * HEAD_D), F32),
                        pltpu.VMEM((t, 2 * HEAD_D), F32)],
        sem=("parallel",), args=(proj, proj, proj, ge, dcat), comm=comm)


def _column_offsets():
    c = np.arange(GRID_W)
    return (np.clip(c[None, :] - c[:, None], -(WIN_COLS - 1), WIN_COLS - 1) + (WIN_COLS - 1)).reshape(-1)


def _rpb_expand(rpb_l):
    n_dc = 32
    et = np.zeros((n_dc, GRID_W * GRID_W), np.float32)
    et[_column_offsets(), np.arange(GRID_W * GRID_W)] = 1.0
    rp = jnp.pad(rpb_l, ((0, 0), (0, N_DR - rpb_l.shape[1]), (0, n_dc - rpb_l.shape[2]))).reshape(N_HEADS * N_DR, n_dc)

    def body(r_ref, e_ref, o_ref):
        o_ref[...] = jnp.dot(r_ref[...], e_ref[...], preferred_element_type=F32, precision=lax.Precision.HIGHEST)

    vm = pl.BlockSpec(memory_space=pltpu.VMEM)
    ge = pl.pallas_call(
        body, name="rpb_expand", in_specs=[vm, vm], out_specs=vm,
        out_shape=jax.ShapeDtypeStruct((N_HEADS * N_DR, GRID_W * GRID_W), F32),
        compiler_params=pltpu.CompilerParams(vmem_limit_bytes=VMEM_LIMIT),
    )(rp, jnp.asarray(et))
    return ge.reshape(N_HEADS, N_DR, GRID_W, GRID_W)


def _rpb_grad(dge):
    e = np.zeros((GRID_W * GRID_W, 128), np.float32)
    e[np.arange(GRID_W * GRID_W), _column_offsets()] = 1.0

    def body(z_ref, e_ref, o_ref):
        o_ref[...] = jnp.dot(z_ref[...], e_ref[...], preferred_element_type=F32, precision=lax.Precision.HIGHEST)

    vm = pl.BlockSpec(memory_space=pltpu.VMEM)
    out = pl.pallas_call(
        body, name="rpb_grad", in_specs=[vm, vm], out_specs=vm,
        out_shape=jax.ShapeDtypeStruct((N_HEADS * N_DR, 128), F32),
        compiler_params=pltpu.CompilerParams(vmem_limit_bytes=VMEM_LIMIT),
    )(dge.reshape(N_HEADS * N_DR, GRID_W * GRID_W), jnp.asarray(e))
    return out.reshape(N_HEADS, N_DR, 128)[:, :2 * WIN_ROWS - 1, :2 * WIN_COLS - 1]


def _cast_bf16(w2d, *, name):
    r, c = w2d.shape
    tr = _tile(r, (512, 256))

    def body(w_ref, o_ref):
        o_ref[...] = w_ref[...].astype(BF16)

    spec = pl.BlockSpec((tr, c), lambda i: (i, 0))
    return pl.pallas_call(
        body, name=name, grid=(r // tr,),
        in_specs=[spec], out_specs=spec, out_shape=jax.ShapeDtypeStruct((r, c), BF16),
        compiler_params=_params("parallel"),
    )(w2d)


def _core_index():
    return lax.axis_index("c").astype(jnp.int32).reshape(1)


def _pair_sum(dw, other, *, name):
    s, r, c = dw.shape
    h = r // 2
    tr = _tile(h, (256, 128))
    nb = h // tr

    def body(c_ref, a_ref, b_ref, o_ref):
        del c_ref
        o_ref[...] = (a_ref[...].astype(F32) + b_ref[...].astype(F32)).astype(BF16)

    return pl.pallas_call(
        body, name=name,
        grid_spec=pltpu.PrefetchScalarGridSpec(
            num_scalar_prefetch=1, grid=(s, nb),
            in_specs=[pl.BlockSpec((None, tr, c), lambda si, i, cr: (si, cr[0] * nb + i, 0)),
                      pl.BlockSpec((None, tr, c), lambda si, i, cr: (si, i, 0))],
            out_specs=pl.BlockSpec((None, tr, c), lambda si, i, cr: (si, i, 0))),
        out_shape=jax.ShapeDtypeStruct((s, h, c), BF16),
        compiler_params=_params("parallel", "parallel"),
    )(_core_index(), dw, other)


def _chip_sum(parts, grad, layer, *, name):
    s, h, c = parts.shape
    tr = _tile(h, (256, 128))
    nb = h // tr

    def body(c_ref, p_ref, g_in, o_ref):
        del c_ref, g_in
        acc = p_ref[0].astype(F32) + p_ref[1].astype(F32)
        acc = acc + p_ref[2].astype(F32)
        o_ref[...] = acc + p_ref[3].astype(F32)

    return pl.pallas_call(
        body, name=name,
        grid_spec=pltpu.PrefetchScalarGridSpec(
            num_scalar_prefetch=1, grid=(nb,),
            in_specs=[pl.BlockSpec((s, tr, c), lambda i, cr: (0, i, 0)), ANY],
            out_specs=pl.BlockSpec((None, tr, c), lambda i, cr: (layer, cr[0] * nb + i, 0))),
        out_shape=jax.ShapeDtypeStruct(grad.shape, F32),
        input_output_aliases={2: 0},
        compiler_params=_params("parallel"),
    )(_core_index(), parts, grad)


def _adamw(w, g, m, v, *, name):
    r, c = w.shape
    tr = _tile(r, (256, 128, 8))
    tr = tr if r % tr == 0 else r
    bc1 = 1.0 - ADAM_B1 ** ADAM_STEP
    bc2 = 1.0 - ADAM_B2 ** ADAM_STEP

    def body(w_ref, g_ref, m_ref, v_ref, go_ref, d_ref, mo_ref, vo_ref):
        gv = g_ref[...]
        mn = ADAM_B1 * m_ref[...] + (1.0 - ADAM_B1) * gv
        vn = ADAM_B2 * v_ref[...] + (1.0 - ADAM_B2) * (gv * gv)
        go_ref[...] = gv
        mo_ref[...] = mn
        vo_ref[...] = vn
        d_ref[...] = -ADAM_LR * ((mn / bc1) / (jnp.sqrt(vn / bc2) + ADAM_EPS) + ADAM_WD * w_ref[...])

    spec = pl.BlockSpec((tr, c), lambda i: (i, 0))
    shape = jax.ShapeDtypeStruct((r, c), F32)
    return _call(body, name=name, grid=(r // tr,), in_specs=[spec] * 4, out_specs=[spec] * 4, out_shape=[shape] * 4,
                 sem=("parallel",), args=(w, g, m, v))


def _me():
    return lax.axis_index("x"), lax.axis_index("y"), lax.axis_index("c")


def _other_chips(x, y):
    return [(1 - x, y), (x, 1 - y), (1 - x, 1 - y)]


def _remote(src, dst, send, recv, k, to):
    return pltpu.make_async_remote_copy(src_ref=src, dst_ref=dst, send_sem=send.at[k], recv_sem=recv.at[k],
                                        device_id=to, device_id_type=MESH)


def _run_comm(comm, *, name):
    def body(*refs):
        ni, no = len(comm.ins), len(comm.out_shape)
        ins, outs, sems = refs[:ni], refs[ni:ni + no], refs[ni + no:]
        comm.start(ins, outs, sems)
        comm.finish(ins, outs, sems)

    return pl.pallas_call(
        body, name=name,
        in_specs=[ANY] * len(comm.ins), out_specs=[ANY] * len(comm.out_shape),
        out_shape=list(comm.out_shape), scratch_shapes=list(comm.sems),
        input_output_aliases={i: i for i in range(len(comm.ins))} if comm.aliased else {},
    )(*comm.ins)


def _half(ref, chip, hc):
    hr = ref.shape[1] // 2
    return ref.at[chip, pl.ds(hc * hr, hr), :]


def _gather_comm(layer, shards):
    nw = len(shards)
    layers = [layer] * nw if isinstance(layer, int) else list(layer)

    def sent(ins, outs, sems):
        send, recv = sems
        x, y, c = _me()
        me_chip = 2 * x + y
        first = []
        for w in range(nw):
            hr = outs[w].shape[1] // 2
            src = ins[w].at[layers[w], pl.ds(c * hr, hr), :]
            for j, chip in enumerate(_other_chips(x, y)):
                first.append(_remote(src, _half(outs[w], me_chip, c), send, recv, 3 * w + j, (*chip, c)))
        return first

    def start(ins, outs, sems):
        for cp in sent(ins, outs, sems):
            cp.start()

    def finish(ins, outs, sems):
        send, recv = sems
        first = sent(ins, outs, sems)
        x, y, c = _me()
        for w in range(nw):
            for j, chip in enumerate(_other_chips(x, y)):
                landed = _half(outs[w], 2 * chip[0] + chip[1], c)
                _remote(landed, landed, send, recv, 3 * w + j, (*chip, c)).wait_recv()
        for cp in first:
            cp.wait_send()

    return _Comm(list(shards), [jax.ShapeDtypeStruct((N_CHIPS,) + s.shape[1:], BF16) for s in shards],
                 [pltpu.SemaphoreType.DMA((3 * nw,)), pltpu.SemaphoreType.DMA((3 * nw,))], start, finish)


def _place_own(gathered, shards, layer, chip):
    layers = [layer] * len(shards) if isinstance(layer, int) else list(layer)
    return [lax.dynamic_update_slice(g, s[l][None], (chip, 0, 0)) for g, s, l in zip(gathered, shards, layers)]


def _forward_comm(gathered):
    nw = len(gathered)

    def sent(outs, sems):
        send, recv = sems
        x, y, c = _me()
        cps = []
        for w in range(nw):
            for j, chip in enumerate(_other_chips(x, y)):
                landed = _half(outs[w], 2 * chip[0] + chip[1], c)
                cps.append(_remote(landed, landed, send, recv, 3 * w + j, (x, y, 1 - c)))
        return cps

    def start(ins, outs, sems):
        for cp in sent(outs, sems):
            cp.start()

    def finish(ins, outs, sems):
        send, recv = sems
        x, y, c = _me()
        for w in range(nw):
            for j, chip in enumerate(_other_chips(x, y)):
                theirs = _half(outs[w], 2 * chip[0] + chip[1], 1 - c)
                _remote(theirs, theirs, send, recv, 3 * w + j, (x, y, 1 - c)).wait_recv()
        for cp in sent(outs, sems):
            cp.wait_send()

    return _Comm(list(gathered), [jax.ShapeDtypeStruct(g.shape, BF16) for g in gathered],
                 [pltpu.SemaphoreType.DMA((3 * nw,)), pltpu.SemaphoreType.DMA((3 * nw,))], start, finish,
                 aliased=True)


def _exchange_comm(dws):
    nw = len(dws)

    def copies(ins, outs, sems):
        send, recv = sems
        x, y, c = _me()
        cps = []
        for w in range(nw):
            hr = ins[w].shape[1] // 2
            src = ins[w].at[:, pl.ds((1 - c) * hr, hr), :]
            cps.append(_remote(src, outs[w], send, recv, w, (x, y, 1 - c)))
        return cps

    def start(ins, outs, sems):
        for cp in copies(ins, outs, sems):
            cp.start()

    def finish(ins, outs, sems):
        for cp in copies(ins, outs, sems):
            cp.wait()

    return _Comm(list(dws), [jax.ShapeDtypeStruct((d.shape[0], d.shape[1] // 2, d.shape[2]), BF16) for d in dws],
                 [pltpu.SemaphoreType.DMA((nw,)), pltpu.SemaphoreType.DMA((nw,))], start, finish)


def _scatter_comm(qs):
    nw = len(qs)

    def sent(ins, outs, sems):
        send, recv, local = sems
        x, y, c = _me()
        me_chip = 2 * x + y
        mine = [pltpu.make_async_copy(ins[w].at[me_chip], outs[w].at[me_chip], local.at[w]) for w in range(nw)]
        cps = []
        for w in range(nw):
            for j, chip in enumerate(_other_chips(x, y)):
                cps.append(_remote(ins[w].at[2 * chip[0] + chip[1]], outs[w].at[me_chip], send, recv,
                                   3 * w + j, (*chip, c)))
        return mine, cps

    def start(ins, outs, sems):
        mine, cps = sent(ins, outs, sems)
        for cp in mine + cps:
            cp.start()

    def finish(ins, outs, sems):
        send, recv, _ = sems
        mine, cps = sent(ins, outs, sems)
        x, y, c = _me()
        for w in range(nw):
            for j, chip in enumerate(_other_chips(x, y)):
                theirs = outs[w].at[2 * chip[0] + chip[1]]
                _remote(theirs, theirs, send, recv, 3 * w + j, (*chip, c)).wait_recv()
        for cp in cps:
            cp.wait_send()
        for cp in mine:
            cp.wait()

    return _Comm(list(qs), [jax.ShapeDtypeStruct(q.shape, BF16) for q in qs],
                 [pltpu.SemaphoreType.DMA((3 * nw,)), pltpu.SemaphoreType.DMA((3 * nw,)),
                  pltpu.SemaphoreType.DMA((nw,))], start, finish)


def _fill_comm(layer, grads):
    nw = len(grads)

    def sent(outs, sems):
        send, recv = sems
        x, y, c = _me()
        cps = []
        for w in range(nw):
            hr = outs[w].shape[1] // 2
            mine = outs[w].at[layer, pl.ds(c * hr, hr), :]
            cps.append(_remote(mine, mine, send, recv, w, (x, y, 1 - c)))
        return cps

    def start(ins, outs, sems):
        for cp in sent(outs, sems):
            cp.start()

    def finish(ins, outs, sems):
        send, recv = sems
        x, y, c = _me()
        for w in range(nw):
            hr = outs[w].shape[1] // 2
            theirs = outs[w].at[layer, pl.ds((1 - c) * hr, hr), :]
            _remote(theirs, theirs, send, recv, w, (x, y, 1 - c)).wait_recv()
        for cp in sent(outs, sems):
            cp.wait_send()

    return _Comm(list(grads), [jax.ShapeDtypeStruct(g.shape, F32) for g in grads],
                 [pltpu.SemaphoreType.DMA((nw,)), pltpu.SemaphoreType.DMA((nw,))], start, finish, aliased=True)


def _all_devices(pack, *, reduce, name):
    r = pack.shape[0]

    def body(p_ref, o_ref, buf, send, recv):
        x, y, c = _me()
        me = 4 * x + 2 * y + c
        buf[me] = p_ref[...]
        flips = [(fx, fy, fc) for fx in (0, 1) for fy in (0, 1) for fc in (0, 1)][1:]
        peers = [(x ^ fx, y ^ fy, c ^ fc) for fx, fy, fc in flips]
        cps = [_remote(p_ref, buf.at[me], send, recv, k, peer) for k, peer in enumerate(peers)]
        for cp in cps:
            cp.start()
        for k, (px, py, pc) in enumerate(peers):
            theirs = buf.at[4 * px + 2 * py + pc]
            _remote(theirs, theirs, send, recv, k, (px, py, pc)).wait_recv()
        for cp in cps:
            cp.wait_send()
        if reduce:
            acc = buf[0]
            for d in range(1, N_DEV):
                acc = acc + buf[d]
            o_ref[...] = acc
        else:
            o_ref[...] = buf[...]

    vm = pl.BlockSpec(memory_space=pltpu.VMEM)
    return pl.pallas_call(
        body, name=name,
        in_specs=[vm], out_specs=vm,
        out_shape=jax.ShapeDtypeStruct((r, 128) if reduce else (N_DEV, r, 128), F32),
        scratch_shapes=[pltpu.VMEM((N_DEV, r, 128), F32), pltpu.SemaphoreType.DMA((N_DEV - 1,)),
                        pltpu.SemaphoreType.DMA((N_DEV - 1,))],
        compiler_params=pltpu.CompilerParams(vmem_limit_bytes=VMEM_LIMIT),
    )(pack)


def _to_rows(flat):
    v = flat.reshape(-1)
    rows = -(-v.shape[0] // 1024) * 8
    return jnp.pad(v, (0, rows * 128 - v.shape[0])).reshape(rows, 128)


def kernel(x, w_in, w_dw, b_dw, conv_ln_g, conv_ln_b, rpb, w_out, w_up, w_down, pre_mix_g, post_mix_g, pre_mlp_g, post_mlp_g, loss_target, m_w_in, m_w_dw, m_b_dw, m_conv_ln_g, m_conv_ln_b, m_rpb, m_w_out, m_w_up, m_w_down, m_pre_mix_g, m_post_mix_g, m_pre_mlp_g, m_post_mlp_g, v_w_in, v_w_dw, v_b_dw, v_conv_ln_g, v_conv_ln_b, v_rpb, v_w_out, v_w_up, v_w_down, v_pre_mix_g, v_post_mix_g, v_pre_mlp_g, v_post_mlp_g):
    depth = w_in.shape[0]
    t = x.shape[1]
    xl = x.reshape(t, D_MODEL)
    target = loss_target.reshape(t, D_MODEL)
    chip = 2 * lax.axis_index("x") + lax.axis_index("y")

    big = {"w_in": w_in, "w_out": w_out, "w_up": w_up, "w_down": w_down}
    big_names = list(big)
    shards = [_cast_bf16(big[n].reshape(-1, big[n].shape[-1]), name=f"cast_{n}").reshape(big[n].shape)
              for n in big_names]

    wdw_all = _all_devices(_to_rows(w_dw), reduce=False, name="gather_w_dw")[::2]
    wdw_all = wdw_all.reshape(N_CHIPS, -1)[:, :w_dw.size].reshape((N_CHIPS,) + w_dw.shape)
    wdw_full = jnp.moveaxis(wdw_all, 0, 2).reshape(depth, CONV_K, CONV_W)
    wdw_pad = jnp.pad(wdw_full, ((0, 0), (0, 32 - CONV_K), (0, 0)))

    vec = lambda a, l: a[l].reshape(1, -1)

    sh_in, sh_out, sh_up, sh_down = shards
    saved = []
    h = _rms_fwd(xl, vec(pre_mix_g, 0), name="rms_first")
    (g_in,) = _place_own(_run_comm(_forward_comm(_run_comm(_gather_comm(0, [sh_in]), name="gather_w_in_first")),
                                   name="forward_w_in_first"), [sh_in], 0, chip)
    g_up = None
    for l in range(depth):
        more = l + 1 < depth
        bias = _rpb_expand(rpb[l])
        early = [sh_out, sh_up] if l == 0 else [sh_out]
        proj, *parts = _mm_nn(h, g_in, out_dtype=BF16, name="proj", comm=_gather_comm(l, early))
        yc, uc, *got = _conv_fwd(proj, wdw_pad[l], vec(b_dw, l), vec(conv_ln_g, l), vec(conv_ln_b, l),
                                 comm=_forward_comm(parts))
        got = _place_own(got, early, l, chip)
        g_out = got[0].reshape(1, D_MODEL, D_MODEL)
        if l == 0:
            g_up = got[1]
        ya, *parts = _natten_fwd(proj, bias, _gather_comm(l + 1, [sh_in]) if more else None)
        cat = jnp.concatenate([yc, ya], axis=1)
        mix, *got = _mm_nn(cat, g_out, out_dtype=F32, name="out_proj", comm=_forward_comm(parts) if more else None)
        g_in_next = _place_own(got, [sh_in], l + 1, chip)
        x1, h2 = _resid_rms(xl, mix, vec(post_mix_g, l), vec(pre_mlp_g, l), name="mix_resid")
        up, act, part = _mm_nn(h2, g_up, out_dtype=BF16, relu2=True, name="mlp_up", comm=_gather_comm(l, [sh_down]))
        (g_down,) = _place_own(_run_comm(_forward_comm([part]), name="forward_w_down"), [sh_down], l, chip)
        g_down = g_down.reshape(1, D_FF, D_MODEL)
        f, *parts = _mm_nn(act, g_down, out_dtype=F32, name="mlp_down",
                           comm=_gather_comm(l + 1, [sh_up]) if more else None)
        saved.append(dict(x=xl, h=h, proj=proj, uc=uc, cat=cat, mix=mix, x1=x1, h2=h2, up=up, act=act, f=f,
                          bias=bias, w=(g_in, g_out, g_up, g_down)))
        if more:
            xl, h, g_up = _resid_rms(x1, f, vec(post_mlp_g, l), vec(pre_mix_g, l + 1), name="mlp_resid",
                                     comm=_forward_comm(parts))
            (g_up,) = _place_own([g_up], [sh_up], l + 1, chip)
            (g_in,) = g_in_next
        else:
            (xl,) = _resid_rms(x1, f, vec(post_mlp_g, l), None, name="mlp_resid_last")

    dx, loss_blk = _loss_head(xl, target)
    loss = lax.psum(loss_blk[0, 0], ("x", "y", "c"))

    grads = {n: lax.empty(big[n].shape, F32) for n in big_names}
    small = [None] * depth

    def chip_sum(n, layer, part):
        grads[n] = _chip_sum(part, grads[n], layer, name=f"chip_sum_{n}_{layer}")
        return _fill_comm(layer, [grads[n]])

    q_in = None
    for l in reversed(range(depth)):
        sv = saved[l]
        g_in, g_out, g_up, g_down = sv["w"]
        df, dg_post_mlp = _rms_bwd(dx, sv["f"], vec(post_mlp_g, l), None, out_dtype=BF16, name="rms_bwd_mlp_post")
        dup, *part = _mm_nt(df, g_down, out_dtype=BF16, up=sv["up"], name="d_act",
                            comm=_scatter_comm([q_in]) if q_in is not None else None)
        dw_down, *done = _mm_tn(sv["act"], df, 1, name="dw_down",
                                comm=chip_sum("w_in", l + 1, part[0]) if part else None)
        if done:
            grads["w_in"] = done[0]
        dw_down = dw_down.reshape(N_CHIPS, -1, D_MODEL)
        dh2, other = _mm_nt(dup, g_up, out_dtype=F32, name="d_h2", comm=_exchange_comm([dw_down]))
        q_down = _pair_sum(dw_down, other, name="pair_sum_w_down")
        dw_up, part = _mm_tn(sv["h2"], dup, N_CHIPS, name="dw_up", comm=_scatter_comm([q_down]))
        dx1, dg_pre_mlp, grads["w_down"] = _rms_bwd(dh2, sv["x1"], vec(pre_mlp_g, l), dx, out_dtype=F32,
                                                    name="rms_bwd_mlp_pre", comm=chip_sum("w_down", l, part))
        dmix, dg_post_mix, other = _rms_bwd(dx1, sv["mix"], vec(post_mix_g, l), None, out_dtype=BF16,
                                            name="rms_bwd_mix_post", comm=_exchange_comm([dw_up]))
        q_up = _pair_sum(dw_up, other, name="pair_sum_w_up")
        dw_out = _mm_tn(sv["cat"], dmix, 1, name="dw_out")[0].reshape(N_CHIPS, -1, D_MODEL)
        dcat, other = _mm_nt(dmix, g_out, out_dtype=F32, name="d_cat", comm=_exchange_comm([dw_out]))
        q_out = _pair_sum(dw_out, other, name="pair_sum_w_out")
        dq, dk, dv, dbias, part = _natten_bwd(sv["proj"], sv["bias"], dcat, _scatter_comm([q_up]))
        dag, dwdw, dvec, grads["w_up"] = _conv_bwd(sv["proj"], sv["uc"], dcat, wdw_pad[l], vec(conv_ln_g, l),
                                                   vec(conv_ln_b, l), comm=chip_sum("w_up", l, part))
        dproj = jnp.concatenate([dag, dq, dk, dv], axis=1)
        dh, part = _mm_nt(dproj, g_in, out_dtype=F32, name="d_h", comm=_scatter_comm([q_out]))
        dw_in, grads["w_out"] = _mm_tn(sv["h"], dproj, N_CHIPS, name="dw_in", comm=chip_sum("w_out", l, part))
        dx, dg_pre_mix, other = _rms_bwd(dh, sv["x"], vec(pre_mix_g, l), dx1, out_dtype=F32,
                                         name="rms_bwd_mix_pre", comm=_exchange_comm([dw_in]))
        q_in = _pair_sum(dw_in, other, name="pair_sum_w_in")

        drpb = _rpb_grad(dbias)
        small[l] = jnp.concatenate([
            dvec[0], dvec[1], dvec[2], dg_pre_mix[0], dg_post_mix[0], dg_pre_mlp[0], dg_post_mlp[0],
            drpb.reshape(-1), dwdw[:CONV_K].reshape(-1)])

    small_sum = _all_devices(_to_rows(jnp.stack(small)), reduce=True, name="reduce_small_grads")
    small_sum = small_sum.reshape(-1)[:depth * small[0].shape[0]].reshape(depth, -1)
    sizes = [CONV_W, CONV_W, CONV_W, D_MODEL, D_MODEL, D_MODEL, D_MODEL, rpb[0].size, CONV_K * CONV_W]
    offs = np.concatenate([[0], np.cumsum(sizes)])
    pieces = [small_sum[:, offs[i]:offs[i + 1]] for i in range(len(sizes))]
    g_small = {
        "b_dw": pieces[0], "conv_ln_g": pieces[1], "conv_ln_b": pieces[2],
        "pre_mix_g": pieces[3], "post_mix_g": pieces[4], "pre_mlp_g": pieces[5], "post_mlp_g": pieces[6],
        "rpb": pieces[7].reshape(rpb.shape),
        "w_dw": lax.dynamic_slice_in_dim(pieces[8].reshape(depth, CONV_K, CONV_W), chip * w_dw.shape[2],
                                         w_dw.shape[2], axis=2),
    }

    given = dict(w_in=(w_in, m_w_in, v_w_in), w_dw=(w_dw, m_w_dw, v_w_dw), b_dw=(b_dw, m_b_dw, v_b_dw),
                 conv_ln_g=(conv_ln_g, m_conv_ln_g, v_conv_ln_g), conv_ln_b=(conv_ln_b, m_conv_ln_b, v_conv_ln_b),
                 rpb=(rpb, m_rpb, v_rpb), w_out=(w_out, m_w_out, v_w_out), w_up=(w_up, m_w_up, v_w_up),
                 w_down=(w_down, m_w_down, v_w_down), pre_mix_g=(pre_mix_g, m_pre_mix_g, v_pre_mix_g),
                 post_mix_g=(post_mix_g, m_post_mix_g, v_post_mix_g), pre_mlp_g=(pre_mlp_g, m_pre_mlp_g, v_pre_mlp_g),
                 post_mlp_g=(post_mlp_g, m_post_mlp_g, v_post_mlp_g))
    results = {}

    (part,) = _run_comm(_scatter_comm([q_in]), name="grad_scatter_w_in_last")
    (grads["w_in"],) = _run_comm(chip_sum("w_in", 0, part), name="grad_fill_w_in_last")
    for n in big_names:
        w, m, v = given[n]
        flat = lambda a: a.reshape(-1, a.shape[-1])
        outs = _adamw(flat(w), flat(grads[n]), flat(m), flat(v), name=f"adamw_{n}")
        results[n] = [o.reshape(w.shape) for o in outs]

    small_names = list(g_small)
    pack = lambda arrs: _to_rows(jnp.concatenate([a.reshape(-1) for a in arrs]))
    outs = _adamw(pack([given[n][0] for n in small_names]), pack([g_small[n] for n in small_names]),
                  pack([given[n][1] for n in small_names]), pack([given[n][2] for n in small_names]),
                  name="adamw_small")
    pos = 0
    for n in small_names:
        w = given[n][0]
        results[n] = [o.reshape(-1)[pos:pos + w.size].reshape(w.shape) for o in outs]
        pos += w.size

    order = ["w_in", "w_dw", "b_dw", "conv_ln_g", "conv_ln_b", "rpb", "w_out", "w_up", "w_down",
             "pre_mix_g", "post_mix_g", "pre_mlp_g", "post_mlp_g"]
    return (loss, dx.reshape(x.shape), *[results[n][0] for n in order], *[results[n][1] for n in order],
            *[results[n][2] for n in order], *[results[n][3] for n in order])
```

```python
import functools

import numpy as np
import jax
import jax.numpy as jnp
from jax import lax
from jax.experimental import pallas as pl
from jax.experimental.pallas import tpu as pltpu

F32 = jnp.float32
BF16 = jnp.bfloat16

D_MODEL = 2048
CONV_W = 1024
NA_W = 1024
N_HEADS = 16
HEAD_D = 64
GRID_W = 64
WIN_ROWS = 8
WIN_COLS = 16
CONV_K = 31
D_FF = 4 * D_MODEL
IN_COLS = 2 * CONV_W + 3 * NA_W
RMS_EPS = 1e-6
LN_EPS = 1e-5
NEG_INF = -1e30
N_CHIPS = 4
N_DEV = 8
HALO = 16
BAND = WIN_ROWS * GRID_W

ADAM_LR = 0.001
ADAM_B1 = 0.9
ADAM_B2 = 0.999
ADAM_EPS = 1e-08
ADAM_WD = 0.01
ADAM_STEP = 10

VMEM_LIMIT = 48 * 1024 * 1024
MESH = pl.DeviceIdType.MESH
ANY = pl.BlockSpec(memory_space=pl.ANY)


def _params(*sem):
    return pltpu.CompilerParams(dimension_semantics=sem, vmem_limit_bytes=VMEM_LIMIT)


class _Comm:
    def __init__(self, ins, out_shape, sems, start, finish, aliased=False):
        self.ins, self.out_shape, self.sems, self.start, self.finish = ins, out_shape, sems, start, finish
        self.aliased = aliased


def _call(body, *, name, grid, in_specs, out_specs, out_shape, scratch_shapes=(), sem, args, comm=None):
    in_specs, out_specs, out_shape = list(in_specs), list(out_specs), list(out_shape)
    scratch_shapes = list(scratch_shapes)
    if comm is None:
        return pl.pallas_call(body, name=name, grid=grid, in_specs=in_specs, out_specs=out_specs,
                              out_shape=out_shape, scratch_shapes=scratch_shapes,
                              compiler_params=_params(*sem))(*args)
    ni, no, ns = len(in_specs), len(out_specs), len(scratch_shapes)
    nci, nco = len(comm.ins), len(comm.out_shape)

    def full(*refs):
        ins, refs = refs[:ni], refs[ni:]
        cins, refs = refs[:nci], refs[nci:]
        outs, refs = refs[:no], refs[no:]
        couts, refs = refs[:nco], refs[nco:]
        scr, csems = refs[:ns], refs[ns:]
        ids = [pl.program_id(d) for d in range(len(grid))]
        first = functools.reduce(jnp.logical_and, [i == 0 for i in ids])
        last = functools.reduce(jnp.logical_and, [i == g - 1 for i, g in zip(ids, grid)])

        @pl.when(first)
        def _():
            comm.start(cins, couts, csems)

        body(*ins, *outs, *scr)

        @pl.when(last)
        def _():
            comm.finish(cins, couts, csems)

    return pl.pallas_call(
        full, name=name, grid=grid,
        in_specs=in_specs + [ANY] * nci, out_specs=out_specs + [ANY] * nco,
        out_shape=out_shape + list(comm.out_shape),
        scratch_shapes=scratch_shapes + list(comm.sems),
        input_output_aliases={ni + i: no + i for i in range(nci)} if comm.aliased else {},
        compiler_params=_params(*(["arbitrary"] * len(grid))),
    )(*args, *comm.ins)


K_TILES = (2048, 1280, 1024, 512)


def _tile(n, pref):
    for t in pref:
        if n % t == 0:
            return t
    return n


def _accumulate(acc, step, n_steps, finish, part):
    if n_steps == 1:
        finish(part())
        return

    @pl.when(step == 0)
    def _():
        acc[...] = part()

    @pl.when(jnp.logical_and(step > 0, step < n_steps - 1))
    def _():
        acc[...] += part()

    @pl.when(step == n_steps - 1)
    def _():
        finish(acc[...] + part())


def _mm_nn(a, w, *, out_dtype, relu2=False, name, comm=None):
    m, k = a.shape
    s, _, n = w.shape
    tm = _tile(m, (1024, 512, 256))
    tn = _tile(n, (1024, 1280, 512))
    tk = _tile(k, K_TILES)
    nps = n // tn
    nk = k // tk

    def body(a_ref, w_ref, *rest):
        outs, acc = rest[:-1], rest[-1]

        def finish(r):
            outs[0][...] = r.astype(outs[0].dtype)
            if relu2:
                p = jnp.maximum(r, 0.0)
                outs[1][...] = (p * p).astype(outs[1].dtype)

        _accumulate(acc, pl.program_id(2), nk, finish,
                    lambda: jnp.dot(a_ref[...], w_ref[...], preferred_element_type=F32))

    o_spec = pl.BlockSpec((tm, tn), lambda i, j, kk: (i, j))
    o_shape = jax.ShapeDtypeStruct((m, s * n), out_dtype)
    return _call(
        body, name=name,
        grid=(m // tm, s * nps, nk),
        in_specs=[pl.BlockSpec((tm, tk), lambda i, j, kk: (i, kk)),
                  pl.BlockSpec((None, tk, tn), lambda i, j, kk: (j // nps, kk, j % nps))],
        out_specs=[o_spec, o_spec] if relu2 else [o_spec],
        out_shape=[o_shape, o_shape] if relu2 else [o_shape],
        scratch_shapes=[pltpu.VMEM((tm, tn), F32)],
        sem=("parallel", "parallel", "arbitrary"), args=(a, w), comm=comm)


def _mm_nt(dy, w, *, out_dtype, up=None, name, comm=None):
    m = dy.shape[0]
    s, k, n = w.shape
    tm = _tile(m, (1024, 512, 256))
    tko = _tile(k, (1024, 512))
    tn = _tile(n, K_TILES)
    nps = n // tn
    nn = s * nps

    def body(dy_ref, w_ref, *rest):
        if up is None:
            o_ref, acc = rest
        else:
            up_ref, o_ref, acc = rest

        def finish(r):
            if up is not None:
                r = r * (2.0 * jnp.maximum(up_ref[...].astype(F32), 0.0))
            o_ref[...] = r.astype(o_ref.dtype)

        _accumulate(acc, pl.program_id(2), nn, finish,
                    lambda: lax.dot_general(dy_ref[...], w_ref[...], (((1,), (1,)), ((), ())),
                                            preferred_element_type=F32))

    in_specs = [pl.BlockSpec((tm, tn), lambda i, j, kk: (i, kk)),
                pl.BlockSpec((None, tko, tn), lambda i, j, kk: (kk // nps, j, kk % nps))]
    args = [dy, w]
    if up is not None:
        in_specs.append(pl.BlockSpec((tm, tko), lambda i, j, kk: (i, j)))
        args.append(up)
    return _call(
        body, name=name,
        grid=(m // tm, k // tko, nn),
        in_specs=in_specs,
        out_specs=[pl.BlockSpec((tm, tko), lambda i, j, kk: (i, j))],
        out_shape=[jax.ShapeDtypeStruct((m, k), out_dtype)],
        scratch_shapes=[pltpu.VMEM((tm, tko), F32)],
        sem=("parallel", "parallel", "arbitrary"), args=args, comm=comm)


def _mm_tn(a, dy, s, *, name, comm=None):
    m, k = a.shape
    n = dy.shape[1] // s
    tk = _tile(k, (1024, 512))
    tn = _tile(n, (1024, 1280, 512))
    tm = _tile(m, K_TILES)
    nps = n // tn
    nm = m // tm

    def body(a_ref, dy_ref, o_ref, acc):
        def finish(r):
            o_ref[...] = r.astype(o_ref.dtype)

        _accumulate(acc, pl.program_id(2), nm, finish,
                    lambda: lax.dot_general(a_ref[...], dy_ref[...], (((0,), (0,)), ((), ())),
                                            preferred_element_type=F32))

    return _call(
        body, name=name,
        grid=(k // tk, s * nps, nm),
        in_specs=[pl.BlockSpec((tm, tk), lambda i, j, mm: (mm, i)),
                  pl.BlockSpec((tm, tn), lambda i, j, mm: (mm, j))],
        out_specs=[pl.BlockSpec((None, tk, tn), lambda i, j, mm: (j // nps, i, j % nps))],
        out_shape=[jax.ShapeDtypeStruct((s, k, n), BF16)],
        scratch_shapes=[pltpu.VMEM((tk, tn), F32)],
        sem=("parallel", "parallel", "arbitrary"), args=(a, dy), comm=comm)


ROWS = 256


def _row_spec(d):
    return pl.BlockSpec((ROWS, d), lambda i: (i, 0))


def _vec_spec(d):
    return pl.BlockSpec((1, d), lambda i: (0, 0))


def _rstd(v):
    return lax.rsqrt(jnp.mean(v * v, axis=-1, keepdims=True) + RMS_EPS)


def _rms_fwd(x, g, *, name):
    t, d = x.shape

    def body(x_ref, g_ref, h_ref):
        xv = x_ref[...]
        h_ref[...] = ((xv * _rstd(xv)) * g_ref[...]).astype(BF16)

    return pl.pallas_call(
        body, name=name, grid=(t // ROWS,),
        in_specs=[_row_spec(d), _vec_spec(d)],
        out_specs=_row_spec(d),
        out_shape=jax.ShapeDtypeStruct((t, d), BF16),
        compiler_params=_params("parallel"),
    )(x, g)


def _resid_rms(res, y, g_post, g_next, *, name, comm=None):
    t, d = res.shape
    with_next = g_next is not None

    def body(res_ref, y_ref, gp_ref, *rest):
        yv = y_ref[...]
        xn = res_ref[...] + (yv * _rstd(yv)) * gp_ref[...]
        if with_next:
            gn_ref, xo_ref, h_ref = rest
            h_ref[...] = ((xn * _rstd(xn)) * gn_ref[...]).astype(BF16)
        else:
            (xo_ref,) = rest
        xo_ref[...] = xn

    in_specs = [_row_spec(d), _row_spec(d), _vec_spec(d)]
    args = [res, y, g_post]
    out_specs = [_row_spec(d)]
    out_shape = [jax.ShapeDtypeStruct((t, d), F32)]
    if with_next:
        in_specs.append(_vec_spec(d))
        args.append(g_next)
        out_specs.append(_row_spec(d))
        out_shape.append(jax.ShapeDtypeStruct((t, d), BF16))
    return _call(body, name=name, grid=(t // ROWS,), in_specs=in_specs, out_specs=out_specs, out_shape=out_shape,
                 sem=("parallel",), args=args, comm=comm)


def _rms_bwd(dy, xin, g, res, *, out_dtype, name, comm=None):
    t, d = xin.shape
    with_res = res is not None

    def body(dy_ref, x_ref, g_ref, *rest):
        if with_res:
            res_ref, dx_ref, dg_ref = rest
        else:
            dx_ref, dg_ref = rest
        xv = x_ref[...]
        r = _rstd(xv)
        nrm = xv * r
        dyv = dy_ref[...]
        dn = dyv * g_ref[...]
        dx = r * (dn - nrm * jnp.mean(dn * nrm, axis=-1, keepdims=True))
        if with_res:
            dx = dx + res_ref[...]
        dx_ref[...] = dx.astype(dx_ref.dtype)

        @pl.when(pl.program_id(0) == 0)
        def _():
            dg_ref[...] = jnp.zeros_like(dg_ref)

        dg_ref[...] += jnp.sum(dyv * nrm, axis=0, keepdims=True)

    in_specs = [_row_spec(d), _row_spec(d), _vec_spec(d)]
    args = [dy, xin, g]
    if with_res:
        in_specs.append(_row_spec(d))
        args.append(res)
    return _call(
        body, name=name, grid=(t // ROWS,),
        in_specs=in_specs,
        out_specs=[_row_spec(d), _vec_spec(d)],
        out_shape=[jax.ShapeDtypeStruct((t, d), out_dtype), jax.ShapeDtypeStruct((1, d), F32)],
        sem=("arbitrary",), args=args, comm=comm)


def _loss_head(y, target):
    t, d = y.shape

    def body(y_ref, t_ref, dy_ref, loss_ref):
        diff = y_ref[...] - t_ref[...]
        dy_ref[...] = diff * (1.0 / d)

        @pl.when(pl.program_id(0) == 0)
        def _():
            loss_ref[...] = jnp.zeros_like(loss_ref)

        loss_ref[...] += jnp.sum(diff * diff) * (0.5 / d)

    return pl.pallas_call(
        body, name="loss_head", grid=(t // ROWS,),
        in_specs=[_row_spec(d), _row_spec(d)],
        out_specs=(_row_spec(d), pl.BlockSpec((8, 128), lambda i: (0, 0))),
        out_shape=(jax.ShapeDtypeStruct((t, d), F32), jax.ShapeDtypeStruct((8, 128), F32)),
        compiler_params=_params("arbitrary"),
    )(y, target)


def _halo_specs(t, col, width):
    rb = ROWS // HALO
    last = t // HALO - 1
    return [pl.BlockSpec((ROWS, width), lambda i: (i, col)),
            pl.BlockSpec((HALO, width), lambda i: (jnp.maximum(i * rb - 1, 0), col)),
            pl.BlockSpec((HALO, width), lambda i: (jnp.minimum((i + 1) * rb, last), col))]


def _glu(a_ref, g_ref):
    return a_ref[...].astype(F32) * jax.nn.sigmoid(g_ref[...].astype(F32))


def _fill_ext(ext, cur, prev, nxt, i, nblk):
    ext[pl.ds(HALO, ROWS), :] = cur
    ext[pl.ds(0, HALO), :] = jnp.where(i > 0, prev, 0.0)
    ext[pl.ds(HALO + ROWS, HALO), :] = jnp.where(i < nblk - 1, nxt, 0.0)


LANES = 128
SUBLANES = 8
EXT_ROWS = ROWS + 2 * HALO
EXT_SCRATCH = pltpu.VMEM((EXT_ROWS, CONV_W), F32)
SHIFT_SCRATCH = pltpu.VMEM((SUBLANES, EXT_ROWS - SUBLANES, CONV_W), F32)


def _shift_copies(ext, shifted):
    for s in range(SUBLANES):
        shifted[s] = ext[pl.ds(s, EXT_ROWS - SUBLANES), :]


def _window(shifted, offset, rows, lanes, start=0):
    row0 = start + offset - offset % SUBLANES
    if not isinstance(row0, int):
        row0 = pl.multiple_of(row0, SUBLANES)
    return shifted[offset % SUBLANES, pl.ds(row0, rows), lanes]


def _conv_fwd(proj, w_dw, b_dw, ln_g, ln_b, comm=None):
    t = proj.shape[0]
    nblk = t // ROWS
    pad = HALO - CONV_K // 2

    def body(ac, ap, an, gc, gp, gn, w_ref, b_ref, lg_ref, lb_ref, yc_ref, uc_ref, uext, ushift):
        i = pl.program_id(0)
        _fill_ext(uext, _glu(ac, gc), _glu(ap, gp), _glu(an, gn), i, nblk)
        _shift_copies(uext, ushift)

        def chunk(cc, carry):
            lanes = pl.ds(pl.multiple_of(cc * LANES, LANES), LANES)
            acc = jnp.broadcast_to(b_ref[:, lanes], (ROWS, LANES))
            for j in range(CONV_K):
                acc = acc + _window(ushift, j + pad, ROWS, lanes) * w_ref[pl.ds(j, 1), lanes]
            uc_ref[:, lanes] = acc
            return carry

        lax.fori_loop(0, CONV_W // LANES, chunk, 0)
        acc = uc_ref[...]
        mu = jnp.mean(acc, axis=-1, keepdims=True)
        xc = acc - mu
        var = jnp.mean(xc * xc, axis=-1, keepdims=True)
        yln = xc * lax.rsqrt(var + LN_EPS) * lg_ref[...] + lb_ref[...]
        yc_ref[...] = (yln * jax.nn.sigmoid(yln)).astype(BF16)

    vec = pl.BlockSpec((1, CONV_W), lambda i: (0, 0))
    return _call(
        body, name="conv_fwd", grid=(nblk,),
        in_specs=_halo_specs(t, 0, CONV_W) + _halo_specs(t, 1, CONV_W)
        + [pl.BlockSpec((32, CONV_W), lambda i: (0, 0)), vec, vec, vec],
        out_specs=[pl.BlockSpec((ROWS, CONV_W), lambda i: (i, 0)),
                   pl.BlockSpec((ROWS, CONV_W), lambda i: (i, 0))],
        out_shape=[jax.ShapeDtypeStruct((t, CONV_W), BF16), jax.ShapeDtypeStruct((t, CONV_W), F32)],
        scratch_shapes=[EXT_SCRATCH, SHIFT_SCRATCH],
        sem=("parallel",), args=(proj, proj, proj, proj, proj, proj, w_dw, b_dw, ln_g, ln_b), comm=comm)


def _conv_bwd(proj, uc, dcat, w_dw, ln_g, ln_b, comm=None):
    t = proj.shape[0]
    nblk = t // ROWS
    pad = HALO - CONV_K // 2

    half = ROWS // 2

    def body(ac, ap, an, gc, gp, gn, uc_c, uc_p, uc_n, dy_c, dy_p, dy_n, w_ref, lg_ref, lb_ref,
             dag_ref, dw_ref, dvec_ref, uext, dext, ushift, dshift, dw_part):
        i = pl.program_id(0)
        _fill_ext(uext, _glu(ac, gc), _glu(ap, gp), _glu(an, gn), i, nblk)
        _shift_copies(uext, ushift)

        def ln_bwd(u_ref, d_ref):
            u = u_ref[...]
            mu = jnp.mean(u, axis=-1, keepdims=True)
            xc = u - mu
            rstd = lax.rsqrt(jnp.mean(xc * xc, axis=-1, keepdims=True) + LN_EPS)
            xhat = xc * rstd
            yln = xhat * lg_ref[...] + lb_ref[...]
            sg = jax.nn.sigmoid(yln)
            dyln = d_ref[...] * (sg * (1.0 + yln * (1.0 - sg)))
            dxh = dyln * lg_ref[...]
            du = rstd * (dxh - jnp.mean(dxh, axis=-1, keepdims=True)
                         - xhat * jnp.mean(dxh * xhat, axis=-1, keepdims=True))
            return du, dyln, xhat

        duc, dyln, xhat = ln_bwd(uc_c, dy_c)
        _fill_ext(dext, duc, ln_bwd(uc_p, dy_p)[0], ln_bwd(uc_n, dy_n)[0], i, nblk)
        _shift_copies(dext, dshift)

        @pl.when(i == 0)
        def _():
            dw_part[...] = jnp.zeros_like(dw_part)
            dvec_ref[...] = jnp.zeros_like(dvec_ref)

        dvec_ref[pl.ds(0, 1), :] += jnp.sum(duc, axis=0, keepdims=True)
        dvec_ref[pl.ds(1, 1), :] += jnp.sum(dyln * xhat, axis=0, keepdims=True)
        dvec_ref[pl.ds(2, 1), :] += jnp.sum(dyln, axis=0, keepdims=True)

        def input_grad(cc, carry):
            lanes = pl.ds(pl.multiple_of(cc * LANES, LANES), LANES)
            gate_lanes = pl.ds(pl.multiple_of(CONV_W + cc * LANES, LANES), LANES)
            for r0 in (0, half):
                du = jnp.zeros((half, LANES), F32)
                for j in range(CONV_K):
                    du = du + _window(dshift, 2 * HALO - pad - j, half, lanes, r0) * w_ref[pl.ds(j, 1), lanes]
                rows = pl.ds(r0, half)
                a = ac[rows, lanes].astype(F32)
                sg = jax.nn.sigmoid(gc[rows, lanes].astype(F32))
                dag_ref[rows, lanes] = (du * sg).astype(BF16)
                dag_ref[rows, gate_lanes] = (du * a * sg * (1.0 - sg)).astype(BF16)
            return carry

        def tap_grad(cc, carry):
            lanes = pl.ds(pl.multiple_of(cc * LANES, LANES), LANES)
            for r0 in (0, half):
                duc_c = dext[pl.ds(HALO + r0, half), lanes]
                parts = [jnp.sum((duc_c * _window(ushift, j + pad, half, lanes, r0))
                                 .reshape(half // SUBLANES, SUBLANES, LANES), axis=0) for j in range(CONV_K)]
                parts.append(jnp.zeros((SUBLANES, LANES), F32))
                dw_part[:, lanes] += jnp.concatenate(parts, axis=0)
            return carry

        lax.fori_loop(0, CONV_W // LANES, input_grad, 0)
        lax.fori_loop(0, CONV_W // LANES, tap_grad, 0)

        @pl.when(i == nblk - 1)
        def _():
            dw_ref[...] = jnp.sum(dw_part[...].reshape(32, SUBLANES, CONV_W), axis=1)

    vec = pl.BlockSpec((1, CONV_W), lambda i: (0, 0))
    return _call(
        body, name="conv_bwd", grid=(nblk,),
        in_specs=_halo_specs(t, 0, CONV_W) + _halo_specs(t, 1, CONV_W) + _halo_specs(t, 0, CONV_W)
        + _halo_specs(t, 0, CONV_W) + [pl.BlockSpec((32, CONV_W), lambda i: (0, 0)), vec, vec],
        out_specs=[pl.BlockSpec((ROWS, 2 * CONV_W), lambda i: (i, 0)),
                   pl.BlockSpec((32, CONV_W), lambda i: (0, 0)),
                   pl.BlockSpec((8, CONV_W), lambda i: (0, 0))],
        out_shape=[jax.ShapeDtypeStruct((t, 2 * CONV_W), BF16),
                   jax.ShapeDtypeStruct((32, CONV_W), F32),
                   jax.ShapeDtypeStruct((8, CONV_W), F32)],
        scratch_shapes=[EXT_SCRATCH, EXT_SCRATCH, SHIFT_SCRATCH, SHIFT_SCRATCH,
                        pltpu.VMEM((32 * SUBLANES, CONV_W), F32)],
        sem=("arbitrary",),
        args=(proj, proj, proj, proj, proj, proj, uc, uc, uc, dcat, dcat, dcat, w_dw, ln_g, ln_b), comm=comm)


Q_BLK, K_BLK, V_BLK = 16, 24, 32


def _head_masks():
    lane = lax.broadcasted_iota(jnp.int32, (1, 2 * HEAD_D), 1)
    return [lane < HEAD_D, lane >= HEAD_D]


def _band(r, rows):
    rs = jnp.clip(r - WIN_ROWS // 2, 0, rows - WIN_ROWS)
    return rs, r - rs, pl.multiple_of(r * GRID_W, GRID_W), pl.multiple_of(rs * GRID_W, GRID_W)


ROWS_PER_STEP = 4
NT_DIMS = (((1,), (1,)), ((), ()))
TN_DIMS = (((0,), (0,)), ((), ()))
N_DR = 16
GE_SPEC = pl.BlockSpec((2, N_DR, GRID_W, GRID_W), lambda h: (h, 0, 0, 0))
BIAS_SCRATCH = pltpu.VMEM((WIN_ROWS, 2 * GRID_W, BAND), F32)


def _stack_heads(block, masks):
    z = jnp.zeros_like(block)
    return jnp.concatenate([jnp.where(masks[0], block, z), jnp.where(masks[1], block, z)], axis=0)


def _unstack_heads(both, masks):
    return jnp.where(masks[0], both[:GRID_W], both[GRID_W:])


def _softmax(s, bias):
    s = s * (HEAD_D ** -0.5) + bias
    e = jnp.exp(s - jnp.max(s, axis=-1, keepdims=True))
    return e * (1.0 / jnp.sum(e, axis=-1, keepdims=True))


def _fill_bias(ge_ref, bias):
    wq = lax.broadcasted_iota(jnp.int32, (GRID_W, BAND), 0)
    wk = jnp.bitwise_and(lax.broadcasted_iota(jnp.int32, (GRID_W, BAND), 1), GRID_W - 1)
    cs = jnp.clip(wq - WIN_COLS // 2, 0, GRID_W - WIN_COLS)
    inside = jnp.logical_and(wk >= cs, wk < cs + WIN_COLS)
    for hh in range(2):
        for off in range(WIN_ROWS):
            tile = jnp.concatenate([ge_ref[hh, WIN_ROWS - 1 - off + kr] for kr in range(WIN_ROWS)], axis=-1)
            bias[off, pl.ds(hh * GRID_W, GRID_W), :] = jnp.where(inside, tile, NEG_INF)


def _natten_fwd(proj, ge, comm=None):
    t = proj.shape[0]
    rows = t // GRID_W

    def body(q_ref, k_ref, v_ref, ge_ref, o_ref, bias):
        masks = _head_masks()
        _fill_bias(ge_ref, bias)

        def step(i, carry):
            bands = [_band(i * ROWS_PER_STEP + u, rows) for u in range(ROWS_PER_STEP)]
            vbs = [v_ref[pl.ds(k0, BAND), :] for _, _, _, k0 in bands]
            ss = [lax.dot_general(_stack_heads(q_ref[pl.ds(q0, GRID_W), :], masks), k_ref[pl.ds(k0, BAND), :],
                                  NT_DIMS, preferred_element_type=F32) for _, _, q0, k0 in bands]
            ps = [_softmax(s, bias[off]) for s, (_, off, _, _) in zip(ss, bands)]
            os = [jnp.dot(p.astype(BF16), vb, preferred_element_type=F32) for p, vb in zip(ps, vbs)]
            for o, (_, _, q0, _) in zip(os, bands):
                o_ref[pl.ds(q0, GRID_W), :] = _unstack_heads(o, masks).astype(BF16)
            return carry

        lax.fori_loop(0, rows // ROWS_PER_STEP, step, 0)

    col = lambda b: pl.BlockSpec((t, 2 * HEAD_D), lambda h: (0, b + h))
    return _call(
        body, name="natten_fwd", grid=(N_HEADS // 2,),
        in_specs=[col(Q_BLK), col(K_BLK), col(V_BLK), GE_SPEC],
        out_specs=[pl.BlockSpec((t, 2 * HEAD_D), lambda h: (0, h))],
        out_shape=[jax.ShapeDtypeStruct((t, NA_W), BF16)],
        scratch_shapes=[BIAS_SCRATCH],
        sem=("parallel",), args=(proj, proj, proj, ge), comm=comm)


def _natten_bwd(proj, ge, dcat, comm=None):
    t = proj.shape[0]
    rows = t // GRID_W
    scale = HEAD_D ** -0.5

    def body(q_ref, k_ref, v_ref, ge_ref, do_ref, dq_ref, dk_ref, dv_ref, dge_ref, bias, dbias, dk_acc, dv_acc):
        masks = _head_masks()
        _fill_bias(ge_ref, bias)
        dk_acc[...] = jnp.zeros_like(dk_acc)
        dv_acc[...] = jnp.zeros_like(dv_acc)
        dbias[...] = jnp.zeros_like(dbias)

        def step(i, carry):
            bands = [_band(i * ROWS_PER_STEP + u, rows) for u in range(ROWS_PER_STEP)]
            kbs = [k_ref[pl.ds(k0, BAND), :] for _, _, _, k0 in bands]
            q2s = [_stack_heads(q_ref[pl.ds(q0, GRID_W), :], masks) for _, _, q0, _ in bands]
            do2s = [_stack_heads(do_ref[pl.ds(q0, GRID_W), :].astype(BF16), masks) for _, _, q0, _ in bands]
            ss = [lax.dot_general(q2, kb, NT_DIMS, preferred_element_type=F32) for q2, kb in zip(q2s, kbs)]
            dps = [lax.dot_general(do2, v_ref[pl.ds(k0, BAND), :], NT_DIMS, preferred_element_type=F32)
                   for do2, (_, _, _, k0) in zip(do2s, bands)]
            ps = [_softmax(s, bias[off]) for s, (_, off, _, _) in zip(ss, bands)]
            dss =[p * (dp - jnp.sum(p * dp, axis=-1, keepdims=True)) for p, dp in zip(ps, dps)]
            dsbs = [ds.astype(BF16) for ds in dss]
            dqs = [jnp.dot(dsb, kb, preferred_element_type=F32) for dsb, kb in zip(dsbs, kbs)]
            dks = [lax.dot_general(dsb, q2, TN_DIMS, preferred_element_type=F32) for dsb, q2 in zip(dsbs, q2s)]
            dvs = [lax.dot_general(p.astype(BF16), do2, TN_DIMS, preferred_element_type=F32)
                   for p, do2 in zip(ps, do2s)]
            for ds, dq, dk, dv, (_, off, q0, k0) in zip(dss, dqs, dks, dvs, bands):
                dbias[off] += ds
                dq_ref[pl.ds(q0, GRID_W), :] = (_unstack_heads(dq, masks) * scale).astype(BF16)
                dk_acc[pl.ds(k0, BAND), :] += dk * scale
                dv_acc[pl.ds(k0, BAND), :] += dv
            return carry

        lax.fori_loop(0, rows // ROWS_PER_STEP, step, 0)
        dk_ref[...] = dk_acc[...].astype(BF16)
        dv_ref[...] = dv_acc[...].astype(BF16)
        for hh in range(2):
            for dr in range(N_DR):
                pieces = [dbias[off, pl.ds(hh * GRID_W, GRID_W), pl.ds((dr + off - WIN_ROWS + 1) * GRID_W, GRID_W)]
                          for off in range(WIN_ROWS) if 0 <= dr + off - WIN_ROWS + 1 < WIN_ROWS]
                dge_ref[hh, dr] = functools.reduce(jnp.add, pieces) if pieces else jnp.zeros((GRID_W, GRID_W), F32)

    col = lambda b: pl.BlockSpec((t, 2 * HEAD_D), lambda h: (0, b + h))
    o_spec = pl.BlockSpec((t, 2 * HEAD_D), lambda h: (0, h))
    o_shape = jax.ShapeDtypeStruct((t, NA_W), BF16)
    return _call(
        body, name="natten_bwd", grid=(N_HEADS // 2,),
        in_specs=[col(Q_BLK), col(K_BLK), col(V_BLK), GE_SPEC, col(CONV_W // (2 * HEAD_D))],
        out_specs=[o_spec, o_spec, o_spec, GE_SPEC],
        out_shape=[o_shape, o_shape, o_shape, jax.ShapeDtypeStruct((N_HEADS, N_DR, GRID_W, GRID_W), F32)],
        scratch_shapes=[BIAS_SCRATCH, BIAS_SCRATCH, pltpu.VMEM((t, 2 * HEAD_D), F32),
                        pltpu.VMEM((t, 2 * HEAD_D), F32)],
        sem=("parallel",), args=(proj, proj, proj, ge, dcat), comm=comm)


def _column_offsets():
    c = np.arange(GRID_W)
    return (np.clip(c[None, :] - c[:, None], -(WIN_COLS - 1), WIN_COLS - 1) + (WIN_COLS - 1)).reshape(-1)


def _rpb_expand(rpb_l):
    n_dc = 32
    et = np.zeros((n_dc, GRID_W * GRID_W), np.float32)
    et[_column_offsets(), np.arange(GRID_W * GRID_W)] = 1.0
    rp = jnp.pad(rpb_l, ((0, 0), (0, N_DR - rpb_l.shape[1]), (0, n_dc - rpb_l.shape[2]))).reshape(N_HEADS * N_DR, n_dc)

    def body(r_ref, e_ref, o_ref):
        o_ref[...] = jnp.dot(r_ref[...], e_ref[...], preferred_element_type=F32, precision=lax.Precision.HIGHEST)

    vm = pl.BlockSpec(memory_space=pltpu.VMEM)
    ge = pl.pallas_call(
        body, name="rpb_expand", in_specs=[vm, vm], out_specs=vm,
        out_shape=jax.ShapeDtypeStruct((N_HEADS * N_DR, GRID_W * GRID_W), F32),
        compiler_params=pltpu.CompilerParams(vmem_limit_bytes=VMEM_LIMIT),
    )(rp, jnp.asarray(et))
    return ge.reshape(N_HEADS, N_DR, GRID_W, GRID_W)


def _rpb_grad(dge):
    e = np.zeros((GRID_W * GRID_W, 128), np.float32)
    e[np.arange(GRID_W * GRID_W), _column_offsets()] = 1.0

    def body(z_ref, e_ref, o_ref):
        o_ref[...] = jnp.dot(z_ref[...], e_ref[...], preferred_element_type=F32, precision=lax.Precision.HIGHEST)

    vm = pl.BlockSpec(memory_space=pltpu.VMEM)
    out = pl.pallas_call(
        body, name="rpb_grad", in_specs=[vm, vm], out_specs=vm,
        out_shape=jax.ShapeDtypeStruct((N_HEADS * N_DR, 128), F32),
        compiler_params=pltpu.CompilerParams(vmem_limit_bytes=VMEM_LIMIT),
    )(dge.reshape(N_HEADS * N_DR, GRID_W * GRID_W), jnp.asarray(e))
    return out.reshape(N_HEADS, N_DR, 128)[:, :2 * WIN_ROWS - 1, :2 * WIN_COLS - 1]


def _cast_bf16(w2d, *, name):
    r, c = w2d.shape
    tr = _tile(r, (512, 256))

    def body(w_ref, o_ref):
        o_ref[...] = w_ref[...].astype(BF16)

    spec = pl.BlockSpec((tr, c), lambda i: (i, 0))
    return pl.pallas_call(
        body, name=name, grid=(r // tr,),
        in_specs=[spec], out_specs=spec, out_shape=jax.ShapeDtypeStruct((r, c), BF16),
        compiler_params=_params("parallel"),
    )(w2d)


def _core_index():
    return lax.axis_index("c").astype(jnp.int32).reshape(1)


def _pair_sum(dw, other, *, name):
    s, r, c = dw.shape
    h = r // 2
    tr = _tile(h, (256, 128))
    nb = h // tr

    def body(c_ref, a_ref, b_ref, o_ref):
        del c_ref
        o_ref[...] = (a_ref[...].astype(F32) + b_ref[...].astype(F32)).astype(BF16)

    return pl.pallas_call(
        body, name=name,
        grid_spec=pltpu.PrefetchScalarGridSpec(
            num_scalar_prefetch=1, grid=(s, nb),
            in_specs=[pl.BlockSpec((None, tr, c), lambda si, i, cr: (si, cr[0] * nb + i, 0)),
                      pl.BlockSpec((None, tr, c), lambda si, i, cr: (si, i, 0))],
            out_specs=pl.BlockSpec((None, tr, c), lambda si, i, cr: (si, i, 0))),
        out_shape=jax.ShapeDtypeStruct((s, h, c), BF16),
        compiler_params=_params("parallel", "parallel"),
    )(_core_index(), dw, other)


def _chip_sum(parts, grad, layer, *, name):
    s, h, c = parts.shape
    tr = _tile(h, (256, 128))
    nb = h // tr

    def body(c_ref, p_ref, g_in, o_ref):
        del c_ref, g_in
        acc = p_ref[0].astype(F32) + p_ref[1].astype(F32)
        acc = acc + p_ref[2].astype(F32)
        o_ref[...] = acc + p_ref[3].astype(F32)

    return pl.pallas_call(
        body, name=name,
        grid_spec=pltpu.PrefetchScalarGridSpec(
            num_scalar_prefetch=1, grid=(nb,),
            in_specs=[pl.BlockSpec((s, tr, c), lambda i, cr: (0, i, 0)), ANY],
            out_specs=pl.BlockSpec((None, tr, c), lambda i, cr: (layer, cr[0] * nb + i, 0))),
        out_shape=jax.ShapeDtypeStruct(grad.shape, F32),
        input_output_aliases={2: 0},
        compiler_params=_params("parallel"),
    )(_core_index(), parts, grad)


def _adamw(w, g, m, v, *, name):
    r, c = w.shape
    tr = _tile(r, (256, 128, 8))
    tr = tr if r % tr == 0 else r
    bc1 = 1.0 - ADAM_B1 ** ADAM_STEP
    bc2 = 1.0 - ADAM_B2 ** ADAM_STEP

    def body(w_ref, g_ref, m_ref, v_ref, go_ref, d_ref, mo_ref, vo_ref):
        gv = g_ref[...]
        mn = ADAM_B1 * m_ref[...] + (1.0 - ADAM_B1) * gv
        vn = ADAM_B2 * v_ref[...] + (1.0 - ADAM_B2) * (gv * gv)
        go_ref[...] = gv
        mo_ref[...] = mn
        vo_ref[...] = vn
        d_ref[...] = -ADAM_LR * ((mn / bc1) / (jnp.sqrt(vn / bc2) + ADAM_EPS) + ADAM_WD * w_ref[...])

    spec = pl.BlockSpec((tr, c), lambda i: (i, 0))
    shape = jax.ShapeDtypeStruct((r, c), F32)
    return _call(body, name=name, grid=(r // tr,), in_specs=[spec] * 4, out_specs=[spec] * 4, out_shape=[shape] * 4,
                 sem=("parallel",), args=(w, g, m, v))


def _me():
    return lax.axis_index("x"), lax.axis_index("y"), lax.axis_index("c")


def _other_chips(x, y):
    return [(1 - x, y), (x, 1 - y), (1 - x, 1 - y)]


def _remote(src, dst, send, recv, k, to):
    return pltpu.make_async_remote_copy(src_ref=src, dst_ref=dst, send_sem=send.at[k], recv_sem=recv.at[k],
                                        device_id=to, device_id_type=MESH)


def _run_comm(comm, *, name):
    def body(*refs):
        ni, no = len(comm.ins), len(comm.out_shape)
        ins, outs, sems = refs[:ni], refs[ni:ni + no], refs[ni + no:]
        comm.start(ins, outs, sems)
        comm.finish(ins, outs, sems)

    return pl.pallas_call(
        body, name=name,
        in_specs=[ANY] * len(comm.ins), out_specs=[ANY] * len(comm.out_shape),
        out_shape=list(comm.out_shape), scratch_shapes=list(comm.sems),
        input_output_aliases={i: i for i in range(len(comm.ins))} if comm.aliased else {},
    )(*comm.ins)


def _half(ref, chip, hc):
    hr = ref.shape[1] // 2
    return ref.at[chip, pl.ds(hc * hr, hr), :]


def _gather_comm(layer, shards):
    nw = len(shards)
    layers = [layer] * nw if isinstance(layer, int) else list(layer)

    def sent(ins, outs, sems):
        send, recv = sems
        x, y, c = _me()
        me_chip = 2 * x + y
        first = []
        for w in range(nw):
            hr = outs[w].shape[1] // 2
            src = ins[w].at[layers[w], pl.ds(c * hr, hr), :]
            for j, chip in enumerate(_other_chips(x, y)):
                first.append(_remote(src, _half(outs[w], me_chip, c), send, recv, 3 * w + j, (*chip, c)))
        return first

    def start(ins, outs, sems):
        for cp in sent(ins, outs, sems):
            cp.start()

    def finish(ins, outs, sems):
        send, recv = sems
        first = sent(ins, outs, sems)
        x, y, c = _me()
        for w in range(nw):
            for j, chip in enumerate(_other_chips(x, y)):
                landed = _half(outs[w], 2 * chip[0] + chip[1], c)
                _remote(landed, landed, send, recv, 3 * w + j, (*chip, c)).wait_recv()
        for cp in first:
            cp.wait_send()

    return _Comm(list(shards), [jax.ShapeDtypeStruct((N_CHIPS,) + s.shape[1:], BF16) for s in shards],
                 [pltpu.SemaphoreType.DMA((3 * nw,)), pltpu.SemaphoreType.DMA((3 * nw,))], start, finish)


def _place_own(gathered, shards, layer, chip):
    layers = [layer] * len(shards) if isinstance(layer, int) else list(layer)
    return [lax.dynamic_update_slice(g, s[l][None], (chip, 0, 0)) for g, s, l in zip(gathered, shards, layers)]


def _forward_comm(gathered):
    nw = len(gathered)

    def sent(outs, sems):
        send, recv = sems
        x, y, c = _me()
        cps = []
        for w in range(nw):
            for j, chip in enumerate(_other_chips(x, y)):
                landed = _half(outs[w], 2 * chip[0] + chip[1], c)
                cps.append(_remote(landed, landed, send, recv, 3 * w + j, (x, y, 1 - c)))
        return cps

    def start(ins, outs, sems):
        for cp in sent(outs, sems):
            cp.start()

    def finish(ins, outs, sems):
        send, recv = sems
        x, y, c = _me()
        for w in range(nw):
            for j, chip in enumerate(_other_chips(x, y)):
                theirs = _half(outs[w], 2 * chip[0] + chip[1], 1 - c)
                _remote(theirs, theirs, send, recv, 3 * w + j, (x, y, 1 - c)).wait_recv()
        for cp in sent(outs, sems):
            cp.wait_send()

    return _Comm(list(gathered), [jax.ShapeDtypeStruct(g.shape, BF16) for g in gathered],
                 [pltpu.SemaphoreType.DMA((3 * nw,)), pltpu.SemaphoreType.DMA((3 * nw,))], start, finish,
                 aliased=True)


def _exchange_comm(dws):
    nw = len(dws)

    def copies(ins, outs, sems):
        send, recv = sems
        x, y, c = _me()
        cps = []
        for w in range(nw):
            hr = ins[w].shape[1] // 2
            src = ins[w].at[:, pl.ds((1 - c) * hr, hr), :]
            cps.append(_remote(src, outs[w], send, recv, w, (x, y, 1 - c)))
        return cps

    def start(ins, outs, sems):
        for cp in copies(ins, outs, sems):
            cp.start()

    def finish(ins, outs, sems):
        for cp in copies(ins, outs, sems):
            cp.wait()

    return _Comm(list(dws), [jax.ShapeDtypeStruct((d.shape[0], d.shape[1] // 2, d.shape[2]), BF16) for d in dws],
                 [pltpu.SemaphoreType.DMA((nw,)), pltpu.SemaphoreType.DMA((nw,))], start, finish)


def _scatter_comm(qs):
    nw = len(qs)

    def sent(ins, outs, sems):
        send, recv, local = sems
        x, y, c = _me()
        me_chip = 2 * x + y
        mine = [pltpu.make_async_copy(ins[w].at[me_chip], outs[w].at[me_chip], local.at[w]) for w in range(nw)]
        cps = []
        for w in range(nw):
            for j, chip in enumerate(_other_chips(x, y)):
                cps.append(_remote(ins[w].at[2 * chip[0] + chip[1]], outs[w].at[me_chip], send, recv,
                                   3 * w + j, (*chip, c)))
        return mine, cps

    def start(ins, outs, sems):
        mine, cps = sent(ins, outs, sems)
        for cp in mine + cps:
            cp.start()

    def finish(ins, outs, sems):
        send, recv, _ = sems
        mine, cps = sent(ins, outs, sems)
        x, y, c = _me()
        for w in range(nw):
            for j, chip in enumerate(_other_chips(x, y)):
                theirs = outs[w].at[2 * chip[0] + chip[1]]
                _remote(theirs, theirs, send, recv, 3 * w + j, (*chip, c)).wait_recv()
        for cp in cps:
            cp.wait_send()
        for cp in mine:
            cp.wait()

    return _Comm(list(qs), [jax.ShapeDtypeStruct(q.shape, BF16) for q in qs],
                 [pltpu.SemaphoreType.DMA((3 * nw,)), pltpu.SemaphoreType.DMA((3 * nw,)),
                  pltpu.SemaphoreType.DMA((nw,))], start, finish)


def _fill_comm(layer, grads):
    nw = len(grads)

    def sent(outs, sems):
        send, recv = sems
        x, y, c = _me()
        cps = []
        for w in range(nw):
            hr = outs[w].shape[1] // 2
            mine = outs[w].at[layer, pl.ds(c * hr, hr), :]
            cps.append(_remote(mine, mine, send, recv, w, (x, y, 1 - c)))
        return cps

    def start(ins, outs, sems):
        for cp in sent(outs, sems):
            cp.start()

    def finish(ins, outs, sems):
        send, recv = sems
        x, y, c = _me()
        for w in range(nw):
            hr = outs[w].shape[1] // 2
            theirs = outs[w].at[layer, pl.ds((1 - c) * hr, hr), :]
            _remote(theirs, theirs, send, recv, w, (x, y, 1 - c)).wait_recv()
        for cp in sent(outs, sems):
            cp.wait_send()

    return _Comm(list(grads), [jax.ShapeDtypeStruct(g.shape, F32) for g in grads],
                 [pltpu.SemaphoreType.DMA((nw,)), pltpu.SemaphoreType.DMA((nw,))], start, finish, aliased=True)


def _all_devices(pack, *, reduce, name):
    r = pack.shape[0]

    def body(p_ref, o_ref, buf, send, recv):
        x, y, c = _me()
        me = 4 * x + 2 * y + c
        buf[me] = p_ref[...]
        flips = [(fx, fy, fc) for fx in (0, 1) for fy in (0, 1) for fc in (0, 1)][1:]
        peers = [(x ^ fx, y ^ fy, c ^ fc) for fx, fy, fc in flips]
        cps = [_remote(p_ref, buf.at[me], send, recv, k, peer) for k, peer in enumerate(peers)]
        for cp in cps:
            cp.start()
        for k, (px, py, pc) in enumerate(peers):
            theirs = buf.at[4 * px + 2 * py + pc]
            _remote(theirs, theirs, send, recv, k, (px, py, pc)).wait_recv()
        for cp in cps:
            cp.wait_send()
        if reduce:
            acc = buf[0]
            for d in range(1, N_DEV):
                acc = acc + buf[d]
            o_ref[...] = acc
        else:
            o_ref[...] = buf[...]

    vm = pl.BlockSpec(memory_space=pltpu.VMEM)
    return pl.pallas_call(
        body, name=name,
        in_specs=[vm], out_specs=vm,
        out_shape=jax.ShapeDtypeStruct((r, 128) if reduce else (N_DEV, r, 128), F32),
        scratch_shapes=[pltpu.VMEM((N_DEV, r, 128), F32), pltpu.SemaphoreType.DMA((N_DEV - 1,)),
                        pltpu.SemaphoreType.DMA((N_DEV - 1,))],
        compiler_params=pltpu.CompilerParams(vmem_limit_bytes=VMEM_LIMIT),
    )(pack)


def _to_rows(flat):
    v = flat.reshape(-1)
    rows = -(-v.shape[0] // 1024) * 8
    return jnp.pad(v, (0, rows * 128 - v.shape[0])).reshape(rows, 128)


def kernel(x, w_in, w_dw, b_dw, conv_ln_g, conv_ln_b, rpb, w_out, w_up, w_down, pre_mix_g, post_mix_g, pre_mlp_g, post_mlp_g, loss_target, m_w_in, m_w_dw, m_b_dw, m_conv_ln_g, m_conv_ln_b, m_rpb, m_w_out, m_w_up, m_w_down, m_pre_mix_g, m_post_mix_g, m_pre_mlp_g, m_post_mlp_g, v_w_in, v_w_dw, v_b_dw, v_conv_ln_g, v_conv_ln_b, v_rpb, v_w_out, v_w_up, v_w_down, v_pre_mix_g, v_post_mix_g, v_pre_mlp_g, v_post_mlp_g):
    depth = w_in.shape[0]
    t = x.shape[1]
    xl = x.reshape(t, D_MODEL)
    target = loss_target.reshape(t, D_MODEL)
    chip = 2 * lax.axis_index("x") + lax.axis_index("y")

    big = {"w_in": w_in, "w_out": w_out, "w_up": w_up, "w_down": w_down}
    big_names = list(big)
    shards = [_cast_bf16(big[n].reshape(-1, big[n].shape[-1]), name=f"cast_{n}").reshape(big[n].shape)
              for n in big_names]

    wdw_all = _all_devices(_to_rows(w_dw), reduce=False, name="gather_w_dw")[::2]
    wdw_all = wdw_all.reshape(N_CHIPS, -1)[:, :w_dw.size].reshape((N_CHIPS,) + w_dw.shape)
    wdw_full = jnp.moveaxis(wdw_all, 0, 2).reshape(depth, CONV_K, CONV_W)
    wdw_pad = jnp.pad(wdw_full, ((0, 0), (0, 32 - CONV_K), (0, 0)))

    vec = lambda a, l: a[l].reshape(1, -1)

    sh_in, sh_out, sh_up, sh_down = shards
    saved = []
    h = _rms_fwd(xl, vec(pre_mix_g, 0), name="rms_first")
    (g_in,) = _place_own(_run_comm(_forward_comm(_run_comm(_gather_comm(0, [sh_in]), name="gather_w_in_first")),
                                   name="forward_w_in_first"), [sh_in], 0, chip)
    g_up = None
    for l in range(depth):
        more = l + 1 < depth
        bias = _rpb_expand(rpb[l])
        early = [sh_out, sh_up] if l == 0 else [sh_out]
        proj, *parts = _mm_nn(h, g_in, out_dtype=BF16, name="proj", comm=_gather_comm(l, early))
        yc, uc, *got = _conv_fwd(proj, wdw_pad[l], vec(b_dw, l), vec(conv_ln_g, l), vec(conv_ln_b, l),
                                 comm=_forward_comm(parts))
        got = _place_own(got, early, l, chip)
        g_out = got[0].reshape(1, D_MODEL, D_MODEL)
        if l == 0:
            g_up = got[1]
        ya, *parts = _natten_fwd(proj, bias, _gather_comm(l + 1, [sh_in]) if more else None)
        cat = jnp.concatenate([yc, ya], axis=1)
        mix, *got = _mm_nn(cat, g_out, out_dtype=F32, name="out_proj", comm=_forward_comm(parts) if more else None)
        g_in_next = _place_own(got, [sh_in], l + 1, chip)
        x1, h2 = _resid_rms(xl, mix, vec(post_mix_g, l), vec(pre_mlp_g, l), name="mix_resid")
        up, act, part = _mm_nn(h2, g_up, out_dtype=BF16, relu2=True, name="mlp_up", comm=_gather_comm(l, [sh_down]))
        (g_down,) = _place_own(_run_comm(_forward_comm([part]), name="forward_w_down"), [sh_down], l, chip)
        g_down = g_down.reshape(1, D_FF, D_MODEL)
        f, *parts = _mm_nn(act, g_down, out_dtype=F32, name="mlp_down",
                           comm=_gather_comm(l + 1, [sh_up]) if more else None)
        saved.append(dict(x=xl, h=h, proj=proj, uc=uc, cat=cat, mix=mix, x1=x1, h2=h2, up=up, act=act, f=f,
                          bias=bias, w=(g_in, g_out, g_up, g_down)))
        if more:
            xl, h, g_up = _resid_rms(x1, f, vec(post_mlp_g, l), vec(pre_mix_g, l + 1), name="mlp_resid",
                                     comm=_forward_comm(parts))
            (g_up,) = _place_own([g_up], [sh_up], l + 1, chip)
            (g_in,) = g_in_next
        else:
            (xl,) = _resid_rms(x1, f, vec(post_mlp_g, l), None, name="mlp_resid_last")

    dx, loss_blk = _loss_head(xl, target)
    loss = lax.psum(loss_blk[0, 0], ("x", "y", "c"))

    grads = {n: lax.empty(big[n].shape, F32) for n in big_names}
    small = [None] * depth

    def chip_sum(n, layer, part):
        grads[n] = _chip_sum(part, grads[n], layer, name=f"chip_sum_{n}_{layer}")
        return _fill_comm(layer, [grads[n]])

    q_in = None
    for l in reversed(range(depth)):
        sv = saved[l]
        g_in, g_out, g_up, g_down = sv["w"]
        df, dg_post_mlp = _rms_bwd(dx, sv["f"], vec(post_mlp_g, l), None, out_dtype=BF16, name="rms_bwd_mlp_post")
        dup, *part = _mm_nt(df, g_down, out_dtype=BF16, up=sv["up"], name="d_act",
                            comm=_scatter_comm([q_in]) if q_in is not None else None)
        dw_down, *done = _mm_tn(sv["act"], df, 1, name="dw_down",
                                comm=chip_sum("w_in", l + 1, part[0]) if part else None)
        if done:
            grads["w_in"] = done[0]
        dw_down = dw_down.reshape(N_CHIPS, -1, D_MODEL)
        dh2, other = _mm_nt(dup, g_up, out_dtype=F32, name="d_h2", comm=_exchange_comm([dw_down]))
        q_down = _pair_sum(dw_down, other, name="pair_sum_w_down")
        (dw_up,) = _mm_tn(sv["h2"], dup, N_CHIPS, name="dw_up")
        dx1, dg_pre_mlp = _rms_bwd(dh2, sv["x1"], vec(pre_mlp_g, l), dx, out_dtype=F32, name="rms_bwd_mlp_pre")
        dmix, dg_post_mix, other = _rms_bwd(dx1, sv["mix"], vec(post_mix_g, l), None, out_dtype=BF16,
                                            name="rms_bwd_mix_post", comm=_exchange_comm([dw_up]))
        q_up = _pair_sum(dw_up, other, name="pair_sum_w_up")
        dw_out = _mm_tn(sv["cat"], dmix, 1, name="dw_out")[0].reshape(N_CHIPS, -1, D_MODEL)
        dcat, other = _mm_nt(dmix, g_out, out_dtype=F32, name="d_cat", comm=_exchange_comm([dw_out]))
        q_out = _pair_sum(dw_out, other, name="pair_sum_w_out")
        dq, dk, dv, dbias, part = _natten_bwd(sv["proj"], sv["bias"], dcat, _scatter_comm([q_down]))
        chip_sum("w_down", l, part)
        dag, dwdw, dvec, part = _conv_bwd(sv["proj"], sv["uc"], dcat, wdw_pad[l], vec(conv_ln_g, l),
                                          vec(conv_ln_b, l), comm=_scatter_comm([q_up]))
        chip_sum("w_up", l, part)
        dproj = jnp.concatenate([dag, dq, dk, dv], axis=1)
        dh, part = _mm_nt(dproj, g_in, out_dtype=F32, name="d_h", comm=_scatter_comm([q_out]))
        chip_sum("w_out", l, part)
        dw_in, grads["w_down"], grads["w_up"], grads["w_out"] = _mm_tn(
            sv["h"], dproj, N_CHIPS, name="dw_in",
            comm=_fill_comm(l, [grads["w_down"], grads["w_up"], grads["w_out"]]))
        dx, dg_pre_mix, other = _rms_bwd(dh, sv["x"], vec(pre_mix_g, l), dx1, out_dtype=F32,
                                         name="rms_bwd_mix_pre", comm=_exchange_comm([dw_in]))
        q_in = _pair_sum(dw_in, other, name="pair_sum_w_in")

        drpb = _rpb_grad(dbias)
        small[l] = jnp.concatenate([
            dvec[0], dvec[1], dvec[2], dg_pre_mix[0], dg_post_mix[0], dg_pre_mlp[0], dg_post_mlp[0],
            drpb.reshape(-1), dwdw[:CONV_K].reshape(-1)])

    small_sum = _all_devices(_to_rows(jnp.stack(small)), reduce=True, name="reduce_small_grads")
    small_sum = small_sum.reshape(-1)[:depth * small[0].shape[0]].reshape(depth, -1)
    sizes = [CONV_W, CONV_W, CONV_W, D_MODEL, D_MODEL, D_MODEL, D_MODEL, rpb[0].size, CONV_K * CONV_W]
    offs = np.concatenate([[0], np.cumsum(sizes)])
    pieces = [small_sum[:, offs[i]:offs[i + 1]] for i in range(len(sizes))]
    g_small = {
        "b_dw": pieces[0], "conv_ln_g": pieces[1], "conv_ln_b": pieces[2],
        "pre_mix_g": pieces[3], "post_mix_g": pieces[4], "pre_mlp_g": pieces[5], "post_mlp_g": pieces[6],
        "rpb": pieces[7].reshape(rpb.shape),
        "w_dw": lax.dynamic_slice_in_dim(pieces[8].reshape(depth, CONV_K, CONV_W), chip * w_dw.shape[2],
                                         w_dw.shape[2], axis=2),
    }

    given = dict(w_in=(w_in, m_w_in, v_w_in), w_dw=(w_dw, m_w_dw, v_w_dw), b_dw=(b_dw, m_b_dw, v_b_dw),
                 conv_ln_g=(conv_ln_g, m_conv_ln_g, v_conv_ln_g), conv_ln_b=(conv_ln_b, m_conv_ln_b, v_conv_ln_b),
                 rpb=(rpb, m_rpb, v_rpb), w_out=(w_out, m_w_out, v_w_out), w_up=(w_up, m_w_up, v_w_up),
                 w_down=(w_down, m_w_down, v_w_down), pre_mix_g=(pre_mix_g, m_pre_mix_g, v_pre_mix_g),
                 post_mix_g=(post_mix_g, m_post_mix_g, v_post_mix_g), pre_mlp_g=(pre_mlp_g, m_pre_mlp_g, v_pre_mlp_g),
                 post_mlp_g=(post_mlp_g, m_post_mlp_g, v_post_mlp_g))
    results = {}

    (part,) = _run_comm(_scatter_comm([q_in]), name="grad_scatter_w_in_last")
    (grads["w_in"],) = _run_comm(chip_sum("w_in", 0, part), name="grad_fill_w_in_last")
    for n in big_names:
        w, m, v = given[n]
        flat = lambda a: a.reshape(-1, a.shape[-1])
        outs = _adamw(flat(w), flat(grads[n]), flat(m), flat(v), name=f"adamw_{n}")
        results[n] = [o.reshape(w.shape) for o in outs]

    small_names = list(g_small)
    pack = lambda arrs: _to_rows(jnp.concatenate([a.reshape(-1) for a in arrs]))
    outs = _adamw(pack([given[n][0] for n in small_names]), pack([g_small[n] for n in small_names]),
                  pack([given[n][1] for n in small_names]), pack([given[n][2] for n in small_names]),
                  name="adamw_small")
    pos = 0
    for n in small_names:
        w = given[n][0]
        results[n] = [o.reshape(-1)[pos:pos + w.size].reshape(w.shape) for o in outs]
        pos += w.size

    order = ["w_in", "w_dw", "b_dw", "conv_ln_g", "conv_ln_b", "rpb", "w_out", "w_up", "w_down",
             "pre_mix_g", "post_mix_g", "pre_mlp_g", "post_mlp_g"]
    return (loss, dx.reshape(x.shape), *[results[n][0] for n in order], *[results[n][1] for n in order],
            *[results[n][2] for n in order], *[results[n][3] for n in order])
```
